```python
import math
import jax
import jax.numpy as jnp
from jax import lax
import numpy as np

D_MODEL = 1024
BATCH = 8
SEQ = 2048
DEPTH = 4
DEC_BATCH = 128
DEC_SEQ = 8
PAST_LEN = 16384
PAGE_SIZE = 128

N_MIXERS = 4
N_META = 16
EPS = 1e-6
CHUNK = 64

D_POOL = D_MODEL
POOL_WINDOWS = (2, 4, 8, 16)
POOL_GROUP = D_POOL // len(POOL_WINDOWS)
POOL_BUF = max(POOL_WINDOWS) - 1

GLA_HEADS = 4
GLA_DK = D_MODEL // 2 // GLA_HEADS
GLA_DV = D_MODEL // GLA_HEADS
GLA_QK = GLA_HEADS * GLA_DK
GLA_V = GLA_HEADS * GLA_DV
GLA_RANK = 16
GLA_GATE_NORM = 16.0
GLA_IN = 2 * GLA_QK + 2 * GLA_V + GLA_RANK

D_S5 = D_MODEL
S5_GROUP = 16
S5_G = D_S5 // S5_GROUP
S5_N = 64
S5_DT_MIN = 1e-3
S5_DT_MAX = 1e-1

GDN_HEADS = 8
GDN_DK = 128
GDN_DV = 128
GDN_CONV = 4
D_GDN_QK = GDN_HEADS * GDN_DK
D_GDN_V = GDN_HEADS * GDN_DV
D_GDN_QKV = 2 * D_GDN_QK + D_GDN_V
GDN_IN = D_GDN_QKV + D_GDN_V + 2 * GDN_HEADS

N_POOL_LAYERS = len(range(0, DEPTH, N_MIXERS))
N_GLA_LAYERS = len(range(1, DEPTH, N_MIXERS))
N_S5_LAYERS = len(range(2, DEPTH, N_MIXERS))
N_GDN_LAYERS = len(range(3, DEPTH, N_MIXERS))

kernel_name = 'hybrid_pool_gla_s5_gdn_decode_step'


def rmsnorm(x, g):
    xf = x.astype(jnp.float32)
    y = xf * lax.rsqrt(jnp.mean(xf * xf, axis=-1, keepdims=True) + EPS)
    return (y * g.astype(jnp.float32)).astype(x.dtype)


def l2norm(x):
    return x * lax.rsqrt(jnp.sum(x * x, axis=-1, keepdims=True) + EPS)


def to_heads(a, n_heads):
    b, l, _ = a.shape
    return a.reshape(b, l, n_heads, -1).transpose(0, 2, 1, 3).astype(jnp.float32)


def from_heads(a, dtype):
    b, h, l, d = a.shape
    return a.transpose(0, 2, 1, 3).reshape(b, l, h * d).astype(dtype)


def to_chunks(a, c):
    b, h, l = a.shape[:3]
    return jnp.moveaxis(a.reshape((b, h, l // c, c) + a.shape[3:]), 2, 0)


def from_chunks(a):
    n, b, h, c = a.shape[:4]
    return jnp.moveaxis(a, 0, 2).reshape((b, h, n * c) + a.shape[4:])


def run_segments(fn, seg_lens, state, *seqs):
    outs = []
    start = 0
    for ln in seg_lens:
        o, state = fn(*[a[:, :, start:start + ln] for a in seqs], state)
        outs.append(o)
        start += ln
    return jnp.concatenate(outs, axis=2), state


def pool_mixer(xn, buf, n_valid, w_in, w_grp, scale, w_out):
    b, l, _ = xn.shape
    u, gate = jnp.split(xn @ w_in, 2, axis=-1)
    z = jnp.concatenate([buf.astype(jnp.float32), u.astype(jnp.float32)], axis=1)
    cs = jnp.concatenate([jnp.zeros((b, 1, D_POOL), jnp.float32), jnp.cumsum(z, axis=1)], axis=1)
    t = jnp.arange(l, dtype=jnp.float32)
    means = []
    for gi, w in enumerate(POOL_WINDOWS):
        sl = slice(gi * POOL_GROUP, (gi + 1) * POOL_GROUP)
        wsum = cs[:, POOL_BUF + 1:, sl] - cs[:, POOL_BUF + 1 - w:POOL_BUF + 1 - w + l, sl]
        cnt = jnp.minimum(float(w), t + 1.0 + n_valid)
        means.append(wsum / cnt[None, :, None])
    mixed = (jnp.concatenate(means, axis=-1) - z[:, POOL_BUF:]).astype(xn.dtype)
    mixed = jnp.einsum('blgc,gcd->blgd', mixed.reshape(b, l, len(POOL_WINDOWS), POOL_GROUP), w_grp)
    mixed = mixed.reshape(b, l, D_POOL) * scale
    y = (mixed * jax.nn.silu(gate)) @ w_out
    return y, z[:, -POOL_BUF:].astype(buf.dtype)


def gla_chunk_scan(q, k, v, g, s):
    c = math.gcd(q.shape[2], CHUNK)
    causal = jnp.tril(jnp.ones((c, c), dtype=bool))

    def step(s, inp):
        qc, kc, vc, gc = inp
        bc = jnp.cumsum(gc, axis=2)
        bl = bc[:, :, -1:, :]
        qg = qc * jnp.exp(bc)
        kg = kc * jnp.exp(-bc)
        kd = kc * jnp.exp(bl - bc)
        att = jnp.where(causal, jnp.einsum('bhtk,bhsk->bhts', qg, kg), 0.0)
        o = jnp.einsum('bhts,bhsv->bhtv', att, vc) + jnp.einsum('bhtk,bhkv->bhtv', qg, s)
        s = jnp.exp(bl[:, :, 0, :, None]) * s + jnp.einsum('bhsk,bhsv->bhkv', kd, vc)
        return s, o

    s, o = lax.scan(step, s, (to_chunks(q, c), to_chunks(k, c), to_chunks(v, c), to_chunks(g, c)))
    return from_chunks(o), s


def gla_mixer(xn, s0, seg_lens, w_in, w_gk, b_gk, norm_g, w_out):
    q, k, v, gate, glow = jnp.split(
        xn @ w_in, [GLA_QK, 2 * GLA_QK, 2 * GLA_QK + GLA_V, 2 * GLA_QK + 2 * GLA_V], axis=-1)
    gk = jax.nn.log_sigmoid((glow @ w_gk + b_gk).astype(jnp.float32)) / GLA_GATE_NORM
    q = to_heads(q, GLA_HEADS) * GLA_DK ** -0.5
    o, s = run_segments(gla_chunk_scan, seg_lens, s0.astype(jnp.float32),
                        q, to_heads(k, GLA_HEADS), to_heads(v, GLA_HEADS), to_heads(gk, GLA_HEADS))
    o = from_heads(rmsnorm(o, norm_g), xn.dtype)
    return (o * jax.nn.silu(gate)) @ w_out, s.astype(s0.dtype)


def s5_combine(e1, e2):
    a1r, a1i, b1r, b1i = e1
    a2r, a2i, b2r, b2i = e2
    return (a1r * a2r - a1i * a2i, a1r * a2i + a1i * a2r,
            a2r * b1r - a2i * b1i + b2r, a2r * b1i + a2i * b1r + b2i)


def s5_mixer(xn, h0r, h0i, w_in, b_re, b_im, c_re, c_im, d_skip, log_dt, a_re, a_im, w_glu, b_glu, w_out):
    f32 = jnp.float32
    b, l, _ = xn.shape
    u, gate = jnp.split(xn @ w_in, 2, axis=-1)
    uf = u.astype(f32).reshape(b, l, S5_G, S5_GROUP)
    dt = jnp.exp(log_dt.astype(f32))[:, None]
    lr = a_re.astype(f32)
    li = a_im.astype(f32)
    mag = jnp.exp(lr * dt)
    abr = mag * jnp.cos(li * dt)
    abi = mag * jnp.sin(li * dt)
    den = lr * lr + li * li
    cr = ((abr - 1.0) * lr + abi * li) / den
    ci = (abi * lr - (abr - 1.0) * li) / den
    br = b_re.astype(f32)
    bi = b_im.astype(f32)
    bbr = cr[..., None] * br - ci[..., None] * bi
    bbi = cr[..., None] * bi + ci[..., None] * br
    bur = jnp.einsum('blgc,gnc->blgn', uf, bbr)
    bui = jnp.einsum('blgc,gnc->blgn', uf, bbi)
    a_seq_r = jnp.broadcast_to(abr, (1, l, S5_G, S5_N))
    a_seq_i = jnp.broadcast_to(abi, (1, l, S5_G, S5_N))
    acr, aci, hr, hi = lax.associative_scan(s5_combine, (a_seq_r, a_seq_i, bur, bui), axis=1)
    h0r_ = h0r.astype(f32)[:, None]
    h0i_ = h0i.astype(f32)[:, None]
    hr = hr + acr * h0r_ - aci * h0i_
    hi = hi + acr * h0i_ + aci * h0r_
    y = (jnp.einsum('blgn,gcn->blgc', hr, c_re.astype(f32))
         - jnp.einsum('blgn,gcn->blgc', hi, c_im.astype(f32)))
    y = (y.reshape(b, l, D_S5) + d_skip.astype(f32) * u.astype(f32)).astype(xn.dtype)
    z = jax.nn.gelu(y)
    z = z * jax.nn.sigmoid(z @ w_glu + b_glu)
    return (z * jax.nn.silu(gate)) @ w_out, hr[:, -1].astype(h0r.dtype), hi[:, -1].astype(h0i.dtype)


def gdn_chunk_scan(q, k, v, g, beta, s):
    c = math.gcd(q.shape[2], CHUNK)
    incl = jnp.tril(jnp.ones((c, c), dtype=bool))
    strict = jnp.tril(jnp.ones((c, c), dtype=bool), -1)
    eye = jnp.eye(c, dtype=jnp.float32)

    def step(s, inp):
        qc, kc, vc, gc, bc = inp
        gcum = jnp.cumsum(gc, axis=-1)
        decay = jnp.exp(jnp.where(incl, gcum[..., :, None] - gcum[..., None, :], -jnp.inf))
        kb = kc * bc[..., None]
        lower = jnp.where(strict, jnp.einsum('bhtk,bhsk->bhts', kb, kc) * decay, 0.0)
        tmat = lower + eye
        u = lax.linalg.triangular_solve(tmat, vc * bc[..., None], left_side=True, lower=True, unit_diagonal=True)
        w = lax.linalg.triangular_solve(tmat, kb * jnp.exp(gcum)[..., None], left_side=True, lower=True,
                                        unit_diagonal=True)
        v_new = u - jnp.einsum('bhtk,bhkv->bhtv', w, s)
        att = jnp.where(incl, jnp.einsum('bhtk,bhsk->bhts', qc, kc) * decay, 0.0)
        o = (jnp.einsum('bhtk,bhkv->bhtv', qc * jnp.exp(gcum)[..., None], s)
             + jnp.einsum('bhts,bhsv->bhtv', att, v_new))
        k_end = kc * jnp.exp(gcum[..., -1:] - gcum)[..., None]
        s = jnp.exp(gcum[..., -1])[..., None, None] * s + jnp.einsum('bhsk,bhsv->bhkv', k_end, v_new)
        return s, o

    s, o = lax.scan(step, s, tuple(to_chunks(a, c) for a in (q, k, v, g, beta)))
    return from_chunks(o), s


def gdn_mixer(xn, s0, conv_buf, seg_lens, w_in, conv_w, a_log, dt_bias, norm_g, w_out):
    f32 = jnp.float32
    qkv, z, a, bt = jnp.split(
        xn @ w_in, [D_GDN_QKV, D_GDN_QKV + D_GDN_V, D_GDN_QKV + D_GDN_V + GDN_HEADS], axis=-1)
    ext = jnp.concatenate([conv_buf.astype(qkv.dtype), qkv], axis=1)
    conv = lax.conv_general_dilated(ext, conv_w[:, None, :].astype(qkv.dtype), window_strides=(1,),
                                    padding='VALID', dimension_numbers=('NWC', 'WIO', 'NWC'),
                                    feature_group_count=D_GDN_QKV)
    q, k, v = jnp.split(jax.nn.silu(conv), [D_GDN_QK, 2 * D_GDN_QK], axis=-1)
    q = l2norm(to_heads(q, GDN_HEADS)) * GDN_DK ** -0.5
    k = l2norm(to_heads(k, GDN_HEADS))
    v = to_heads(v, GDN_HEADS)
    g = -jnp.exp(a_log.astype(f32)) * jax.nn.softplus(a.astype(f32) + dt_bias.astype(f32))
    beta = jax.nn.sigmoid(bt.astype(f32))
    o, s = run_segments(gdn_chunk_scan, seg_lens, s0.astype(f32), q, k, v,
                        g.transpose(0, 2, 1), beta.transpose(0, 2, 1))
    o = from_heads(rmsnorm(o, norm_g), xn.dtype)
    return ((o * jax.nn.silu(z)) @ w_out, s.astype(s0.dtype),
            ext[:, -(GDN_CONV - 1):].astype(conv_buf.dtype))


def setup_inputs(seed: int = 0) -> dict:
    key = jax.random.key(seed)
    keys = iter(jax.random.split(key, 48))
    f32 = jnp.float32

    def nrm(shape, s=1.0):
        return s * jax.random.normal(next(keys), shape, f32)

    def wt(shape, fan_in):
        return jax.random.normal(next(keys), shape, f32) * fan_in ** -0.5

    def gain(shape):
        return 1.0 + 0.02 * jax.random.normal(next(keys), shape, f32)

    nA, nB, nC, nD = N_POOL_LAYERS, N_GLA_LAYERS, N_S5_LAYERS, N_GDN_LAYERS
    inp = {}
    inp['x_prompt'] = nrm((BATCH, SEQ, D_MODEL))
    inp['x_sample'] = nrm((DEC_BATCH, DEC_SEQ, D_MODEL))
    inp['state_pool'] = nrm((nA, DEC_BATCH, POOL_BUF, D_POOL))
    inp['state_gla'] = nrm((nB, DEC_BATCH, GLA_HEADS, GLA_DK, GLA_DV), 0.1)
    inp['state_s5_re'] = nrm((nC, DEC_BATCH, S5_G, S5_N), 0.1)
    inp['state_s5_im'] = nrm((nC, DEC_BATCH, S5_G, S5_N), 0.1)
    inp['state_gdn'] = nrm((nD, DEC_BATCH, GDN_HEADS, GDN_DK, GDN_DV), 0.1)
    inp['state_gdn_conv'] = nrm((nD, DEC_BATCH, GDN_CONV - 1, D_GDN_QKV))
    inp['meta_tokens'] = nrm((N_META, D_MODEL))
    inp['norm_g'] = gain((DEPTH, D_MODEL))
    inp['final_norm_g'] = gain((D_MODEL,))
    inp['pool_w_in'] = wt((nA, D_MODEL, 2 * D_POOL), D_MODEL)
    inp['pool_w_grp'] = wt((nA, len(POOL_WINDOWS), POOL_GROUP, POOL_GROUP), POOL_GROUP)
    inp['pool_scale'] = gain((nA, D_POOL))
    inp['pool_w_out'] = wt((nA, D_POOL, D_MODEL), D_POOL)
    inp['gla_w_in'] = wt((nB, D_MODEL, GLA_IN), D_MODEL)
    inp['gla_w_gk'] = wt((nB, GLA_RANK, GLA_QK), GLA_RANK)
    inp['gla_b_gk'] = nrm((nB, GLA_QK), 0.1)
    inp['gla_norm_g'] = gain((nB, GLA_DV))
    inp['gla_w_out'] = wt((nB, GLA_V, D_MODEL), GLA_V)
    inp['s5_w_in'] = wt((nC, D_MODEL, 2 * D_S5), D_MODEL)
    inp['s5_b_re'] = wt((nC, S5_G, S5_N, S5_GROUP), 2 * S5_GROUP)
    inp['s5_b_im'] = wt((nC, S5_G, S5_N, S5_GROUP), 2 * S5_GROUP)
    inp['s5_c_re'] = wt((nC, S5_G, S5_GROUP, S5_N), 2 * S5_N)
    inp['s5_c_im'] = wt((nC, S5_G, S5_GROUP, S5_N), 2 * S5_N)
    inp['s5_d'] = nrm((nC, D_S5))
    inp['s5_log_dt'] = jax.random.uniform(next(keys), (nC, S5_G), f32,
                                          minval=math.log(S5_DT_MIN), maxval=math.log(S5_DT_MAX))
    inp['s5_a_re'] = -0.5 + nrm((nC, S5_G, S5_N), 0.01)
    inp['s5_a_im'] = jnp.pi * jnp.arange(S5_N, dtype=f32) + nrm((nC, S5_G, S5_N), 0.01)
    inp['s5_w_glu'] = wt((nC, D_S5, D_S5), D_S5)
    inp['s5_b_glu'] = nrm((nC, D_S5), 0.01)
    inp['s5_w_out'] = wt((nC, D_S5, D_MODEL), D_S5)
    inp['gdn_w_in'] = wt((nD, D_MODEL, GDN_IN), D_MODEL)
    inp['gdn_conv_w'] = wt((nD, GDN_CONV, D_GDN_QKV), GDN_CONV)
    inp['gdn_a_log'] = jnp.log(jax.random.uniform(next(keys), (nD, GDN_HEADS), f32, minval=1.0, maxval=16.0))
    dt = jnp.exp(jax.random.uniform(next(keys), (nD, GDN_HEADS), f32,
                                    minval=math.log(1e-3), maxval=math.log(1e-1)))
    inp['gdn_dt_bias'] = dt + jnp.log(-jnp.expm1(-dt))
    inp['gdn_norm_g'] = gain((nD, GDN_DV))
    inp['gdn_w_out'] = wt((nD, D_GDN_V, D_MODEL), D_GDN_V)
    return inp


def reference(x_prompt, x_sample, state_pool, state_gla, state_s5_re, state_s5_im, state_gdn, state_gdn_conv,
              meta_tokens, norm_g, final_norm_g,
              pool_w_in, pool_w_grp, pool_scale, pool_w_out,
              gla_w_in, gla_w_gk, gla_b_gk, gla_norm_g, gla_w_out,
              s5_w_in, s5_b_re, s5_b_im, s5_c_re, s5_c_im, s5_d, s5_log_dt, s5_a_re, s5_a_im,
              s5_w_glu, s5_b_glu, s5_w_out,
              gdn_w_in, gdn_conv_w, gdn_a_log, gdn_dt_bias, gdn_norm_g, gdn_w_out):
    bp, sp_len, _ = x_prompt.shape
    bs, ss_len, _ = x_sample.shape
    dt_p = x_prompt.dtype
    meta = jnp.broadcast_to(meta_tokens.astype(dt_p)[None], (bp, N_META, D_MODEL))
    hp = jnp.concatenate([meta, x_prompt], axis=1)
    hs = x_sample
    segs_p = (N_META, sp_len)
    segs_s = (ss_len,)
    pool_p, pool_s, gla_p, gla_s = [], [], [], []
    s5r_p, s5i_p, s5r_s, s5i_s = [], [], [], []
    gdn_p, gdnc_p, gdn_s, gdnc_s = [], [], [], []
    for i in range(DEPTH):
        m, j = i % N_MIXERS, i // N_MIXERS
        xp = rmsnorm(hp, norm_g[i])
        xs = rmsnorm(hs, norm_g[i])
        if m == 0:
            prm = (pool_w_in[j], pool_w_grp[j], pool_scale[j], pool_w_out[j])
            yp, st_p = pool_mixer(xp, jnp.zeros((bp, POOL_BUF, D_POOL), dt_p), 0, *prm)
            ys, st_s = pool_mixer(xs, state_pool[j], POOL_BUF, *prm)
            pool_p.append(st_p)
            pool_s.append(st_s)
        elif m == 1:
            prm = (gla_w_in[j], gla_w_gk[j], gla_b_gk[j], gla_norm_g[j], gla_w_out[j])
            yp, st_p = gla_mixer(xp, jnp.zeros((bp, GLA_HEADS, GLA_DK, GLA_DV), dt_p), segs_p, *prm)
            ys, st_s = gla_mixer(xs, state_gla[j], segs_s, *prm)
            gla_p.append(st_p)
            gla_s.append(st_s)
        elif m == 2:
            prm = (s5_w_in[j], s5_b_re[j], s5_b_im[j], s5_c_re[j], s5_c_im[j], s5_d[j], s5_log_dt[j],
                   s5_a_re[j], s5_a_im[j], s5_w_glu[j], s5_b_glu[j], s5_w_out[j])
            z0 = jnp.zeros((bp, S5_G, S5_N), dt_p)
            yp, hr_p, hi_p = s5_mixer(xp, z0, z0, *prm)
            ys, hr_s, hi_s = s5_mixer(xs, state_s5_re[j], state_s5_im[j], *prm)
            s5r_p.append(hr_p)
            s5i_p.append(hi_p)
            s5r_s.append(hr_s)
            s5i_s.append(hi_s)
        else:
            prm = (gdn_w_in[j], gdn_conv_w[j], gdn_a_log[j], gdn_dt_bias[j], gdn_norm_g[j], gdn_w_out[j])
            yp, st_p, cb_p = gdn_mixer(xp, jnp.zeros((bp, GDN_HEADS, GDN_DK, GDN_DV), dt_p),
                                       jnp.zeros((bp, GDN_CONV - 1, D_GDN_QKV), dt_p), segs_p, *prm[:2],
                                       *prm[2:])
            ys, st_s, cb_s = gdn_mixer(xs, state_gdn[j], state_gdn_conv[j], segs_s, *prm)
            gdn_p.append(st_p)
            gdnc_p.append(cb_p)
            gdn_s.append(st_s)
            gdnc_s.append(cb_s)
        hp = hp + yp
        hs = hs + ys
    y_prompt = rmsnorm(hp, final_norm_g)[:, N_META:]
    y_sample = rmsnorm(hs, final_norm_g)
    return (y_prompt, y_sample,
            jnp.stack(pool_p), jnp.stack(pool_s),
            jnp.stack(gla_p), jnp.stack(gla_s),
            jnp.stack(s5r_p), jnp.stack(s5i_p), jnp.stack(s5r_s), jnp.stack(s5i_s),
            jnp.stack(gdn_p), jnp.stack(gdnc_p), jnp.stack(gdn_s), jnp.stack(gdnc_s))
```

```python
import functools
import math

import jax
import jax.numpy as jnp
from jax import lax
from jax.experimental import pallas as pl
from jax.experimental.pallas import tpu as pltpu

F32 = jnp.float32
BF16 = jnp.bfloat16
HIGHEST = lax.Precision.HIGHEST

D_MODEL = 1024
EPS = 1e-6
N_META = 16

POOL_WINDOWS = (2, 4, 8, 16)
POOL_GROUP = D_MODEL // len(POOL_WINDOWS)
POOL_BUF = max(POOL_WINDOWS) - 1
POOL_PAD = POOL_BUF + 1

GLA_HEADS = 4
GLA_DK = 128
GLA_DV = 256
GLA_QK = GLA_HEADS * GLA_DK
GLA_V = GLA_HEADS * GLA_DV
GLA_RANK = 16
GLA_GATE_NORM = 16.0

S5_GROUP = 16
S5_G = D_MODEL // S5_GROUP
S5_N = 64
S5_BUNDLE = 8
S5_NB = S5_G // S5_BUNDLE
S5_HALF = S5_BUNDLE * S5_N
S5_STATE = 2 * S5_G * S5_N

GDN_HEADS = 8
GDN_DK = 128
GDN_DV = 128
GDN_CONV = 4
GDN_QK = GDN_HEADS * GDN_DK
GDN_V = GDN_HEADS * GDN_DV
GDN_QKV = 2 * GDN_QK + GDN_V
GDN_PAD = 8

LANES = 128
SUBLANES = 8
VMEM_LIMIT = 52 * 1024 * 1024

_NT = (((1,), (1,)), ((), ()))
_TN = (((0,), (0,)), ((), ()))


def _rms_rows(x, g):
    return x * lax.rsqrt(jnp.mean(x * x, axis=-1, keepdims=True) + EPS) * g


def _mm(a, w):
    return jnp.dot(a.astype(BF16), w, preferred_element_type=F32)


def _cast_small(c, *xs):
    if c % 16 == 0:
        return tuple(x.astype(BF16) for x in xs)
    return xs


def _tri_masks(c):
    r = lax.broadcasted_iota(jnp.int32, (c, c), 0)
    s = lax.broadcasted_iota(jnp.int32, (c, c), 1)
    return r >= s, r > s, r == s


def _pool_kernel(n_valid, h_ref, buf_ref, g_ref, win_ref, wgrp_ref, scale_ref, wout_ref,
                 o_ref, st_ref, z_ref):
    bb, tl, _ = h_ref.shape
    rows = bb * tl
    l = pl.program_id(1)

    @pl.when(l == 0)
    def _():
        z_ref[:, 0:POOL_PAD, :] = buf_ref[...]

    h = h_ref[...].reshape(rows, D_MODEL)
    xn = _rms_rows(h, g_ref[...])
    ug = _mm(xn, win_ref[...])
    gate = ug[:, D_MODEL:]
    z_ref[:, POOL_PAD:POOL_PAD + tl, :] = ug[:, :D_MODEL].reshape(bb, tl, D_MODEL)

    t = (l * tl + lax.broadcasted_iota(jnp.int32, (1, tl, POOL_GROUP), 1)).astype(F32)
    parts = []
    for gi, w in enumerate(POOL_WINDOWS):
        lo = gi * POOL_GROUP
        cur = z_ref[:, POOL_PAD:POOL_PAD + tl, lo:lo + POOL_GROUP]
        acc = cur
        for j in range(1, w):
            acc = acc + z_ref[:, POOL_PAD - j:POOL_PAD - j + tl, lo:lo + POOL_GROUP]
        cnt = jnp.minimum(float(w), t + (1.0 + n_valid))
        mixed = acc / cnt - cur
        parts.append(_mm(mixed.reshape(rows, POOL_GROUP), wgrp_ref[gi]))
    mixed = jnp.concatenate(parts, axis=-1) * scale_ref[...]
    y = _mm(mixed * jax.nn.silu(gate), wout_ref[...])
    o_ref[...] = (h + y).reshape(bb, tl, D_MODEL)

    @pl.when(l == pl.num_programs(1) - 1)
    def _():
        st_ref[...] = z_ref[:, tl + 1:tl + POOL_PAD, :]

    z_ref[:, 0:POOL_PAD, :] = z_ref[:, tl:tl + POOL_PAD, :]


def _const_spec(shape):
    nd = len(shape)
    return pl.BlockSpec(shape, lambda b, l: (0,) * nd)


def _state_spec(block, shared):
    nd = len(block)
    if shared:
        return pl.BlockSpec(block, lambda b, l: (0,) * nd)
    return pl.BlockSpec(block, lambda b, l: (b,) + (0,) * (nd - 1))


def _params():
    return pltpu.CompilerParams(dimension_semantics=("parallel", "arbitrary"),
                                vmem_limit_bytes=VMEM_LIMIT)


def _pool_layer(h, buf, n_valid, w, bb, tl):
    bsz, seq, _ = h.shape
    shared = buf.shape[0] != bsz
    hspec = pl.BlockSpec((bb, tl, D_MODEL), lambda b, l: (b, l, 0))
    return pl.pallas_call(
        functools.partial(_pool_kernel, float(n_valid)),
        grid=(bsz // bb, seq // tl),
        in_specs=[hspec, _state_spec((bb, POOL_PAD, D_MODEL), shared),
                  _const_spec((1, D_MODEL)), _const_spec((D_MODEL, 2 * D_MODEL)),
                  _const_spec((len(POOL_WINDOWS), POOL_GROUP, POOL_GROUP)),
                  _const_spec((1, D_MODEL)), _const_spec((D_MODEL, D_MODEL))],
        out_specs=[hspec, pl.BlockSpec((bb, POOL_BUF, D_MODEL), lambda b, l: (b, 0, 0))],
        out_shape=[jax.ShapeDtypeStruct(h.shape, F32),
                   jax.ShapeDtypeStruct((bsz, POOL_BUF, D_MODEL), F32)],
        scratch_shapes=[pltpu.VMEM((bb, POOL_PAD + tl, D_MODEL), F32)],
        compiler_params=_params(),
        name="pool_layer",
    )(h, buf, w["g"], w["w_in"], w["w_grp"], w["scale"], w["w_out"])


def _gla_kernel(c, h_ref, s0_ref, g_ref, wq_ref, wk_ref, wv_ref, wgate_ref, wglow_ref, wgk_ref,
                bgk_ref, ng_ref, wout_ref, o_ref, sout_ref, s_ref, q_s, k_s, v_s, gk_s, o_s):
    bb, tl, _ = h_ref.shape
    rows = bb * tl
    nch = tl // c
    l = pl.program_id(1)

    @pl.when(l == 0)
    def _():
        s_ref[...] = jnp.broadcast_to(s0_ref[...], s_ref.shape)

    h = h_ref[...].reshape(rows, D_MODEL)
    xn = _rms_rows(h, g_ref[...]).astype(BF16)
    q_s[...] = _mm(xn, wq_ref[...]) * GLA_DK ** -0.5
    k_s[...] = _mm(xn, wk_ref[...])
    v_s[...] = _mm(xn, wv_ref[...])
    glow = _mm(xn, wglow_ref[...])
    gk_s[...] = jax.nn.log_sigmoid(_mm(glow, wgk_ref[...]) + bgk_ref[...]) / GLA_GATE_NORM

    incl, _, _ = _tri_masks(c)
    tri = incl.astype(F32)
    ones = jnp.ones((c, GLA_DK), F32)

    def chunk(it, carry):
        r0 = pl.multiple_of(it * c, c)
        b = it // nch
        for hd in range(GLA_HEADS):
            ksl = slice(hd * GLA_DK, (hd + 1) * GLA_DK)
            vsl = slice(hd * GLA_DV, (hd + 1) * GLA_DV)
            g = gk_s[pl.ds(r0, c), ksl]
            bc = jnp.dot(tri, g, precision=HIGHEST, preferred_element_type=F32)
            bl = bc[c - 1:c, :]
            qc = q_s[pl.ds(r0, c), ksl]
            kc = k_s[pl.ds(r0, c), ksl]
            vc = v_s[pl.ds(r0, c), vsl]
            qg = qc * jnp.exp(bc)
            kg = kc * jnp.exp(-bc)
            kd = kc * jnp.exp(bl - bc)
            s = s_ref[b, hd]
            qg_, kg_, kd_, vc_, s_ = _cast_small(c, qg, kg, kd, vc, s)
            att = jnp.where(incl, lax.dot_general(qg_, kg_, _NT, preferred_element_type=F32), 0.0)
            (att_,) = _cast_small(c, att)
            o = (jnp.dot(att_, vc_, preferred_element_type=F32)
                 + jnp.dot(qg_, s_, preferred_element_type=F32))
            ebl = jnp.exp(lax.dot_general(g, ones, _TN, precision=HIGHEST, preferred_element_type=F32))
            s_ref[b, hd] = (jnp.concatenate([ebl] * (GLA_DV // GLA_DK), axis=1) * s
                            + lax.dot_general(kd_, vc_, _TN, preferred_element_type=F32))
            o_s[pl.ds(r0, c), vsl] = o
        return carry

    lax.fori_loop(0, bb * nch, chunk, 0)

    gate = _mm(xn, wgate_ref[...])
    parts = []
    for hd in range(GLA_HEADS):
        parts.append(_rms_rows(o_s[:, hd * GLA_DV:(hd + 1) * GLA_DV], ng_ref[...]))
    on = jnp.concatenate(parts, axis=-1)
    y = _mm(on * jax.nn.silu(gate), wout_ref[...])
    o_ref[...] = (h + y).reshape(bb, tl, D_MODEL)

    @pl.when(l == pl.num_programs(1) - 1)
    def _():
        sout_ref[...] = s_ref[...]


def _gla_layer(h, s0, c, w, bb, tl):
    bsz, seq, _ = h.shape
    shared = s0.shape[0] != bsz
    sblock = (bb, GLA_HEADS, GLA_DK, GLA_DV)
    hspec = pl.BlockSpec((bb, tl, D_MODEL), lambda b, l: (b, l, 0))
    rows = bb * tl
    return pl.pallas_call(
        functools.partial(_gla_kernel, c),
        grid=(bsz // bb, seq // tl),
        in_specs=[hspec, _state_spec((1,) + sblock[1:] if shared else sblock, shared),
                  _const_spec((1, D_MODEL)),
                  _const_spec((D_MODEL, GLA_QK)), _const_spec((D_MODEL, GLA_QK)),
                  _const_spec((D_MODEL, GLA_V)), _const_spec((D_MODEL, GLA_V)),
                  _const_spec((D_MODEL, LANES)), _const_spec((LANES, GLA_QK)),
                  _const_spec((1, GLA_QK)), _const_spec((1, GLA_DV)),
                  _const_spec((GLA_V, D_MODEL))],
        out_specs=[hspec, pl.BlockSpec(sblock, lambda b, l: (b, 0, 0, 0))],
        out_shape=[jax.ShapeDtypeStruct(h.shape, F32),
                   jax.ShapeDtypeStruct((bsz,) + sblock[1:], F32)],
        scratch_shapes=[pltpu.VMEM(sblock, F32),
                        pltpu.VMEM((rows, GLA_QK), F32), pltpu.VMEM((rows, GLA_QK), F32),
                        pltpu.VMEM((rows, GLA_V), F32), pltpu.VMEM((rows, GLA_QK), F32),
                        pltpu.VMEM((rows, GLA_V), F32)],
        compiler_params=_params(),
        name="gla_layer",
    )(h, s0, w["g"], w["wq"], w["wk"], w["wv"], w["wgate"], w["wglow"], w["wgk"], w["bgk"],
      w["ng"], w["w_out"])


def _s5_prep_kernel(logdt_ref, are_ref, aim_ref, bre_ref, bim_ref, abr_ref, abi_ref, bbr_ref, bbi_ref):
    dt = jnp.exp(logdt_ref[...])
    lr = are_ref[...]
    li = aim_ref[...]
    mag = jnp.exp(lr * dt)
    abr = mag * jnp.cos(li * dt)
    abi = mag * jnp.sin(li * dt)
    den = lr * lr + li * li
    cr = ((abr - 1.0) * lr + abi * li) / den
    ci = (abi * lr - (abr - 1.0) * li) / den
    abr_ref[...] = abr
    abi_ref[...] = abi
    br = bre_ref[...]
    bi = bim_ref[...]
    bbr_ref[...] = cr[:, None, :] * br - ci[:, None, :] * bi
    bbi_ref[...] = cr[:, None, :] * bi + ci[:, None, :] * br


def _s5_prep(log_dt, a_re, a_im, b_re, b_im):
    gn = jax.ShapeDtypeStruct((S5_G, S5_N), F32)
    gcn = jax.ShapeDtypeStruct((S5_G, S5_GROUP, S5_N), F32)
    return pl.pallas_call(_s5_prep_kernel, out_shape=[gn, gn, gcn, gcn], name="s5_discretize")(
        log_dt.reshape(S5_G, 1), a_re, a_im, b_re.transpose(0, 2, 1), b_im.transpose(0, 2, 1))


def _s5_kernel(h_ref, h0_ref, g_ref, win_ref, wb_ref, wc_ref, abr_ref, abi_ref, dskip_ref,
               wglu_ref, bglu_ref, wout_ref, o_ref, hout_ref, st_ref, hs_ref):
    tl, nb, _ = h_ref.shape
    rows = tl * nb
    ngrp = nb // SUBLANES
    l = pl.program_id(1)

    @pl.when(l == 0)
    def _():
        hs_ref[...] = h0_ref[...]

    h = h_ref[...].reshape(rows, D_MODEL)
    xn = _rms_rows(h, g_ref[...]).astype(BF16)
    ug = _mm(xn, win_ref[...])
    u = ug[:, :D_MODEL]
    gate = ug[:, D_MODEL:]
    ub = u.astype(BF16)
    width = 2 * S5_HALF
    in_w = S5_BUNDLE * S5_GROUP
    for j in range(S5_NB):
        st_ref[:, j * width:(j + 1) * width] = jnp.dot(
            ub[:, j * in_w:(j + 1) * in_w], wb_ref[j], preferred_element_type=F32)

    for j in range(S5_NB):
        re = slice(j * width, j * width + S5_HALF)
        im = slice(j * width + S5_HALF, (j + 1) * width)
        ar = jnp.broadcast_to(abr_ref[:, j * S5_HALF:(j + 1) * S5_HALF], (SUBLANES, S5_HALF))
        ai = jnp.broadcast_to(abi_ref[:, j * S5_HALF:(j + 1) * S5_HALF], (SUBLANES, S5_HALF))

        def group(bg, carry, re=re, im=im, ar=ar, ai=ai):
            r00 = pl.multiple_of(bg * SUBLANES, SUBLANES)

            def step(t, hc):
                hr, hi = hc
                r = pl.multiple_of(t * nb + r00, SUBLANES)
                nhr = ar * hr - ai * hi + st_ref[pl.ds(r, SUBLANES), re]
                nhi = ar * hi + ai * hr + st_ref[pl.ds(r, SUBLANES), im]
                st_ref[pl.ds(r, SUBLANES), re] = nhr
                st_ref[pl.ds(r, SUBLANES), im] = nhi
                return nhr, nhi

            hr, hi = lax.fori_loop(
                0, tl, step, (hs_ref[pl.ds(r00, SUBLANES), re], hs_ref[pl.ds(r00, SUBLANES), im]))
            hs_ref[pl.ds(r00, SUBLANES), re] = hr
            hs_ref[pl.ds(r00, SUBLANES), im] = hi
            return carry

        lax.fori_loop(0, ngrp, group, 0)

    parts = []
    for j in range(S5_NB):
        parts.append(_mm(st_ref[:, j * width:(j + 1) * width], wc_ref[j]))
    y = jnp.concatenate(parts, axis=-1) + dskip_ref[...] * u
    z = jax.nn.gelu(y)
    z = z * jax.nn.sigmoid(_mm(z, wglu_ref[...]) + bglu_ref[...])
    out = h + _mm(z * jax.nn.silu(gate), wout_ref[...])
    o_ref[...] = out.reshape(tl, nb, D_MODEL)

    @pl.when(l == pl.num_programs(1) - 1)
    def _():
        hout_ref[...] = hs_ref[...]


def _s5_layer(h_tb, h0, w, nb, tl):
    seq, bsz, _ = h_tb.shape
    hspec = pl.BlockSpec((tl, nb, D_MODEL), lambda b, l: (l, b, 0))
    sspec = pl.BlockSpec((nb, S5_STATE), lambda b, l: (b, 0))
    in_w = S5_BUNDLE * S5_GROUP
    return pl.pallas_call(
        _s5_kernel,
        grid=(bsz // nb, seq // tl),
        in_specs=[hspec, sspec, _const_spec((1, D_MODEL)), _const_spec((D_MODEL, 2 * D_MODEL)),
                  _const_spec((S5_NB, in_w, 2 * S5_HALF)), _const_spec((S5_NB, 2 * S5_HALF, in_w)),
                  _const_spec((1, S5_G * S5_N)), _const_spec((1, S5_G * S5_N)),
                  _const_spec((1, D_MODEL)), _const_spec((D_MODEL, D_MODEL)),
                  _const_spec((1, D_MODEL)), _const_spec((D_MODEL, D_MODEL))],
        out_specs=[hspec, sspec],
        out_shape=[jax.ShapeDtypeStruct(h_tb.shape, F32), jax.ShapeDtypeStruct((bsz, S5_STATE), F32)],
        scratch_shapes=[pltpu.VMEM((tl * nb, S5_STATE), F32), pltpu.VMEM((nb, S5_STATE), F32)],
        compiler_params=_params(),
        name="s5_layer",
    )(h_tb, h0, w["g"], w["w_in"], w["wb"], w["wc"], w["abr"], w["abi"], w["d"], w["w_glu"],
      w["b_glu"], w["w_out"])


def _s5_pack_state(re, im):
    b = re.shape[0]
    return jnp.concatenate([re.reshape(b, S5_NB, S5_HALF), im.reshape(b, S5_NB, S5_HALF)],
                           axis=-1).reshape(b, S5_STATE)


def _s5_unpack_state(st):
    b = st.shape[0]
    st = st.reshape(b, S5_NB, 2 * S5_HALF)
    return (st[..., :S5_HALF].reshape(b, S5_G, S5_N), st[..., S5_HALF:].reshape(b, S5_G, S5_N))


def _gdn_kernel(c, h_ref, s0_ref, cb_ref, g_ref, wqkv_ref, wz_ref, wab_ref, convw_ref, alog_ref,
                dtb_ref, ng_ref, fg_ref, wout_ref, o_ref, sout_ref, cbout_ref,
                s_ref, ext_ref, q_s, k_s, v_s, g_s, b_s, o_s):
    bb, tl, _ = h_ref.shape
    rows = bb * tl
    nch = tl // c
    l = pl.program_id(1)

    @pl.when(l == 0)
    def _():
        s_ref[...] = jnp.broadcast_to(s0_ref[...], s_ref.shape)
        ext_ref[:, 0:GDN_PAD, :] = jnp.broadcast_to(cb_ref[...], (bb, GDN_PAD, GDN_QKV))

    h = h_ref[...].reshape(rows, D_MODEL)
    xn = _rms_rows(h, g_ref[...]).astype(BF16)
    ext_ref[:, GDN_PAD:GDN_PAD + tl, :] = _mm(xn, wqkv_ref[...]).reshape(bb, tl, GDN_QKV)
    first = GDN_PAD - (GDN_CONV - 1)
    conv = convw_ref[0:1, :] * ext_ref[:, first:first + tl, :]
    for j in range(1, GDN_CONV):
        conv = conv + convw_ref[j:j + 1, :] * ext_ref[:, first + j:first + j + tl, :]
    act = jax.nn.silu(conv).reshape(rows, GDN_QKV)
    for hd in range(GDN_HEADS):
        sl = slice(hd * GDN_DK, (hd + 1) * GDN_DK)
        qh = act[:, hd * GDN_DK:(hd + 1) * GDN_DK]
        kh = act[:, GDN_QK + hd * GDN_DK:GDN_QK + (hd + 1) * GDN_DK]
        q_s[:, sl] = qh * lax.rsqrt(jnp.sum(qh * qh, axis=-1, keepdims=True) + EPS) * GDN_DK ** -0.5
        k_s[:, sl] = kh * lax.rsqrt(jnp.sum(kh * kh, axis=-1, keepdims=True) + EPS)
    v_s[...] = act[:, 2 * GDN_QK:]
    ab = _mm(xn, wab_ref[...])
    g_s[...] = -jnp.exp(alog_ref[...]) * jax.nn.softplus(ab + dtb_ref[...])
    b_s[...] = jax.nn.sigmoid(ab)

    incl, strict, diag = _tri_masks(c)
    tri = incl.astype(F32)
    eye = diag.astype(F32)
    n_sq = int(math.log2(c)) - 1

    def chunk(it, carry):
        r0 = pl.multiple_of(it * c, c)
        b = it // nch
        gcum = jnp.dot(tri, g_s[pl.ds(r0, c), :], precision=HIGHEST, preferred_element_type=F32)
        beta = b_s[pl.ds(r0, c), :]
        for hd in range(GDN_HEADS):
            sl = slice(hd * GDN_DK, (hd + 1) * GDN_DK)
            gc = gcum[:, hd:hd + 1]
            bcol = beta[:, GDN_HEADS + hd:GDN_HEADS + hd + 1]
            gr = jnp.sum(eye * gc, axis=0, keepdims=True)
            decay = jnp.where(incl, jnp.exp(gc - gr), 0.0)
            qc = q_s[pl.ds(r0, c), sl]
            kc = k_s[pl.ds(r0, c), sl]
            vc = v_s[pl.ds(r0, c), sl]
            kb = kc * bcol
            egc = jnp.exp(gc)
            qc_, kc_, kb_ = _cast_small(c, qc, kc, kb)
            lower = jnp.where(strict, lax.dot_general(kb_, kc_, _NT, preferred_element_type=F32) * decay, 0.0)
            m = -lower
            p = eye + m
            for _ in range(n_sq):
                (m_,) = _cast_small(c, m)
                m = jnp.dot(m_, m_, preferred_element_type=F32)
                p_, m_ = _cast_small(c, p, m)
                p = p + jnp.dot(p_, m_, preferred_element_type=F32)
            rhs = jnp.concatenate([vc * bcol, kb * egc], axis=1)
            p_, rhs_ = _cast_small(c, p, rhs)
            uw = jnp.dot(p_, rhs_, preferred_element_type=F32)
            s = s_ref[b, hd]
            w_, s_, qe_ = _cast_small(c, uw[:, GDN_DV:], s, qc * egc)
            v_new = uw[:, :GDN_DV] - jnp.dot(w_, s_, preferred_element_type=F32)
            att = jnp.where(incl, lax.dot_general(qc_, kc_, _NT, preferred_element_type=F32) * decay, 0.0)
            att_, vn_ = _cast_small(c, att, v_new)
            o = (jnp.dot(qe_, s_, preferred_element_type=F32)
                 + jnp.dot(att_, vn_, preferred_element_type=F32))
            g_last = gc[c - 1:c, :]
            (kend_,) = _cast_small(c, kc * jnp.exp(g_last - gc))
            s_ref[b, hd] = jnp.exp(g_last) * s + lax.dot_general(kend_, vn_, _TN, preferred_element_type=F32)
            o_s[pl.ds(r0, c), sl] = o
        return carry

    lax.fori_loop(0, bb * nch, chunk, 0)

    z = _mm(xn, wz_ref[...])
    parts = []
    for hd in range(GDN_HEADS):
        parts.append(_rms_rows(o_s[:, hd * GDN_DV:(hd + 1) * GDN_DV], ng_ref[...]))
    on = jnp.concatenate(parts, axis=-1)
    out = h + _mm(on * jax.nn.silu(z), wout_ref[...])
    o_ref[...] = _rms_rows(out, fg_ref[...]).reshape(bb, tl, D_MODEL)

    @pl.when(l == pl.num_programs(1) - 1)
    def _():
        sout_ref[...] = s_ref[...]
        cbout_ref[...] = ext_ref[:, tl + GDN_PAD - (GDN_CONV - 1):tl + GDN_PAD, :]

    ext_ref[:, 0:GDN_PAD, :] = ext_ref[:, tl:tl + GDN_PAD, :]


def _gdn_layer(h, s0, cb, c, w, bb, tl):
    bsz, seq, _ = h.shape
    shared = s0.shape[0] != bsz
    sblock = (bb, GDN_HEADS, GDN_DK, GDN_DV)
    cblock = (bb, GDN_PAD, GDN_QKV)
    hspec = pl.BlockSpec((bb, tl, D_MODEL), lambda b, l: (b, l, 0))
    rows = bb * tl
    return pl.pallas_call(
        functools.partial(_gdn_kernel, c),
        grid=(bsz // bb, seq // tl),
        in_specs=[hspec, _state_spec((1,) + sblock[1:] if shared else sblock, shared),
                  _state_spec((1,) + cblock[1:] if shared else cblock, shared),
                  _const_spec((1, D_MODEL)), _const_spec((D_MODEL, GDN_QKV)),
                  _const_spec((D_MODEL, GDN_V)), _const_spec((D_MODEL, LANES)),
                  _const_spec((GDN_CONV, GDN_QKV)), _const_spec((1, LANES)), _const_spec((1, LANES)),
                  _const_spec((1, GDN_DV)), _const_spec((1, D_MODEL)), _const_spec((GDN_V, D_MODEL))],
        out_specs=[hspec, pl.BlockSpec(sblock, lambda b, l: (b, 0, 0, 0)),
                   pl.BlockSpec((bb, GDN_CONV - 1, GDN_QKV), lambda b, l: (b, 0, 0))],
        out_shape=[jax.ShapeDtypeStruct(h.shape, F32),
                   jax.ShapeDtypeStruct((bsz,) + sblock[1:], F32),
                   jax.ShapeDtypeStruct((bsz, GDN_CONV - 1, GDN_QKV), F32)],
        scratch_shapes=[pltpu.VMEM(sblock, F32), pltpu.VMEM((bb, GDN_PAD + tl, GDN_QKV), F32),
                        pltpu.VMEM((rows, GDN_QK), F32), pltpu.VMEM((rows, GDN_QK), F32),
                        pltpu.VMEM((rows, GDN_V), F32), pltpu.VMEM((rows, LANES), F32),
                        pltpu.VMEM((rows, LANES), F32), pltpu.VMEM((rows, GDN_V), F32)],
        compiler_params=_params(),
        name="gdn_layer",
    )(h, s0, cb, w["g"], w["wqkv"], w["wz"], w["wab"], w["conv_w"], w["a_log"], w["dt_bias"],
      w["ng"], w["fg"], w["w_out"])


def _row(x, width=None):
    x = x.reshape(1, -1).astype(F32)
    if width is not None and x.shape[1] < width:
        x = jnp.pad(x, ((0, 0), (0, width - x.shape[1])))
    return x


def _pad_cols(w, width):
    return jnp.pad(w, ((0, 0), (0, width - w.shape[1])))


def _block_diag(x):
    nb, k, r, c = x.shape
    eye = jnp.eye(k, dtype=x.dtype)
    return jnp.einsum("jgrc,gh->jgrhc", x, eye).reshape(nb, k * r, k * c)


def _s5_weights(j, norm_g, s5_w_in, s5_b_re, s5_b_im, s5_c_re, s5_c_im, s5_d, s5_log_dt, s5_a_re,
                s5_a_im, s5_w_glu, s5_b_glu, s5_w_out):
    abr, abi, bbr, bbi = _s5_prep(s5_log_dt[j], s5_a_re[j], s5_a_im[j], s5_b_re[j], s5_b_im[j])
    shp = (S5_NB, S5_BUNDLE, S5_GROUP, S5_N)
    wb = jnp.concatenate([_block_diag(bbr.reshape(shp)), _block_diag(bbi.reshape(shp))], axis=-1)
    shp = (S5_NB, S5_BUNDLE, S5_N, S5_GROUP)
    wc = jnp.concatenate([_block_diag(s5_c_re[j].transpose(0, 2, 1).reshape(shp)),
                          _block_diag(-s5_c_im[j].transpose(0, 2, 1).reshape(shp))], axis=1)
    return dict(g=_row(norm_g), w_in=s5_w_in[j].astype(BF16), wb=wb.astype(BF16), wc=wc.astype(BF16),
                abr=_row(abr), abi=_row(abi), d=_row(s5_d[j]), w_glu=s5_w_glu[j].astype(BF16),
                b_glu=_row(s5_b_glu[j]), w_out=s5_w_out[j].astype(BF16))


def kernel(x_prompt, x_sample, state_pool, state_gla, state_s5_re, state_s5_im, state_gdn, state_gdn_conv, meta_tokens, norm_g, final_norm_g, pool_w_in, pool_w_grp, pool_scale, pool_w_out, gla_w_in, gla_w_gk, gla_b_gk, gla_norm_g, gla_w_out, s5_w_in, s5_b_re, s5_b_im, s5_c_re, s5_c_im, s5_d, s5_log_dt, s5_a_re, s5_a_im, s5_w_glu, s5_b_glu, s5_w_out, gdn_w_in, gdn_conv_w, gdn_a_log, gdn_dt_bias, gdn_norm_g, gdn_w_out):
    bp = x_prompt.shape[0]
    bs, ls, _ = x_sample.shape

    wp = dict(g=_row(norm_g[0]), w_in=pool_w_in[0].astype(BF16), w_grp=pool_w_grp[0].astype(BF16),
              scale=_row(pool_scale[0]), w_out=pool_w_out[0].astype(BF16))
    gw = gla_w_in[0]
    wg = dict(g=_row(norm_g[1]), wq=gw[:, :GLA_QK].astype(BF16), wk=gw[:, GLA_QK:2 * GLA_QK].astype(BF16),
              wv=gw[:, 2 * GLA_QK:2 * GLA_QK + GLA_V].astype(BF16),
              wgate=gw[:, 2 * GLA_QK + GLA_V:2 * GLA_QK + 2 * GLA_V].astype(BF16),
              wglow=_pad_cols(gw[:, 2 * GLA_QK + 2 * GLA_V:], LANES).astype(BF16),
              wgk=jnp.pad(gla_w_gk[0], ((0, LANES - GLA_RANK), (0, 0))).astype(BF16),
              bgk=_row(gla_b_gk[0]), ng=_row(gla_norm_g[0]), w_out=gla_w_out[0].astype(BF16))
    ws = _s5_weights(0, norm_g[2], s5_w_in, s5_b_re, s5_b_im, s5_c_re, s5_c_im, s5_d, s5_log_dt,
                     s5_a_re, s5_a_im, s5_w_glu, s5_b_glu, s5_w_out)
    dw = gdn_w_in[0]
    wd = dict(g=_row(norm_g[3]), wqkv=dw[:, :GDN_QKV].astype(BF16),
              wz=dw[:, GDN_QKV:GDN_QKV + GDN_V].astype(BF16),
              wab=_pad_cols(dw[:, GDN_QKV + GDN_V:], LANES).astype(BF16), conv_w=gdn_conv_w[0],
              a_log=_row(gdn_a_log[0], LANES), dt_bias=_row(gdn_dt_bias[0], LANES),
              ng=_row(gdn_norm_g[0]), fg=_row(final_norm_g), w_out=gdn_w_out[0].astype(BF16))

    def pool_hist(st):
        return jnp.pad(st, ((0, 0), (POOL_PAD - POOL_BUF, 0), (0, 0)))

    def conv_hist(st):
        return jnp.pad(st, ((0, 0), (GDN_PAD - (GDN_CONV - 1), 0), (0, 0)))

    def run(h, pool_st, n_valid, gla_st, s5_st, gdn_st, conv_st, blocks):
        (pb, pt), (gb, gt, gc), (sb, stl), (db, dtl, dc) = blocks
        h, pool_new = _pool_layer(h, pool_hist(pool_st), n_valid, wp, pb, pt)
        h, gla_new = _gla_layer(h, gla_st, gc, wg, gb, gt)
        h_tb, s5_new = _s5_layer(jnp.swapaxes(h, 0, 1), s5_st, ws, sb, stl)
        h = jnp.swapaxes(h_tb, 0, 1)
        y, gdn_new, conv_new = _gdn_layer(h, gdn_st, conv_hist(conv_st), dc, wd, db, dtl)
        return y, pool_new, gla_new, s5_new, gdn_new, conv_new

    hm = meta_tokens.astype(F32)[None]
    zeros = lambda *s: jnp.zeros(s, F32)
    hm, m_pool = _pool_layer(hm, zeros(1, POOL_PAD, D_MODEL), 0, wp, 1, N_META)
    hm, m_gla = _gla_layer(hm, zeros(1, GLA_HEADS, GLA_DK, GLA_DV), N_META, wg, 1, N_META)
    hm_tb, m_s5 = _s5_layer(jnp.broadcast_to(jnp.swapaxes(hm, 0, 1), (N_META, SUBLANES, D_MODEL)),
                            zeros(SUBLANES, S5_STATE), ws, SUBLANES, N_META)
    hm = jnp.swapaxes(hm_tb[:, 0:1], 0, 1)
    _, m_gdn, m_conv = _gdn_layer(hm, zeros(1, GDN_HEADS, GDN_DK, GDN_DV), zeros(1, GDN_PAD, GDN_QKV),
                                  N_META, wd, 1, N_META)

    yp, pool_p, gla_p, s5_p, gdn_p, conv_p = run(
        x_prompt, m_pool, N_META, m_gla, jnp.broadcast_to(m_s5[0:1], (bp, S5_STATE)), m_gdn, m_conv,
        ((1, 512), (1, 256, 64), (bp, 32), (1, 256, 64)))
    ys, pool_s, gla_s, s5_s, gdn_s, conv_s = run(
        x_sample, state_pool[0], POOL_BUF, state_gla[0], _s5_pack_state(state_s5_re[0], state_s5_im[0]),
        state_gdn[0], state_gdn_conv[0], ((16, ls), (8, ls, ls), (32, ls), (8, ls, ls)))

    s5r_p, s5i_p = _s5_unpack_state(s5_p)
    s5r_s, s5i_s = _s5_unpack_state(s5_s)
    return (yp, ys, pool_p[None], pool_s[None], gla_p[None], gla_s[None],
            s5r_p[None], s5i_p[None], s5r_s[None], s5i_s[None],
            gdn_p[None], conv_p[None], gdn_s[None], conv_s[None])
```

```python
import functools
import math

import jax
import jax.numpy as jnp
from jax import lax
from jax.experimental import pallas as pl
from jax.experimental.pallas import tpu as pltpu

F32 = jnp.float32
BF16 = jnp.bfloat16
HIGHEST = lax.Precision.HIGHEST

D_MODEL = 1024
EPS = 1e-6
N_META = 16

POOL_WINDOWS = (2, 4, 8, 16)
POOL_GROUP = D_MODEL // len(POOL_WINDOWS)
POOL_BUF = max(POOL_WINDOWS) - 1
POOL_PAD = POOL_BUF + 1

GLA_HEADS = 4
GLA_DK = 128
GLA_DV = 256
GLA_QK = GLA_HEADS * GLA_DK
GLA_V = GLA_HEADS * GLA_DV
GLA_RANK = 16
GLA_GATE_NORM = 16.0

S5_GROUP = 16
S5_G = D_MODEL // S5_GROUP
S5_N = 64
S5_BUNDLE = 8
S5_NB = S5_G // S5_BUNDLE
S5_HALF = S5_BUNDLE * S5_N
S5_STATE = 2 * S5_G * S5_N

GDN_HEADS = 8
GDN_DK = 128
GDN_DV = 128
GDN_CONV = 4
GDN_QK = GDN_HEADS * GDN_DK
GDN_V = GDN_HEADS * GDN_DV
GDN_QKV = 2 * GDN_QK + GDN_V
GDN_PAD = 8

LANES = 128
SUBLANES = 8
VMEM_LIMIT = 52 * 1024 * 1024

_NT = (((1,), (1,)), ((), ()))
_TN = (((0,), (0,)), ((), ()))


def _rms_rows(x, g):
    return x * lax.rsqrt(jnp.mean(x * x, axis=-1, keepdims=True) + EPS) * g


def _mm(a, w):
    return jnp.dot(a.astype(BF16), w, preferred_element_type=F32)


def _cast_small(c, *xs):
    if c % 16 == 0:
        return tuple(x.astype(BF16) for x in xs)
    return xs


def _bf(*xs):
    return tuple(x.astype(BF16) for x in xs)


def _block_masks(n, c):
    r = lax.broadcasted_iota(jnp.int32, (n, n), 0)
    s = lax.broadcasted_iota(jnp.int32, (n, n), 1)
    sh = int(math.log2(c))
    same = (r >> sh) == (s >> sh)
    return same & (r >= s), same & (r > s), r == s


def _exact_dot(x, dot_piece):
    hi = x.astype(BF16)
    r = x - hi.astype(F32)
    mid = r.astype(BF16)
    lo = (r - mid.astype(F32)).astype(BF16)
    return dot_piece(hi) + dot_piece(mid) + dot_piece(lo)


def _dot(a, b):
    return jnp.dot(a, b, preferred_element_type=F32)


def _dot_nt(a, b):
    return lax.dot_general(a, b, _NT, preferred_element_type=F32)


def _dot_tn(a, b):
    return lax.dot_general(a, b, _TN, preferred_element_type=F32)


def _chunk_plan(bb, tl, c, nbu):
    sup = nbu * c
    trips = (bb * tl) // sup
    assert (trips == 1 and nbu == bb and tl == c) or (nbu == 1 and bb == 1), (bb, tl, c, nbu)
    return sup, trips


def _pool_kernel(n_valid, h_ref, buf_ref, g_ref, win_ref, wgrp_ref, scale_ref, wout_ref,
                 o_ref, st_ref, z_ref):
    bb, tl, _ = h_ref.shape
    rows = bb * tl
    l = pl.program_id(1)

    @pl.when(l == 0)
    def _():
        z_ref[:, 0:POOL_PAD, :] = buf_ref[...]

    h = h_ref[...].reshape(rows, D_MODEL)
    xn = _rms_rows(h, g_ref[...])
    ug = _mm(xn, win_ref[...])
    gate = ug[:, D_MODEL:]
    z_ref[:, POOL_PAD:POOL_PAD + tl, :] = ug[:, :D_MODEL].reshape(bb, tl, D_MODEL)

    t = (l * tl + lax.broadcasted_iota(jnp.int32, (1, tl, POOL_GROUP), 1)).astype(F32)
    parts = []
    for gi, w in enumerate(POOL_WINDOWS):
        lo = gi * POOL_GROUP
        cur = z_ref[:, POOL_PAD:POOL_PAD + tl, lo:lo + POOL_GROUP]
        acc = cur
        for j in range(1, w):
            acc = acc + z_ref[:, POOL_PAD - j:POOL_PAD - j + tl, lo:lo + POOL_GROUP]
        cnt = jnp.minimum(float(w), t + (1.0 + n_valid))
        mixed = acc / cnt - cur
        parts.append(_mm(mixed.reshape(rows, POOL_GROUP), wgrp_ref[gi]))
    mixed = jnp.concatenate(parts, axis=-1) * scale_ref[...]
    y = _mm(mixed * jax.nn.silu(gate), wout_ref[...])
    o_ref[...] = (h + y).reshape(bb, tl, D_MODEL)

    @pl.when(l == pl.num_programs(1) - 1)
    def _():
        st_ref[...] = z_ref[:, tl + 1:tl + POOL_PAD, :]

    z_ref[:, 0:POOL_PAD, :] = z_ref[:, tl:tl + POOL_PAD, :]


def _const_spec(shape):
    nd = len(shape)
    return pl.BlockSpec(shape, lambda b, l: (0,) * nd)


def _state_spec(block, shared):
    nd = len(block)
    if shared:
        return pl.BlockSpec(block, lambda b, l: (0,) * nd)
    return pl.BlockSpec(block, lambda b, l: (b,) + (0,) * (nd - 1))


def _params():
    return pltpu.CompilerParams(dimension_semantics=("parallel", "arbitrary"),
                                vmem_limit_bytes=VMEM_LIMIT)


def _pool_layer(h, buf, n_valid, w, bb, tl):
    bsz, seq, _ = h.shape
    shared = buf.shape[0] != bsz
    hspec = pl.BlockSpec((bb, tl, D_MODEL), lambda b, l: (b, l, 0))
    return pl.pallas_call(
        functools.partial(_pool_kernel, float(n_valid)),
        grid=(bsz // bb, seq // tl),
        in_specs=[hspec, _state_spec((bb, POOL_PAD, D_MODEL), shared),
                  _const_spec((1, D_MODEL)), _const_spec((D_MODEL, 2 * D_MODEL)),
                  _const_spec((len(POOL_WINDOWS), POOL_GROUP, POOL_GROUP)),
                  _const_spec((1, D_MODEL)), _const_spec((D_MODEL, D_MODEL))],
        out_specs=[hspec, pl.BlockSpec((bb, POOL_BUF, D_MODEL), lambda b, l: (b, 0, 0))],
        out_shape=[jax.ShapeDtypeStruct(h.shape, F32),
                   jax.ShapeDtypeStruct((bsz, POOL_BUF, D_MODEL), F32)],
        scratch_shapes=[pltpu.VMEM((bb, POOL_PAD + tl, D_MODEL), F32)],
        compiler_params=_params(),
        name="pool_layer",
    )(h, buf, w["g"], w["w_in"], w["w_grp"], w["scale"], w["w_out"])


def _gla_kernel(c, nbu, h_ref, s0_ref, g_ref, wq_ref, wk_ref, wv_ref, wgate_ref, wglow_ref, wgk_ref,
                bgk_ref, ng_ref, wout_ref, o_ref, sout_ref, s_ref, q_s, k_s, v_s, gk_s, o_s):
    bb, tl, _ = h_ref.shape
    rows = bb * tl
    l = pl.program_id(1)

    @pl.when(l == 0)
    def _():
        s_ref[...] = jnp.broadcast_to(s0_ref[...], s_ref.shape)

    h = h_ref[...].reshape(rows, D_MODEL)
    xn = _rms_rows(h, g_ref[...]).astype(BF16)
    q_s[...] = _mm(xn, wq_ref[...]) * GLA_DK ** -0.5
    k_s[...] = _mm(xn, wk_ref[...])
    v_s[...] = _mm(xn, wv_ref[...])
    glow = _mm(xn, wglow_ref[...])
    gk_s[...] = jax.nn.log_sigmoid(_mm(glow, wgk_ref[...]) + bgk_ref[...]) / GLA_GATE_NORM

    sup, trips = _chunk_plan(bb, tl, c, nbu)
    incl, _, _ = _block_masks(sup, c)
    tri = incl.astype(F32).astype(BF16)
    ones = jnp.ones((sup, GLA_DK), BF16)
    rowid = lax.broadcasted_iota(jnp.int32, (sup, GLA_QK), 0)
    heads = range(GLA_HEADS)
    ksl = [slice(hd * GLA_DK, (hd + 1) * GLA_DK) for hd in heads]
    vsl = [slice(hd * GLA_DV, (hd + 1) * GLA_DV) for hd in heads]

    def chunk(it, carry):
        rs = slice(0, sup) if trips == 1 else pl.ds(pl.multiple_of(it * sup, sup), sup)
        g = gk_s[rs, :]
        bc = _exact_dot(g, lambda p: _dot(tri, p))
        q = q_s[rs, :]
        k = k_s[rs, :]
        v = v_s[rs, :]
        qg = q * jnp.exp(bc)
        kg = k * jnp.exp(-bc)
        kd, ebl = [], []
        for i in range(nbu):
            blk = slice(i * c, (i + 1) * c)
            last = (i + 1) * c - 1
            kd.append(k[blk] * jnp.exp(bc[last:last + 1, :] - bc[blk]))
            sel = jnp.where(rowid == last, bc, 0.0)
            ebl.append(jnp.exp(_exact_dot(sel, lambda p: _dot_tn(p, ones))))
        qg_b, kg_b, v_b = _bf(qg, kg, v)
        att = [jnp.where(incl, _dot_nt(qg_b[:, ksl[hd]], kg_b[:, ksl[hd]]), 0.0) for hd in heads]
        o = [_dot(att[hd].astype(BF16), v_b[:, vsl[hd]]) for hd in heads]
        st = [[s_ref[i if trips == 1 else 0, hd] for hd in heads] for i in range(nbu)]
        os_ = [[None] * GLA_HEADS for _ in range(nbu)]
        for i in range(nbu):
            blk = slice(i * c, (i + 1) * c)
            for hd in heads:
                qi, si = _cast_small(c, qg[blk, ksl[hd]], st[i][hd])
                os_[i][hd] = _dot(qi, si)
        for i in range(nbu):
            blk = slice(i * c, (i + 1) * c)
            for hd in heads:
                kdi, vi = _cast_small(c, kd[i][:, ksl[hd]], v[blk, vsl[hd]])
                dec = ebl[i][ksl[hd], :]
                s_ref[i if trips == 1 else 0, hd] = (
                    jnp.concatenate([dec] * (GLA_DV // GLA_DK), axis=1) * st[i][hd] + _dot_tn(kdi, vi))
        o_s[rs, :] = jnp.concatenate(
            [o[hd] + jnp.concatenate([os_[i][hd] for i in range(nbu)], axis=0) for hd in heads], axis=1)
        return carry

    if trips == 1:
        chunk(0, 0)
    else:
        lax.fori_loop(0, trips, chunk, 0)

    gate = _mm(xn, wgate_ref[...])
    parts = []
    for hd in range(GLA_HEADS):
        parts.append(_rms_rows(o_s[:, hd * GLA_DV:(hd + 1) * GLA_DV], ng_ref[...]))
    on = jnp.concatenate(parts, axis=-1)
    out = h + _mm(on * jax.nn.silu(gate), wout_ref[...])
    for i in range(bb):
        o_ref[:, i * D_MODEL:(i + 1) * D_MODEL] = out[i * tl:(i + 1) * tl]

    @pl.when(l == pl.num_programs(1) - 1)
    def _():
        sout_ref[...] = s_ref[...]


def _gla_layer(h, s0, c, w, bb, tl, nbu):
    bsz, seq, _ = h.shape
    shared = s0.shape[0] != bsz
    sblock = (bb, GLA_HEADS, GLA_DK, GLA_DV)
    hspec = pl.BlockSpec((bb, tl, D_MODEL), lambda b, l: (b, l, 0))
    rows = bb * tl
    return pl.pallas_call(
        functools.partial(_gla_kernel, c, nbu),
        grid=(bsz // bb, seq // tl),
        in_specs=[hspec, _state_spec((1,) + sblock[1:] if shared else sblock, shared),
                  _const_spec((1, D_MODEL)),
                  _const_spec((D_MODEL, GLA_QK)), _const_spec((D_MODEL, GLA_QK)),
                  _const_spec((D_MODEL, GLA_V)), _const_spec((D_MODEL, GLA_V)),
                  _const_spec((D_MODEL, LANES)), _const_spec((LANES, GLA_QK)),
                  _const_spec((1, GLA_QK)), _const_spec((1, GLA_DV)),
                  _const_spec((GLA_V, D_MODEL))],
        out_specs=[pl.BlockSpec((tl, bb * D_MODEL), lambda b, l: (l, b)),
                   pl.BlockSpec(sblock, lambda b, l: (b, 0, 0, 0))],
        out_shape=[jax.ShapeDtypeStruct((seq, bsz * D_MODEL), F32),
                   jax.ShapeDtypeStruct((bsz,) + sblock[1:], F32)],
        scratch_shapes=[pltpu.VMEM(sblock, F32),
                        pltpu.VMEM((rows, GLA_QK), F32), pltpu.VMEM((rows, GLA_QK), F32),
                        pltpu.VMEM((rows, GLA_V), F32), pltpu.VMEM((rows, GLA_QK), F32),
                        pltpu.VMEM((rows, GLA_V), F32)],
        compiler_params=_params(),
        name="gla_layer",
    )(h, s0, w["g"], w["wq"], w["wk"], w["wv"], w["wgate"], w["wglow"], w["wgk"], w["bgk"],
      w["ng"], w["w_out"])


def _s5_prep_kernel(logdt_ref, are_ref, aim_ref, bre_ref, bim_ref, abr_ref, abi_ref, bbr_ref, bbi_ref):
    dt = jnp.exp(logdt_ref[...])
    lr = are_ref[...]
    li = aim_ref[...]
    mag = jnp.exp(lr * dt)
    abr = mag * jnp.cos(li * dt)
    abi = mag * jnp.sin(li * dt)
    den = lr * lr + li * li
    cr = ((abr - 1.0) * lr + abi * li) / den
    ci = (abi * lr - (abr - 1.0) * li) / den
    abr_ref[...] = abr
    abi_ref[...] = abi
    br = bre_ref[...]
    bi = bim_ref[...]
    bbr_ref[...] = cr[:, None, :] * br - ci[:, None, :] * bi
    bbi_ref[...] = cr[:, None, :] * bi + ci[:, None, :] * br


def _s5_prep(log_dt, a_re, a_im, b_re, b_im):
    gn = jax.ShapeDtypeStruct((S5_G, S5_N), F32)
    gcn = jax.ShapeDtypeStruct((S5_G, S5_GROUP, S5_N), F32)
    return pl.pallas_call(_s5_prep_kernel, out_shape=[gn, gn, gcn, gcn], name="s5_discretize")(
        log_dt.reshape(S5_G, 1), a_re, a_im, b_re.transpose(0, 2, 1), b_im.transpose(0, 2, 1))


def _s5_kernel(h_ref, h0_ref, g_ref, win_ref, wb_ref, wc_ref, abr_ref, abi_ref, dskip_ref,
               wglu_ref, bglu_ref, wout_ref, o_ref, hout_ref, st_ref, hs_ref):
    tl, nb, _ = h_ref.shape
    rows = tl * nb
    ngrp = nb // SUBLANES
    l = pl.program_id(1)

    @pl.when(l == 0)
    def _():
        hs_ref[...] = h0_ref[...]

    h = h_ref[...].reshape(rows, D_MODEL)
    xn = _rms_rows(h, g_ref[...]).astype(BF16)
    ug = _mm(xn, win_ref[...])
    u = ug[:, :D_MODEL]
    gate = ug[:, D_MODEL:]
    ub = u.astype(BF16)
    width = 2 * S5_HALF
    in_w = S5_BUNDLE * S5_GROUP
    for j in range(S5_NB):
        st_ref[:, j * width:(j + 1) * width] = jnp.dot(
            ub[:, j * in_w:(j + 1) * in_w], wb_ref[j], preferred_element_type=F32)

    for j in range(S5_NB):
        re = slice(j * width, j * width + S5_HALF)
        im = slice(j * width + S5_HALF, (j + 1) * width)
        ar = jnp.broadcast_to(abr_ref[:, j * S5_HALF:(j + 1) * S5_HALF], (SUBLANES, S5_HALF))
        ai = jnp.broadcast_to(abi_ref[:, j * S5_HALF:(j + 1) * S5_HALF], (SUBLANES, S5_HALF))

        def group(bg, carry, re=re, im=im, ar=ar, ai=ai):
            r00 = pl.multiple_of(bg * SUBLANES, SUBLANES)

            def step(t, hc):
                hr, hi = hc
                r = pl.multiple_of(t * nb + r00, SUBLANES)
                nhr = ar * hr - ai * hi + st_ref[pl.ds(r, SUBLANES), re]
                nhi = ar * hi + ai * hr + st_ref[pl.ds(r, SUBLANES), im]
                st_ref[pl.ds(r, SUBLANES), re] = nhr
                st_ref[pl.ds(r, SUBLANES), im] = nhi
                return nhr, nhi

            hr, hi = lax.fori_loop(
                0, tl, step, (hs_ref[pl.ds(r00, SUBLANES), re], hs_ref[pl.ds(r00, SUBLANES), im]))
            hs_ref[pl.ds(r00, SUBLANES), re] = hr
            hs_ref[pl.ds(r00, SUBLANES), im] = hi
            return carry

        lax.fori_loop(0, ngrp, group, 0)

    parts = []
    for j in range(S5_NB):
        parts.append(_mm(st_ref[:, j * width:(j + 1) * width], wc_ref[j]))
    y = jnp.concatenate(parts, axis=-1) + dskip_ref[...] * u
    z = jax.nn.gelu(y)
    z = z * jax.nn.sigmoid(_mm(z, wglu_ref[...]) + bglu_ref[...])
    out = h + _mm(z * jax.nn.silu(gate), wout_ref[...])
    o_ref[...] = out.reshape(tl, nb, D_MODEL)

    @pl.when(l == pl.num_programs(1) - 1)
    def _():
        hout_ref[...] = hs_ref[...]


def _s5_layer(h_tb, h0, w, nb, tl):
    seq, bsz, _ = h_tb.shape
    hspec = pl.BlockSpec((tl, nb, D_MODEL), lambda b, l: (l, b, 0))
    sspec = pl.BlockSpec((nb, S5_STATE), lambda b, l: (b, 0))
    in_w = S5_BUNDLE * S5_GROUP
    return pl.pallas_call(
        _s5_kernel,
        grid=(bsz // nb, seq // tl),
        in_specs=[hspec, sspec, _const_spec((1, D_MODEL)), _const_spec((D_MODEL, 2 * D_MODEL)),
                  _const_spec((S5_NB, in_w, 2 * S5_HALF)), _const_spec((S5_NB, 2 * S5_HALF, in_w)),
                  _const_spec((1, S5_G * S5_N)), _const_spec((1, S5_G * S5_N)),
                  _const_spec((1, D_MODEL)), _const_spec((D_MODEL, D_MODEL)),
                  _const_spec((1, D_MODEL)), _const_spec((D_MODEL, D_MODEL))],
        out_specs=[hspec, sspec],
        out_shape=[jax.ShapeDtypeStruct(h_tb.shape, F32), jax.ShapeDtypeStruct((bsz, S5_STATE), F32)],
        scratch_shapes=[pltpu.VMEM((tl * nb, S5_STATE), F32), pltpu.VMEM((nb, S5_STATE), F32)],
        compiler_params=_params(),
        name="s5_layer",
    )(h_tb, h0, w["g"], w["w_in"], w["wb"], w["wc"], w["abr"], w["abi"], w["d"], w["w_glu"],
      w["b_glu"], w["w_out"])


def _s5_pack_state(re, im):
    b = re.shape[0]
    return jnp.concatenate([re.reshape(b, S5_NB, S5_HALF), im.reshape(b, S5_NB, S5_HALF)],
                           axis=-1).reshape(b, S5_STATE)


def _s5_unpack_state(st):
    b = st.shape[0]
    st = st.reshape(b, S5_NB, 2 * S5_HALF)
    return (st[..., :S5_HALF].reshape(b, S5_G, S5_N), st[..., S5_HALF:].reshape(b, S5_G, S5_N))


def _gdn_kernel(c, nbu, h_ref, s0_ref, cb_ref, g_ref, wqkv_ref, wz_ref, wab_ref, convw_ref, alog_ref,
                dtb_ref, ng_ref, fg_ref, wout_ref, o_ref, sout_ref, cbout_ref,
                s_ref, ext_ref, q_s, k_s, v_s, g_s, b_s, o_s):
    bb, tl, _ = o_ref.shape
    rows = bb * tl
    l = pl.program_id(1)

    @pl.when(l == 0)
    def _():
        s_ref[...] = jnp.broadcast_to(s0_ref[...], s_ref.shape)
        ext_ref[:, 0:GDN_PAD, :] = jnp.broadcast_to(cb_ref[...], (bb, GDN_PAD, GDN_QKV))

    h = jnp.concatenate([h_ref[:, i * D_MODEL:(i + 1) * D_MODEL] for i in range(bb)], axis=0)
    xn = _rms_rows(h, g_ref[...]).astype(BF16)
    ext_ref[:, GDN_PAD:GDN_PAD + tl, :] = _mm(xn, wqkv_ref[...]).reshape(bb, tl, GDN_QKV)
    first = GDN_PAD - (GDN_CONV - 1)
    conv = convw_ref[0:1, :] * ext_ref[:, first:first + tl, :]
    for j in range(1, GDN_CONV):
        conv = conv + convw_ref[j:j + 1, :] * ext_ref[:, first + j:first + j + tl, :]
    act = jax.nn.silu(conv).reshape(rows, GDN_QKV)
    for hd in range(GDN_HEADS):
        sl = slice(hd * GDN_DK, (hd + 1) * GDN_DK)
        qh = act[:, hd * GDN_DK:(hd + 1) * GDN_DK]
        kh = act[:, GDN_QK + hd * GDN_DK:GDN_QK + (hd + 1) * GDN_DK]
        q_s[:, sl] = qh * lax.rsqrt(jnp.sum(qh * qh, axis=-1, keepdims=True) + EPS) * GDN_DK ** -0.5
        k_s[:, sl] = kh * lax.rsqrt(jnp.sum(kh * kh, axis=-1, keepdims=True) + EPS)
    v_s[...] = act[:, 2 * GDN_QK:]
    ab = _mm(xn, wab_ref[...])
    g_s[...] = -jnp.exp(alog_ref[...]) * jax.nn.softplus(ab + dtb_ref[...])
    b_s[...] = jax.nn.sigmoid(ab)

    sup, trips = _chunk_plan(bb, tl, c, nbu)
    assert sup <= GDN_DK
    incl, strict, diag = _block_masks(sup, c)
    tri = incl.astype(F32).astype(BF16)
    eye = diag.astype(F32)
    n_sq = int(math.log2(c)) - 1
    src = lax.broadcasted_iota(jnp.int32, (LANES, GDN_QK), 0)
    dst_head = lax.broadcasted_iota(jnp.int32, (LANES, GDN_QK), 1) >> int(math.log2(GDN_DK))
    spread_g = (src == dst_head).astype(F32).astype(BF16)
    spread_b = (src == dst_head + GDN_HEADS).astype(F32).astype(BF16)
    heads = range(GDN_HEADS)
    hsl = [slice(hd * GDN_DK, (hd + 1) * GDN_DK) for hd in heads]
    blks = [slice(i * c, (i + 1) * c) for i in range(nbu)]

    def chunk(it, carry):
        rs = slice(0, sup) if trips == 1 else pl.ds(pl.multiple_of(it * sup, sup), sup)
        gcum = _exact_dot(g_s[rs, :], lambda p: _dot(tri, p))
        gcx = _exact_dot(gcum, lambda p: _dot(p, spread_g))
        bx = _exact_dot(b_s[rs, :], lambda p: _dot(p, spread_b))
        q = q_s[rs, :]
        k = k_s[rs, :]
        v = v_s[rs, :]
        egx = jnp.exp(gcx)
        kb = k * bx
        qe = q * egx
        q_b, k_b, kb_b = _bf(q, k, kb)
        rhs_v = v * bx
        rhs_k = kb * egx
        kend, egl = [], []
        for i in range(nbu):
            gl = gcx[(i + 1) * c - 1:(i + 1) * c, :]
            kend.append(k[blks[i]] * jnp.exp(gl - gcx[blks[i]]))
            egl.append(jnp.exp(gl))
        m, att, rhs = [], [], []
        for hd in heads:
            kk = _dot_nt(kb_b[:, hsl[hd]], k_b[:, hsl[hd]])
            qk = _dot_nt(q_b[:, hsl[hd]], k_b[:, hsl[hd]])
            gcol = gcx[:, hd * GDN_DK:hd * GDN_DK + sup]
            grow = jnp.sum(eye * gcol, axis=0, keepdims=True)
            decay = jnp.exp(gcol - grow)
            m.append(-jnp.where(strict, kk * decay, 0.0))
            att.append(jnp.where(incl, qk * decay, 0.0))
            rhs.append(jnp.concatenate([rhs_v[:, hsl[hd]], rhs_k[:, hsl[hd]]], axis=1))
        for j in range(n_sq + 1):
            m_b = [mh.astype(BF16) for mh in m]
            rhs = [rhs[hd] + _dot(m_b[hd], rhs[hd].astype(BF16)) for hd in heads]
            if j < n_sq:
                m = [_dot(m_b[hd], m_b[hd]) for hd in heads]
        st = [[s_ref[i if trips == 1 else 0, hd] for hd in heads] for i in range(nbu)]
        v_new = [[None] * GDN_HEADS for _ in range(nbu)]
        qs = [[None] * GDN_HEADS for _ in range(nbu)]
        for i in range(nbu):
            for hd in heads:
                wq = jnp.concatenate([rhs[hd][blks[i], GDN_DV:], qe[blks[i], hsl[hd]]], axis=0)
                ws = _dot(wq.astype(BF16), st[i][hd].astype(BF16))
                v_new[i][hd] = rhs[hd][blks[i], :GDN_DV] - ws[:c]
                qs[i][hd] = ws[c:]
        o = []
        for hd in heads:
            vn = jnp.concatenate([v_new[i][hd] for i in range(nbu)], axis=0)
            o.append(jnp.concatenate([qs[i][hd] for i in range(nbu)], axis=0)
                     + _dot(att[hd].astype(BF16), vn.astype(BF16)))
        for i in range(nbu):
            for hd in heads:
                ke, vn = _cast_small(c, kend[i][:, hsl[hd]], v_new[i][hd])
                s_ref[i if trips == 1 else 0, hd] = egl[i][:, hsl[hd]] * st[i][hd] + _dot_tn(ke, vn)
        o_s[rs, :] = jnp.concatenate(o, axis=1)
        return carry

    if trips == 1:
        chunk(0, 0)
    else:
        lax.fori_loop(0, trips, chunk, 0)

    z = _mm(xn, wz_ref[...])
    parts = []
    for hd in range(GDN_HEADS):
        parts.append(_rms_rows(o_s[:, hd * GDN_DV:(hd + 1) * GDN_DV], ng_ref[...]))
    on = jnp.concatenate(parts, axis=-1)
    out = h + _mm(on * jax.nn.silu(z), wout_ref[...])
    o_ref[...] = _rms_rows(out, fg_ref[...]).reshape(bb, tl, D_MODEL)

    @pl.when(l == pl.num_programs(1) - 1)
    def _():
        sout_ref[...] = s_ref[...]
        cbout_ref[...] = ext_ref[:, tl + GDN_PAD - (GDN_CONV - 1):tl + GDN_PAD, :]

    ext_ref[:, 0:GDN_PAD, :] = ext_ref[:, tl:tl + GDN_PAD, :]


def _gdn_layer(h_tm, s0, cb, c, w, bb, tl, nbu):
    seq = h_tm.shape[0]
    bsz = h_tm.shape[1] // D_MODEL
    shared = s0.shape[0] != bsz
    sblock = (bb, GDN_HEADS, GDN_DK, GDN_DV)
    cblock = (bb, GDN_PAD, GDN_QKV)
    hspec = pl.BlockSpec((bb, tl, D_MODEL), lambda b, l: (b, l, 0))
    rows = bb * tl
    return pl.pallas_call(
        functools.partial(_gdn_kernel, c, nbu),
        grid=(bsz // bb, seq // tl),
        in_specs=[pl.BlockSpec((tl, bb * D_MODEL), lambda b, l: (l, b)),
                  _state_spec((1,) + sblock[1:] if shared else sblock, shared),
                  _state_spec((1,) + cblock[1:] if shared else cblock, shared),
                  _const_spec((1, D_MODEL)), _const_spec((D_MODEL, GDN_QKV)),
                  _const_spec((D_MODEL, GDN_V)), _const_spec((D_MODEL, LANES)),
                  _const_spec((GDN_CONV, GDN_QKV)), _const_spec((1, LANES)), _const_spec((1, LANES)),
                  _const_spec((1, GDN_DV)), _const_spec((1, D_MODEL)), _const_spec((GDN_V, D_MODEL))],
        out_specs=[hspec, pl.BlockSpec(sblock, lambda b, l: (b, 0, 0, 0)),
                   pl.BlockSpec((bb, GDN_CONV - 1, GDN_QKV), lambda b, l: (b, 0, 0))],
        out_shape=[jax.ShapeDtypeStruct((bsz, seq, D_MODEL), F32),
                   jax.ShapeDtypeStruct((bsz,) + sblock[1:], F32),
                   jax.ShapeDtypeStruct((bsz, GDN_CONV - 1, GDN_QKV), F32)],
        scratch_shapes=[pltpu.VMEM(sblock, F32), pltpu.VMEM((bb, GDN_PAD + tl, GDN_QKV), F32),
                        pltpu.VMEM((rows, GDN_QK), F32), pltpu.VMEM((rows, GDN_QK), F32),
                        pltpu.VMEM((rows, GDN_V), F32), pltpu.VMEM((rows, LANES), F32),
                        pltpu.VMEM((rows, LANES), F32), pltpu.VMEM((rows, GDN_V), F32)],
        compiler_params=_params(),
        name="gdn_layer",
    )(h_tm, s0, cb, w["g"], w["wqkv"], w["wz"], w["wab"], w["conv_w"], w["a_log"], w["dt_bias"],
      w["ng"], w["fg"], w["w_out"])


def _row(x, width=None):
    x = x.reshape(1, -1).astype(F32)
    if width is not None and x.shape[1] < width:
        x = jnp.pad(x, ((0, 0), (0, width - x.shape[1])))
    return x


def _pad_cols(w, width):
    return jnp.pad(w, ((0, 0), (0, width - w.shape[1])))


def _block_diag(x):
    nb, k, r, c = x.shape
    eye = jnp.eye(k, dtype=x.dtype)
    return jnp.einsum("jgrc,gh->jgrhc", x, eye).reshape(nb, k * r, k * c)


def _s5_weights(j, norm_g, s5_w_in, s5_b_re, s5_b_im, s5_c_re, s5_c_im, s5_d, s5_log_dt, s5_a_re,
                s5_a_im, s5_w_glu, s5_b_glu, s5_w_out):
    abr, abi, bbr, bbi = _s5_prep(s5_log_dt[j], s5_a_re[j], s5_a_im[j], s5_b_re[j], s5_b_im[j])
    shp = (S5_NB, S5_BUNDLE, S5_GROUP, S5_N)
    wb = jnp.concatenate([_block_diag(bbr.reshape(shp)), _block_diag(bbi.reshape(shp))], axis=-1)
    shp = (S5_NB, S5_BUNDLE, S5_N, S5_GROUP)
    wc = jnp.concatenate([_block_diag(s5_c_re[j].transpose(0, 2, 1).reshape(shp)),
                          _block_diag(-s5_c_im[j].transpose(0, 2, 1).reshape(shp))], axis=1)
    return dict(g=_row(norm_g), w_in=s5_w_in[j].astype(BF16), wb=wb.astype(BF16), wc=wc.astype(BF16),
                abr=_row(abr), abi=_row(abi), d=_row(s5_d[j]), w_glu=s5_w_glu[j].astype(BF16),
                b_glu=_row(s5_b_glu[j]), w_out=s5_w_out[j].astype(BF16))


def kernel(x_prompt, x_sample, state_pool, state_gla, state_s5_re, state_s5_im, state_gdn, state_gdn_conv, meta_tokens, norm_g, final_norm_g, pool_w_in, pool_w_grp, pool_scale, pool_w_out, gla_w_in, gla_w_gk, gla_b_gk, gla_norm_g, gla_w_out, s5_w_in, s5_b_re, s5_b_im, s5_c_re, s5_c_im, s5_d, s5_log_dt, s5_a_re, s5_a_im, s5_w_glu, s5_b_glu, s5_w_out, gdn_w_in, gdn_conv_w, gdn_a_log, gdn_dt_bias, gdn_norm_g, gdn_w_out):
    bp = x_prompt.shape[0]
    bs, ls, _ = x_sample.shape

    wp = dict(g=_row(norm_g[0]), w_in=pool_w_in[0].astype(BF16), w_grp=pool_w_grp[0].astype(BF16),
              scale=_row(pool_scale[0]), w_out=pool_w_out[0].astype(BF16))
    gw = gla_w_in[0]
    wg = dict(g=_row(norm_g[1]), wq=gw[:, :GLA_QK].astype(BF16), wk=gw[:, GLA_QK:2 * GLA_QK].astype(BF16),
              wv=gw[:, 2 * GLA_QK:2 * GLA_QK + GLA_V].astype(BF16),
              wgate=gw[:, 2 * GLA_QK + GLA_V:2 * GLA_QK + 2 * GLA_V].astype(BF16),
              wglow=_pad_cols(gw[:, 2 * GLA_QK + 2 * GLA_V:], LANES).astype(BF16),
              wgk=jnp.pad(gla_w_gk[0], ((0, LANES - GLA_RANK), (0, 0))).astype(BF16),
              bgk=_row(gla_b_gk[0]), ng=_row(gla_norm_g[0]), w_out=gla_w_out[0].astype(BF16))
    ws = _s5_weights(0, norm_g[2], s5_w_in, s5_b_re, s5_b_im, s5_c_re, s5_c_im, s5_d, s5_log_dt,
                     s5_a_re, s5_a_im, s5_w_glu, s5_b_glu, s5_w_out)
    dw = gdn_w_in[0]
    wd = dict(g=_row(norm_g[3]), wqkv=dw[:, :GDN_QKV].astype(BF16),
              wz=dw[:, GDN_QKV:GDN_QKV + GDN_V].astype(BF16),
              wab=_pad_cols(dw[:, GDN_QKV + GDN_V:], LANES).astype(BF16), conv_w=gdn_conv_w[0],
              a_log=_row(gdn_a_log[0], LANES), dt_bias=_row(gdn_dt_bias[0], LANES),
              ng=_row(gdn_norm_g[0]), fg=_row(final_norm_g), w_out=gdn_w_out[0].astype(BF16))

    def pool_hist(st):
        return jnp.pad(st, ((0, 0), (POOL_PAD - POOL_BUF, 0), (0, 0)))

    def conv_hist(st):
        return jnp.pad(st, ((0, 0), (GDN_PAD - (GDN_CONV - 1), 0), (0, 0)))

    def run(h, pool_st, n_valid, gla_st, s5_st, gdn_st, conv_st, blocks):
        (pb, pt), (gb, gt, gc, gn), (sb, stl), (db, dtl, dc, dn) = blocks
        bsz, seq, _ = h.shape
        h, pool_new = _pool_layer(h, pool_hist(pool_st), n_valid, wp, pb, pt)
        h_tm, gla_new = _gla_layer(h, gla_st, gc, wg, gb, gt, gn)
        h_tb, s5_new = _s5_layer(h_tm.reshape(seq, bsz, D_MODEL), s5_st, ws, sb, stl)
        y, gdn_new, conv_new = _gdn_layer(h_tb.reshape(seq, bsz * D_MODEL), gdn_st, conv_hist(conv_st),
                                          dc, wd, db, dtl, dn)
        return y, pool_new, gla_new, s5_new, gdn_new, conv_new

    hm = meta_tokens.astype(F32)[None]
    zeros = lambda *s: jnp.zeros(s, F32)
    hm, m_pool = _pool_layer(hm, zeros(1, POOL_PAD, D_MODEL), 0, wp, 1, N_META)
    hm, m_gla = _gla_layer(hm, zeros(1, GLA_HEADS, GLA_DK, GLA_DV), N_META, wg, 1, N_META, 1)
    hm_tb, m_s5 = _s5_layer(jnp.broadcast_to(hm[:, None, :], (N_META, SUBLANES, D_MODEL)),
                            zeros(SUBLANES, S5_STATE), ws, SUBLANES, N_META)
    _, m_gdn, m_conv = _gdn_layer(hm_tb[:, 0], zeros(1, GDN_HEADS, GDN_DK, GDN_DV),
                                  zeros(1, GDN_PAD, GDN_QKV), N_META, wd, 1, N_META, 1)

    yp, pool_p, gla_p, s5_p, gdn_p, conv_p = run(
        x_prompt, m_pool, N_META, m_gla, jnp.broadcast_to(m_s5[0:1], (bp, S5_STATE)), m_gdn, m_conv,
        ((1, 512), (1, 256, 64, 1), (bp, 32), (1, 256, 64, 1)))
    ys, pool_s, gla_s, s5_s, gdn_s, conv_s = run(
        x_sample, state_pool[0], POOL_BUF, state_gla[0], _s5_pack_state(state_s5_re[0], state_s5_im[0]),
        state_gdn[0], state_gdn_conv[0], ((16, ls), (8, ls, ls, 8), (32, ls), (8, ls, ls, 8)))

    s5r_p, s5i_p = _s5_unpack_state(s5_p)
    s5r_s, s5i_s = _s5_unpack_state(s5_s)
    return (yp, ys, pool_p[None], pool_s[None], gla_p[None], gla_s[None],
            s5r_p[None], s5i_p[None], s5r_s[None], s5i_s[None],
            gdn_p[None], conv_p[None], gdn_s[None], conv_s[None])
```

```python
import functools
import math

import jax
import jax.numpy as jnp
from jax import lax
from jax.experimental import pallas as pl
from jax.experimental.pallas import tpu as pltpu

F32 = jnp.float32
BF16 = jnp.bfloat16
HIGHEST = lax.Precision.HIGHEST

D_MODEL = 1024
EPS = 1e-6
N_META = 16

POOL_WINDOWS = (2, 4, 8, 16)
POOL_GROUP = D_MODEL // len(POOL_WINDOWS)
POOL_BUF = max(POOL_WINDOWS) - 1
POOL_PAD = POOL_BUF + 1

GLA_HEADS = 4
GLA_DK = 128
GLA_DV = 256
GLA_QK = GLA_HEADS * GLA_DK
GLA_V = GLA_HEADS * GLA_DV
GLA_RANK = 16
GLA_GATE_NORM = 16.0

S5_GROUP = 16
S5_G = D_MODEL // S5_GROUP
S5_N = 64
S5_BUNDLE = 8
S5_NB = S5_G // S5_BUNDLE
S5_HALF = S5_BUNDLE * S5_N
S5_STATE = 2 * S5_G * S5_N

GDN_HEADS = 8
GDN_DK = 128
GDN_DV = 128
GDN_CONV = 4
GDN_QK = GDN_HEADS * GDN_DK
GDN_V = GDN_HEADS * GDN_DV
GDN_QKV = 2 * GDN_QK + GDN_V
GDN_PAD = 8

LANES = 128
SUBLANES = 8
VMEM_LIMIT = 52 * 1024 * 1024

_NT = (((1,), (1,)), ((), ()))
_TN = (((0,), (0,)), ((), ()))


def _rms_rows(x, g):
    return x * lax.rsqrt(jnp.mean(x * x, axis=-1, keepdims=True) + EPS) * g


def _mm(a, w):
    return jnp.dot(a.astype(BF16), w, preferred_element_type=F32)


def _cast_small(c, *xs):
    if c % 16 == 0:
        return tuple(x.astype(BF16) for x in xs)
    return xs


def _bf(*xs):
    return tuple(x.astype(BF16) for x in xs)


def _block_masks(n, c):
    r = lax.broadcasted_iota(jnp.int32, (n, n), 0)
    s = lax.broadcasted_iota(jnp.int32, (n, n), 1)
    sh = int(math.log2(c))
    same = (r >> sh) == (s >> sh)
    return same & (r >= s), same & (r > s), r == s


def _exact_dot(x, dot_piece):
    hi = x.astype(BF16)
    r = x - hi.astype(F32)
    mid = r.astype(BF16)
    lo = (r - mid.astype(F32)).astype(BF16)
    return dot_piece(hi) + dot_piece(mid) + dot_piece(lo)


def _dot(a, b):
    return jnp.dot(a, b, preferred_element_type=F32)


def _dot_nt(a, b):
    return lax.dot_general(a, b, _NT, preferred_element_type=F32)


def _dot_tn(a, b):
    return lax.dot_general(a, b, _TN, preferred_element_type=F32)


def _pool_kernel(n_valid, h_ref, buf_ref, g_ref, win_ref, wgrp_ref, scale_ref, wout_ref,
                 o_ref, st_ref, z_ref):
    bb, tl, _ = h_ref.shape
    rows = bb * tl
    l = pl.program_id(1)

    @pl.when(l == 0)
    def _():
        z_ref[:, 0:POOL_PAD, :] = buf_ref[...]

    h = h_ref[...].reshape(rows, D_MODEL)
    xn = _rms_rows(h, g_ref[...])
    ug = _mm(xn, win_ref[...])
    gate = ug[:, D_MODEL:]
    z_ref[:, POOL_PAD:POOL_PAD + tl, :] = ug[:, :D_MODEL].reshape(bb, tl, D_MODEL)

    t = (l * tl + lax.broadcasted_iota(jnp.int32, (1, tl, POOL_GROUP), 1)).astype(F32)
    parts = []
    for gi, w in enumerate(POOL_WINDOWS):
        lo = gi * POOL_GROUP
        cur = z_ref[:, POOL_PAD:POOL_PAD + tl, lo:lo + POOL_GROUP]
        acc = cur
        for j in range(1, w):
            acc = acc + z_ref[:, POOL_PAD - j:POOL_PAD - j + tl, lo:lo + POOL_GROUP]
        cnt = jnp.minimum(float(w), t + (1.0 + n_valid))
        mixed = acc / cnt - cur
        parts.append(_mm(mixed.reshape(rows, POOL_GROUP), wgrp_ref[gi]))
    mixed = jnp.concatenate(parts, axis=-1) * scale_ref[...]
    y = _mm(mixed * jax.nn.silu(gate), wout_ref[...])
    o_ref[...] = (h + y).reshape(bb, tl, D_MODEL)

    @pl.when(l == pl.num_programs(1) - 1)
    def _():
        st_ref[...] = z_ref[:, tl + 1:tl + POOL_PAD, :]

    z_ref[:, 0:POOL_PAD, :] = z_ref[:, tl:tl + POOL_PAD, :]


def _const_spec(shape):
    nd = len(shape)
    return pl.BlockSpec(shape, lambda b, l: (0,) * nd)


def _state_spec(block, shared):
    nd = len(block)
    if shared:
        return pl.BlockSpec(block, lambda b, l: (0,) * nd)
    return pl.BlockSpec(block, lambda b, l: (b,) + (0,) * (nd - 1))


def _params():
    return pltpu.CompilerParams(dimension_semantics=("parallel", "arbitrary"),
                                vmem_limit_bytes=VMEM_LIMIT)


def _pool_layer(h, buf, n_valid, w, bb, tl):
    bsz, seq, _ = h.shape
    shared = buf.shape[0] != bsz
    hspec = pl.BlockSpec((bb, tl, D_MODEL), lambda b, l: (b, l, 0))
    return pl.pallas_call(
        functools.partial(_pool_kernel, float(n_valid)),
        grid=(bsz // bb, seq // tl),
        in_specs=[hspec, _state_spec((bb, POOL_PAD, D_MODEL), shared),
                  _const_spec((1, D_MODEL)), _const_spec((D_MODEL, 2 * D_MODEL)),
                  _const_spec((len(POOL_WINDOWS), POOL_GROUP, POOL_GROUP)),
                  _const_spec((1, D_MODEL)), _const_spec((D_MODEL, D_MODEL))],
        out_specs=[hspec, pl.BlockSpec((bb, POOL_BUF, D_MODEL), lambda b, l: (b, 0, 0))],
        out_shape=[jax.ShapeDtypeStruct(h.shape, F32),
                   jax.ShapeDtypeStruct((bsz, POOL_BUF, D_MODEL), F32)],
        scratch_shapes=[pltpu.VMEM((bb, POOL_PAD + tl, D_MODEL), F32)],
        compiler_params=_params(),
        name="pool_layer",
    )(h, buf, w["g"], w["w_in"], w["w_grp"], w["scale"], w["w_out"])


def _gla_kernel(h_ref, s0_ref, g_ref, wq_ref, wk_ref, wv_ref, wgate_ref, wglow_ref, wgk_ref,
                bgk_ref, ng_ref, wout_ref, o_ref, sout_ref, s_ref):
    bb, tl, _ = h_ref.shape
    rows = bb * tl
    l = pl.program_id(1)

    @pl.when(l == 0)
    def _():
        s_ref[...] = jnp.broadcast_to(s0_ref[...], s_ref.shape)

    h = h_ref[...].reshape(rows, D_MODEL)
    xn = _rms_rows(h, g_ref[...]).astype(BF16)
    q = _mm(xn, wq_ref[...]) * GLA_DK ** -0.5
    k = _mm(xn, wk_ref[...])
    v = _mm(xn, wv_ref[...])
    glow = _mm(xn, wglow_ref[...])
    gk = jax.nn.log_sigmoid(_mm(glow, wgk_ref[...]) + bgk_ref[...]) / GLA_GATE_NORM

    incl, _, _ = _block_masks(rows, tl)
    tri = incl.astype(F32).astype(BF16)
    eye_k = (lax.broadcasted_iota(jnp.int32, (GLA_DK, GLA_DK), 0)
             == lax.broadcasted_iota(jnp.int32, (GLA_DK, GLA_DK), 1))
    heads = range(GLA_HEADS)
    seqs = range(bb)
    ksl = [slice(hd * GLA_DK, (hd + 1) * GLA_DK) for hd in heads]
    vsl = [slice(hd * GLA_DV, (hd + 1) * GLA_DV) for hd in heads]
    blks = [slice(b * tl, (b + 1) * tl) for b in seqs]

    bc = _exact_dot(gk, lambda p: _dot(tri, p))
    qg = q * jnp.exp(bc)
    kg = k * jnp.exp(-bc)
    kd, ebl = [], []
    for b in seqs:
        bl = bc[(b + 1) * tl - 1:(b + 1) * tl, :]
        kd.append(k[blks[b]] * jnp.exp(bl - bc[blks[b]]))
        ebl.append([jnp.exp(jnp.sum(jnp.where(eye_k, jnp.broadcast_to(bl[:, ksl[hd]], (GLA_DK, GLA_DK)), 0.0),
                                    axis=1, keepdims=True)) for hd in heads])
    qg_b, kg_b, v_b = _bf(qg, kg, v)
    att = [jnp.where(incl, _dot_nt(qg_b[:, ksl[hd]], kg_b[:, ksl[hd]]), 0.0) for hd in heads]
    o = [_dot(att[hd].astype(BF16), v_b[:, vsl[hd]]) for hd in heads]
    st = [[s_ref[b, hd] for hd in heads] for b in seqs]
    o_st = [[None] * GLA_HEADS for _ in seqs]
    for b in seqs:
        for hd in heads:
            qi, si = _cast_small(tl, qg[blks[b], ksl[hd]], st[b][hd])
            o_st[b][hd] = _dot(qi, si)
    for b in seqs:
        for hd in heads:
            kdi, vi = _cast_small(tl, kd[b][:, ksl[hd]], v[blks[b], vsl[hd]])
            s_ref[b, hd] = ebl[b][hd] * st[b][hd] + _dot_tn(kdi, vi)
    o_all = jnp.concatenate(
        [o[hd] + jnp.concatenate([o_st[b][hd] for b in seqs], axis=0) for hd in heads], axis=1)

    gate = _mm(xn, wgate_ref[...])
    parts = []
    for hd in range(GLA_HEADS):
        parts.append(_rms_rows(o_all[:, hd * GLA_DV:(hd + 1) * GLA_DV], ng_ref[...]))
    on = jnp.concatenate(parts, axis=-1)
    out = h + _mm(on * jax.nn.silu(gate), wout_ref[...])
    for i in range(bb):
        o_ref[:, i * D_MODEL:(i + 1) * D_MODEL] = out[i * tl:(i + 1) * tl]

    @pl.when(l == pl.num_programs(1) - 1)
    def _():
        sout_ref[...] = s_ref[...]


def _gla_layer(h, s0, w, bb, tl):
    bsz, seq, _ = h.shape
    shared = s0.shape[0] != bsz
    sblock = (bb, GLA_HEADS, GLA_DK, GLA_DV)
    hspec = pl.BlockSpec((bb, tl, D_MODEL), lambda b, l: (b, l, 0))
    return pl.pallas_call(
        _gla_kernel,
        grid=(bsz // bb, seq // tl),
        in_specs=[hspec, _state_spec((1,) + sblock[1:] if shared else sblock, shared),
                  _const_spec((1, D_MODEL)),
                  _const_spec((D_MODEL, GLA_QK)), _const_spec((D_MODEL, GLA_QK)),
                  _const_spec((D_MODEL, GLA_V)), _const_spec((D_MODEL, GLA_V)),
                  _const_spec((D_MODEL, LANES)), _const_spec((LANES, GLA_QK)),
                  _const_spec((1, GLA_QK)), _const_spec((1, GLA_DV)),
                  _const_spec((GLA_V, D_MODEL))],
        out_specs=[pl.BlockSpec((tl, bb * D_MODEL), lambda b, l: (l, b)),
                   pl.BlockSpec(sblock, lambda b, l: (b, 0, 0, 0))],
        out_shape=[jax.ShapeDtypeStruct((seq, bsz * D_MODEL), F32),
                   jax.ShapeDtypeStruct((bsz,) + sblock[1:], F32)],
        scratch_shapes=[pltpu.VMEM(sblock, F32)],
        compiler_params=_params(),
        name="gla_layer",
    )(h, s0, w["g"], w["wq"], w["wk"], w["wv"], w["wgate"], w["wglow"], w["wgk"], w["bgk"],
      w["ng"], w["w_out"])


def _s5_prep_kernel(logdt_ref, are_ref, aim_ref, bre_ref, bim_ref, abr_ref, abi_ref, bbr_ref, bbi_ref):
    dt = jnp.exp(logdt_ref[...])
    lr = are_ref[...]
    li = aim_ref[...]
    mag = jnp.exp(lr * dt)
    abr = mag * jnp.cos(li * dt)
    abi = mag * jnp.sin(li * dt)
    den = lr * lr + li * li
    cr = ((abr - 1.0) * lr + abi * li) / den
    ci = (abi * lr - (abr - 1.0) * li) / den
    abr_ref[...] = abr
    abi_ref[...] = abi
    br = bre_ref[...]
    bi = bim_ref[...]
    bbr_ref[...] = cr[:, None, :] * br - ci[:, None, :] * bi
    bbi_ref[...] = cr[:, None, :] * bi + ci[:, None, :] * br


def _s5_prep(log_dt, a_re, a_im, b_re, b_im):
    gn = jax.ShapeDtypeStruct((S5_G, S5_N), F32)
    gcn = jax.ShapeDtypeStruct((S5_G, S5_GROUP, S5_N), F32)
    return pl.pallas_call(_s5_prep_kernel, out_shape=[gn, gn, gcn, gcn], name="s5_discretize")(
        log_dt.reshape(S5_G, 1), a_re, a_im, b_re.transpose(0, 2, 1), b_im.transpose(0, 2, 1))


def _s5_kernel(h_ref, h0_ref, g_ref, win_ref, wb_ref, wc_ref, abr_ref, abi_ref, dskip_ref,
               wglu_ref, bglu_ref, wout_ref, o_ref, hout_ref, st_ref, hs_ref):
    tl, nb, _ = h_ref.shape
    rows = tl * nb
    ngrp = nb // SUBLANES
    l = pl.program_id(1)

    @pl.when(l == 0)
    def _():
        hs_ref[...] = h0_ref[...]

    h = h_ref[...].reshape(rows, D_MODEL)
    xn = _rms_rows(h, g_ref[...]).astype(BF16)
    ug = _mm(xn, win_ref[...])
    u = ug[:, :D_MODEL]
    gate = ug[:, D_MODEL:]
    ub = u.astype(BF16)
    width = 2 * S5_HALF
    in_w = S5_BUNDLE * S5_GROUP
    for j in range(S5_NB):
        st_ref[:, j * width:(j + 1) * width] = jnp.dot(
            ub[:, j * in_w:(j + 1) * in_w], wb_ref[j], preferred_element_type=F32)

    for j in range(S5_NB):
        re = slice(j * width, j * width + S5_HALF)
        im = slice(j * width + S5_HALF, (j + 1) * width)
        ar = jnp.broadcast_to(abr_ref[:, j * S5_HALF:(j + 1) * S5_HALF], (SUBLANES, S5_HALF))
        ai = jnp.broadcast_to(abi_ref[:, j * S5_HALF:(j + 1) * S5_HALF], (SUBLANES, S5_HALF))

        def group(bg, carry, re=re, im=im, ar=ar, ai=ai):
            r00 = pl.multiple_of(bg * SUBLANES, SUBLANES)

            def step(t, hc):
                hr, hi = hc
                r = pl.multiple_of(t * nb + r00, SUBLANES)
                nhr = ar * hr - ai * hi + st_ref[pl.ds(r, SUBLANES), re]
                nhi = ar * hi + ai * hr + st_ref[pl.ds(r, SUBLANES), im]
                st_ref[pl.ds(r, SUBLANES), re] = nhr
                st_ref[pl.ds(r, SUBLANES), im] = nhi
                return nhr, nhi

            hr, hi = lax.fori_loop(
                0, tl, step, (hs_ref[pl.ds(r00, SUBLANES), re], hs_ref[pl.ds(r00, SUBLANES), im]))
            hs_ref[pl.ds(r00, SUBLANES), re] = hr
            hs_ref[pl.ds(r00, SUBLANES), im] = hi
            return carry

        lax.fori_loop(0, ngrp, group, 0)

    parts = []
    for j in range(S5_NB):
        parts.append(_mm(st_ref[:, j * width:(j + 1) * width], wc_ref[j]))
    y = jnp.concatenate(parts, axis=-1) + dskip_ref[...] * u
    z = jax.nn.gelu(y)
    z = z * jax.nn.sigmoid(_mm(z, wglu_ref[...]) + bglu_ref[...])
    out = h + _mm(z * jax.nn.silu(gate), wout_ref[...])
    o_ref[...] = out.reshape(tl, nb, D_MODEL)

    @pl.when(l == pl.num_programs(1) - 1)
    def _():
        hout_ref[...] = hs_ref[...]


def _s5_layer(h_tb, h0, w, nb, tl):
    seq, bsz, _ = h_tb.shape
    hspec = pl.BlockSpec((tl, nb, D_MODEL), lambda b, l: (l, b, 0))
    sspec = pl.BlockSpec((nb, S5_STATE), lambda b, l: (b, 0))
    in_w = S5_BUNDLE * S5_GROUP
    return pl.pallas_call(
        _s5_kernel,
        grid=(bsz // nb, seq // tl),
        in_specs=[hspec, sspec, _const_spec((1, D_MODEL)), _const_spec((D_MODEL, 2 * D_MODEL)),
                  _const_spec((S5_NB, in_w, 2 * S5_HALF)), _const_spec((S5_NB, 2 * S5_HALF, in_w)),
                  _const_spec((1, S5_G * S5_N)), _const_spec((1, S5_G * S5_N)),
                  _const_spec((1, D_MODEL)), _const_spec((D_MODEL, D_MODEL)),
                  _const_spec((1, D_MODEL)), _const_spec((D_MODEL, D_MODEL))],
        out_specs=[hspec, sspec],
        out_shape=[jax.ShapeDtypeStruct(h_tb.shape, F32), jax.ShapeDtypeStruct((bsz, S5_STATE), F32)],
        scratch_shapes=[pltpu.VMEM((tl * nb, S5_STATE), F32), pltpu.VMEM((nb, S5_STATE), F32)],
        compiler_params=_params(),
        name="s5_layer",
    )(h_tb, h0, w["g"], w["w_in"], w["wb"], w["wc"], w["abr"], w["abi"], w["d"], w["w_glu"],
      w["b_glu"], w["w_out"])


def _s5_pack_state(re, im):
    b = re.shape[0]
    return jnp.concatenate([re.reshape(b, S5_NB, S5_HALF), im.reshape(b, S5_NB, S5_HALF)],
                           axis=-1).reshape(b, S5_STATE)


def _s5_unpack_state(st):
    b = st.shape[0]
    st = st.reshape(b, S5_NB, 2 * S5_HALF)
    return (st[..., :S5_HALF].reshape(b, S5_G, S5_N), st[..., S5_HALF:].reshape(b, S5_G, S5_N))


def _gdn_kernel(nbu, h_ref, s0_ref, cb_ref, g_ref, wqkv_ref, wz_ref, wab_ref, convw_ref, alog_ref,
                dtb_ref, ng_ref, fg_ref, wout_ref, o_ref, sout_ref, cbout_ref, s_ref, ext_ref):
    bb, tl, _ = o_ref.shape
    rows = bb * tl
    l = pl.program_id(1)

    @pl.when(l == 0)
    def _():
        s_ref[...] = jnp.broadcast_to(s0_ref[...], s_ref.shape)
        ext_ref[:, 0:GDN_PAD, :] = jnp.broadcast_to(cb_ref[...], (bb, GDN_PAD, GDN_QKV))

    h = jnp.concatenate([h_ref[:, i * D_MODEL:(i + 1) * D_MODEL] for i in range(bb)], axis=0)
    xn = _rms_rows(h, g_ref[...]).astype(BF16)
    ext_ref[:, GDN_PAD:GDN_PAD + tl, :] = _mm(xn, wqkv_ref[...]).reshape(bb, tl, GDN_QKV)
    first = GDN_PAD - (GDN_CONV - 1)
    conv = convw_ref[0:1, :] * ext_ref[:, first:first + tl, :]
    for j in range(1, GDN_CONV):
        conv = conv + convw_ref[j:j + 1, :] * ext_ref[:, first + j:first + j + tl, :]
    act = jax.nn.silu(conv).reshape(rows, GDN_QKV)
    heads = range(GDN_HEADS)
    seqs = range(bb)
    hsl = [slice(hd * GDN_DK, (hd + 1) * GDN_DK) for hd in heads]
    q_parts, k_parts = [], []
    for hd in heads:
        qh = act[:, hd * GDN_DK:(hd + 1) * GDN_DK]
        kh = act[:, GDN_QK + hd * GDN_DK:GDN_QK + (hd + 1) * GDN_DK]
        q_parts.append(qh * lax.rsqrt(jnp.sum(qh * qh, axis=-1, keepdims=True) + EPS) * GDN_DK ** -0.5)
        k_parts.append(kh * lax.rsqrt(jnp.sum(kh * kh, axis=-1, keepdims=True) + EPS))
    q = jnp.concatenate(q_parts, axis=1)
    k = jnp.concatenate(k_parts, axis=1)
    v = act[:, 2 * GDN_QK:]
    ab = _mm(xn, wab_ref[...])
    g = -jnp.exp(alog_ref[...]) * jax.nn.softplus(ab + dtb_ref[...])
    beta = jax.nn.sigmoid(ab)

    group = 4
    sup = nbu * tl
    width = group * sup
    n_units = bb // nbu
    n_sq = int(math.log2(tl)) - 1
    lg_sup = int(math.log2(sup))
    lg_tl = int(math.log2(tl))
    assert bb % nbu == 0 and 1 << lg_sup == sup and 1 << lg_tl == tl and n_sq >= 1
    tri = _block_masks(rows, tl)[0].astype(F32).astype(BF16)
    src = lax.broadcasted_iota(jnp.int32, (LANES, GDN_QK), 0)
    dst_head = lax.broadcasted_iota(jnp.int32, (LANES, GDN_QK), 1) >> int(math.log2(GDN_DK))
    spread_g = (src == dst_head).astype(F32).astype(BF16)
    spread_b = (src == dst_head + GDN_HEADS).astype(F32).astype(BF16)
    t4 = lax.broadcasted_iota(jnp.int32, (sup, width), 0)
    col4 = lax.broadcasted_iota(jnp.int32, (sup, width), 1)
    s4 = col4 & (sup - 1)
    same4 = (t4 >> lg_tl) == (s4 >> lg_tl)
    incl4 = same4 & (t4 >= s4)
    strict4 = same4 & (t4 > s4)
    eye4 = (t4 == s4).astype(F32)
    bd_mask = ((lax.broadcasted_iota(jnp.int32, (width, width), 0) >> lg_sup)
               == (lax.broadcasted_iota(jnp.int32, (width, width), 1) >> lg_sup))
    src4 = lax.broadcasted_iota(jnp.int32, (LANES, width), 0)
    head4 = lax.broadcasted_iota(jnp.int32, (LANES, width), 1) >> lg_sup
    spread4 = [(src4 == hg * group + head4).astype(F32).astype(BF16) for hg in range(GDN_HEADS // group)]

    gcum = _exact_dot(g, lambda p: _dot(tri, p))
    gcx = _exact_dot(gcum, lambda p: _dot(p, spread_g))
    bx = _exact_dot(beta, lambda p: _dot(p, spread_b))
    egx = jnp.exp(gcx)
    kb = k * bx
    qe = q * egx
    q_b, k_b, kb_b = _bf(q, k, kb)
    rhs_v = v * bx
    rhs_k = kb * egx
    kend, egl = [], []
    for b in seqs:
        gl = gcx[(b + 1) * tl - 1:(b + 1) * tl, :]
        kend.append(k[b * tl:(b + 1) * tl] * jnp.exp(gl - gcx[b * tl:(b + 1) * tl]))
        egl.append(jnp.exp(gl))

    keys = [(u, hg) for u in range(n_units) for hg in range(GDN_HEADS // group)]
    m4, att4 = {}, {}
    for u, hg in keys:
        ru = slice(u * sup, (u + 1) * sup)
        lanes4 = slice(hg * group * GDN_DK, (hg + 1) * group * GDN_DK)
        kdiag = []
        for hh in range(group):
            pieces = [jnp.zeros((sup, GDN_DK), BF16)] * group
            pieces[hh] = k_b[ru, hsl[hg * group + hh]]
            kdiag.append(jnp.concatenate(pieces, axis=1))
        kdiag = jnp.concatenate(kdiag, axis=0)
        kk = _dot_nt(kb_b[ru, lanes4], kdiag)
        qk = _dot_nt(q_b[ru, lanes4], kdiag)
        gcol = _exact_dot(gcum[ru], lambda p, hg=hg: _dot(p, spread4[hg]))
        grow = jnp.sum(eye4 * gcol, axis=0, keepdims=True)
        decay = jnp.exp(gcol - grow)
        m4[u, hg] = -jnp.where(strict4, kk * decay, 0.0)
        att4[u, hg] = jnp.where(incl4, qk * decay, 0.0)

    def block_diag(m_b):
        return jnp.where(bd_mask, jnp.concatenate([m_b] * group, axis=0), jnp.zeros((), BF16))

    p4 = {key: eye4 + m4[key] for key in keys}
    m_b = {key: m4[key].astype(BF16) for key in keys}
    m4 = {key: _dot(m_b[key], block_diag(m_b[key])) for key in keys}
    for _ in range(1, n_sq):
        m_b = {key: m4[key].astype(BF16) for key in keys}
        x = {key: _dot(jnp.concatenate([p4[key].astype(BF16), m_b[key]], axis=0), block_diag(m_b[key]))
             for key in keys}
        p4 = {key: p4[key] + x[key][:sup] for key in keys}
        m4 = {key: x[key][sup:] for key in keys}
    m_b = {key: m4[key].astype(BF16) for key in keys}
    p4 = {key: p4[key] + _dot(p4[key].astype(BF16), block_diag(m_b[key])) for key in keys}

    uw = [[None] * GDN_HEADS for _ in range(n_units)]
    for u, hg in keys:
        ru = slice(u * sup, (u + 1) * sup)
        p_b = p4[u, hg].astype(BF16)
        for hh in range(group):
            hd = hg * group + hh
            rhs = jnp.concatenate([rhs_v[ru, hsl[hd]], rhs_k[ru, hsl[hd]]], axis=1)
            uw[u][hd] = _dot(p_b[:, hh * sup:(hh + 1) * sup], rhs.astype(BF16))

    st = [[s_ref[b, hd] for hd in heads] for b in seqs]
    v_new = [[None] * GDN_HEADS for _ in seqs]
    qs = [[None] * GDN_HEADS for _ in seqs]
    for b in seqs:
        u, i = divmod(b, nbu)
        for hd in heads:
            wq = jnp.concatenate([uw[u][hd][i * tl:(i + 1) * tl, GDN_DV:], qe[b * tl:(b + 1) * tl, hsl[hd]]],
                                 axis=0)
            ws = _dot(wq.astype(BF16), st[b][hd].astype(BF16))
            v_new[b][hd] = uw[u][hd][i * tl:(i + 1) * tl, :GDN_DV] - ws[:tl]
            qs[b][hd] = ws[tl:]
    o_units = []
    for u in range(n_units):
        o_heads = []
        for hd in heads:
            hg, hh = divmod(hd, group)
            members = range(u * nbu, (u + 1) * nbu)
            vn = jnp.concatenate([v_new[b][hd] for b in members], axis=0)
            att = att4[u, hg][:, hh * sup:(hh + 1) * sup]
            o_heads.append(jnp.concatenate([qs[b][hd] for b in members], axis=0)
                           + _dot(att.astype(BF16), vn.astype(BF16)))
        o_units.append(jnp.concatenate(o_heads, axis=1))
    for b in seqs:
        for hd in heads:
            ke, vn = _cast_small(tl, kend[b][:, hsl[hd]], v_new[b][hd])
            s_ref[b, hd] = egl[b][:, hsl[hd]] * st[b][hd] + _dot_tn(ke, vn)
    o_all = jnp.concatenate(o_units, axis=0)

    z = _mm(xn, wz_ref[...])
    parts = []
    for hd in range(GDN_HEADS):
        parts.append(_rms_rows(o_all[:, hd * GDN_DV:(hd + 1) * GDN_DV], ng_ref[...]))
    on = jnp.concatenate(parts, axis=-1)
    out = h + _mm(on * jax.nn.silu(z), wout_ref[...])
    o_ref[...] = _rms_rows(out, fg_ref[...]).reshape(bb, tl, D_MODEL)

    @pl.when(l == pl.num_programs(1) - 1)
    def _():
        sout_ref[...] = s_ref[...]
        cbout_ref[...] = ext_ref[:, tl + GDN_PAD - (GDN_CONV - 1):tl + GDN_PAD, :]

    ext_ref[:, 0:GDN_PAD, :] = ext_ref[:, tl:tl + GDN_PAD, :]


def _gdn_layer(h_tm, s0, cb, w, bb, tl, nbu):
    seq = h_tm.shape[0]
    bsz = h_tm.shape[1] // D_MODEL
    shared = s0.shape[0] != bsz
    sblock = (bb, GDN_HEADS, GDN_DK, GDN_DV)
    cblock = (bb, GDN_PAD, GDN_QKV)
    hspec = pl.BlockSpec((bb, tl, D_MODEL), lambda b, l: (b, l, 0))
    return pl.pallas_call(
        functools.partial(_gdn_kernel, nbu),
        grid=(bsz // bb, seq // tl),
        in_specs=[pl.BlockSpec((tl, bb * D_MODEL), lambda b, l: (l, b)),
                  _state_spec((1,) + sblock[1:] if shared else sblock, shared),
                  _state_spec((1,) + cblock[1:] if shared else cblock, shared),
                  _const_spec((1, D_MODEL)), _const_spec((D_MODEL, GDN_QKV)),
                  _const_spec((D_MODEL, GDN_V)), _const_spec((D_MODEL, LANES)),
                  _const_spec((GDN_CONV, GDN_QKV)), _const_spec((1, LANES)), _const_spec((1, LANES)),
                  _const_spec((1, GDN_DV)), _const_spec((1, D_MODEL)), _const_spec((GDN_V, D_MODEL))],
        out_specs=[hspec, pl.BlockSpec(sblock, lambda b, l: (b, 0, 0, 0)),
                   pl.BlockSpec((bb, GDN_CONV - 1, GDN_QKV), lambda b, l: (b, 0, 0))],
        out_shape=[jax.ShapeDtypeStruct((bsz, seq, D_MODEL), F32),
                   jax.ShapeDtypeStruct((bsz,) + sblock[1:], F32),
                   jax.ShapeDtypeStruct((bsz, GDN_CONV - 1, GDN_QKV), F32)],
        scratch_shapes=[pltpu.VMEM(sblock, F32), pltpu.VMEM((bb, GDN_PAD + tl, GDN_QKV), F32)],
        compiler_params=_params(),
        name="gdn_layer",
    )(h_tm, s0, cb, w["g"], w["wqkv"], w["wz"], w["wab"], w["conv_w"], w["a_log"], w["dt_bias"],
      w["ng"], w["fg"], w["w_out"])


def _row(x, width=None):
    x = x.reshape(1, -1).astype(F32)
    if width is not None and x.shape[1] < width:
        x = jnp.pad(x, ((0, 0), (0, width - x.shape[1])))
    return x


def _pad_cols(w, width):
    return jnp.pad(w, ((0, 0), (0, width - w.shape[1])))


def _block_diag(x):
    nb, k, r, c = x.shape
    eye = jnp.eye(k, dtype=x.dtype)
    return jnp.einsum("jgrc,gh->jgrhc", x, eye).reshape(nb, k * r, k * c)


def _s5_weights(j, norm_g, s5_w_in, s5_b_re, s5_b_im, s5_c_re, s5_c_im, s5_d, s5_log_dt, s5_a_re,
                s5_a_im, s5_w_glu, s5_b_glu, s5_w_out):
    abr, abi, bbr, bbi = _s5_prep(s5_log_dt[j], s5_a_re[j], s5_a_im[j], s5_b_re[j], s5_b_im[j])
    shp = (S5_NB, S5_BUNDLE, S5_GROUP, S5_N)
    wb = jnp.concatenate([_block_diag(bbr.reshape(shp)), _block_diag(bbi.reshape(shp))], axis=-1)
    shp = (S5_NB, S5_BUNDLE, S5_N, S5_GROUP)
    wc = jnp.concatenate([_block_diag(s5_c_re[j].transpose(0, 2, 1).reshape(shp)),
                          _block_diag(-s5_c_im[j].transpose(0, 2, 1).reshape(shp))], axis=1)
    return dict(g=_row(norm_g), w_in=s5_w_in[j].astype(BF16), wb=wb.astype(BF16), wc=wc.astype(BF16),
                abr=_row(abr), abi=_row(abi), d=_row(s5_d[j]), w_glu=s5_w_glu[j].astype(BF16),
                b_glu=_row(s5_b_glu[j]), w_out=s5_w_out[j].astype(BF16))


def kernel(x_prompt, x_sample, state_pool, state_gla, state_s5_re, state_s5_im, state_gdn, state_gdn_conv, meta_tokens, norm_g, final_norm_g, pool_w_in, pool_w_grp, pool_scale, pool_w_out, gla_w_in, gla_w_gk, gla_b_gk, gla_norm_g, gla_w_out, s5_w_in, s5_b_re, s5_b_im, s5_c_re, s5_c_im, s5_d, s5_log_dt, s5_a_re, s5_a_im, s5_w_glu, s5_b_glu, s5_w_out, gdn_w_in, gdn_conv_w, gdn_a_log, gdn_dt_bias, gdn_norm_g, gdn_w_out):
    bp = x_prompt.shape[0]
    bs, ls, _ = x_sample.shape

    wp = dict(g=_row(norm_g[0]), w_in=pool_w_in[0].astype(BF16), w_grp=pool_w_grp[0].astype(BF16),
              scale=_row(pool_scale[0]), w_out=pool_w_out[0].astype(BF16))
    gw = gla_w_in[0]
    wg = dict(g=_row(norm_g[1]), wq=gw[:, :GLA_QK].astype(BF16), wk=gw[:, GLA_QK:2 * GLA_QK].astype(BF16),
              wv=gw[:, 2 * GLA_QK:2 * GLA_QK + GLA_V].astype(BF16),
              wgate=gw[:, 2 * GLA_QK + GLA_V:2 * GLA_QK + 2 * GLA_V].astype(BF16),
              wglow=_pad_cols(gw[:, 2 * GLA_QK + 2 * GLA_V:], LANES).astype(BF16),
              wgk=jnp.pad(gla_w_gk[0], ((0, LANES - GLA_RANK), (0, 0))).astype(BF16),
              bgk=_row(gla_b_gk[0]), ng=_row(gla_norm_g[0]), w_out=gla_w_out[0].astype(BF16))
    ws = _s5_weights(0, norm_g[2], s5_w_in, s5_b_re, s5_b_im, s5_c_re, s5_c_im, s5_d, s5_log_dt,
                     s5_a_re, s5_a_im, s5_w_glu, s5_b_glu, s5_w_out)
    dw = gdn_w_in[0]
    wd = dict(g=_row(norm_g[3]), wqkv=dw[:, :GDN_QKV].astype(BF16),
              wz=dw[:, GDN_QKV:GDN_QKV + GDN_V].astype(BF16),
              wab=_pad_cols(dw[:, GDN_QKV + GDN_V:], LANES).astype(BF16), conv_w=gdn_conv_w[0],
              a_log=_row(gdn_a_log[0], LANES), dt_bias=_row(gdn_dt_bias[0], LANES),
              ng=_row(gdn_norm_g[0]), fg=_row(final_norm_g), w_out=gdn_w_out[0].astype(BF16))

    def pool_hist(st):
        return jnp.pad(st, ((0, 0), (POOL_PAD - POOL_BUF, 0), (0, 0)))

    def conv_hist(st):
        return jnp.pad(st, ((0, 0), (GDN_PAD - (GDN_CONV - 1), 0), (0, 0)))

    def run(h, pool_st, n_valid, gla_st, s5_st, gdn_st, conv_st, blocks):
        (pb, pt), (gb, gt), (sb, stl), (db, dtl, dn) = blocks
        bsz, seq, _ = h.shape
        h, pool_new = _pool_layer(h, pool_hist(pool_st), n_valid, wp, pb, pt)
        h_tm, gla_new = _gla_layer(h, gla_st, wg, gb, gt)
        h_tb, s5_new = _s5_layer(h_tm.reshape(seq, bsz, D_MODEL), s5_st, ws, sb, stl)
        y, gdn_new, conv_new = _gdn_layer(h_tb.reshape(seq, bsz * D_MODEL), gdn_st, conv_hist(conv_st),
                                          wd, db, dtl, dn)
        return y, pool_new, gla_new, s5_new, gdn_new, conv_new

    hm = meta_tokens.astype(F32)[None]
    zeros = lambda *s: jnp.zeros(s, F32)
    hm, m_pool = _pool_layer(hm, zeros(1, POOL_PAD, D_MODEL), 0, wp, 1, N_META)
    hm, m_gla = _gla_layer(hm, zeros(1, GLA_HEADS, GLA_DK, GLA_DV), wg, 1, N_META)
    hm_tb, m_s5 = _s5_layer(jnp.broadcast_to(hm[:, None, :], (N_META, SUBLANES, D_MODEL)),
                            zeros(SUBLANES, S5_STATE), ws, SUBLANES, N_META)
    _, m_gdn, m_conv = _gdn_layer(hm_tb[:, 0], zeros(1, GDN_HEADS, GDN_DK, GDN_DV),
                                  zeros(1, GDN_PAD, GDN_QKV), wd, 1, N_META, 1)

    yp, pool_p, gla_p, s5_p, gdn_p, conv_p = run(
        x_prompt, m_pool, N_META, m_gla, jnp.broadcast_to(m_s5[0:1], (bp, S5_STATE)), m_gdn, m_conv,
        ((1, 512), (4, 64), (bp, 32), (4, 64, 1)))
    ys, pool_s, gla_s, s5_s, gdn_s, conv_s = run(
        x_sample, state_pool[0], POOL_BUF, state_gla[0], _s5_pack_state(state_s5_re[0], state_s5_im[0]),
        state_gdn[0], state_gdn_conv[0], ((16, ls), (8, ls), (32, ls), (8, ls, 8)))

    s5r_p, s5i_p = _s5_unpack_state(s5_p)
    s5r_s, s5i_s = _s5_unpack_state(s5_s)
    return (yp, ys, pool_p[None], pool_s[None], gla_p[None], gla_s[None],
            s5r_p[None], s5i_p[None], s5r_s[None], s5i_s[None],
            gdn_p[None], conv_p[None], gdn_s[None], conv_s[None])
```

```python
import functools
import math

import jax
import jax.numpy as jnp
from jax import lax
from jax.experimental import pallas as pl
from jax.experimental.pallas import tpu as pltpu

F32 = jnp.float32
BF16 = jnp.bfloat16
HIGHEST = lax.Precision.HIGHEST

D_MODEL = 1024
EPS = 1e-6
N_META = 16

POOL_WINDOWS = (2, 4, 8, 16)
POOL_GROUP = D_MODEL // len(POOL_WINDOWS)
POOL_BUF = max(POOL_WINDOWS) - 1
POOL_PAD = POOL_BUF + 1

GLA_HEADS = 4
GLA_DK = 128
GLA_DV = 256
GLA_QK = GLA_HEADS * GLA_DK
GLA_V = GLA_HEADS * GLA_DV
GLA_RANK = 16
GLA_GATE_NORM = 16.0

S5_GROUP = 16
S5_G = D_MODEL // S5_GROUP
S5_N = 64
S5_BUNDLE = 8
S5_NB = S5_G // S5_BUNDLE
S5_HALF = S5_BUNDLE * S5_N
S5_STATE = 2 * S5_G * S5_N

GDN_HEADS = 8
GDN_DK = 128
GDN_DV = 128
GDN_CONV = 4
GDN_QK = GDN_HEADS * GDN_DK
GDN_V = GDN_HEADS * GDN_DV
GDN_QKV = 2 * GDN_QK + GDN_V
GDN_PAD = 8

LANES = 128
SUBLANES = 8
VMEM_LIMIT = 52 * 1024 * 1024

_NT = (((1,), (1,)), ((), ()))
_TN = (((0,), (0,)), ((), ()))


def _rms_rows(x, g):
    return x * lax.rsqrt(jnp.mean(x * x, axis=-1, keepdims=True) + EPS) * g


def _mm(a, w):
    return jnp.dot(a.astype(BF16), w, preferred_element_type=F32)


def _cast_small(c, *xs):
    if c % 16 == 0:
        return tuple(x.astype(BF16) for x in xs)
    return xs


def _bf(*xs):
    return tuple(x.astype(BF16) for x in xs)


def _block_masks(n, c):
    r = lax.broadcasted_iota(jnp.int32, (n, n), 0)
    s = lax.broadcasted_iota(jnp.int32, (n, n), 1)
    sh = int(math.log2(c))
    same = (r >> sh) == (s >> sh)
    return same & (r >= s), same & (r > s), r == s


def _exact_dot(x, dot_piece):
    hi = x.astype(BF16)
    r = x - hi.astype(F32)
    mid = r.astype(BF16)
    lo = (r - mid.astype(F32)).astype(BF16)
    return dot_piece(hi) + dot_piece(mid) + dot_piece(lo)


def _dot(a, b):
    return jnp.dot(a, b, preferred_element_type=F32)


def _dot_nt(a, b):
    return lax.dot_general(a, b, _NT, preferred_element_type=F32)


def _dot_tn(a, b):
    return lax.dot_general(a, b, _TN, preferred_element_type=F32)


def _pool_kernel(n_valid, h_ref, buf_ref, g_ref, win_ref, wgrp_ref, scale_ref, wout_ref,
                 o_ref, st_ref, z_ref):
    bb, tl, _ = h_ref.shape
    rows = bb * tl
    l = pl.program_id(1)

    @pl.when(l == 0)
    def _():
        z_ref[:, POOL_PAD - POOL_BUF:POOL_PAD, :] = jnp.broadcast_to(buf_ref[...], (bb, POOL_BUF, D_MODEL))

    h = h_ref[...].reshape(rows, D_MODEL)
    xn = _rms_rows(h, g_ref[...])
    ug = _mm(xn, win_ref[...])
    gate = ug[:, D_MODEL:]
    z_ref[:, POOL_PAD:POOL_PAD + tl, :] = ug[:, :D_MODEL].reshape(bb, tl, D_MODEL)

    t = (l * tl + lax.broadcasted_iota(jnp.int32, (1, tl, POOL_GROUP), 1)).astype(F32)
    parts = []
    for gi, w in enumerate(POOL_WINDOWS):
        lo = gi * POOL_GROUP
        cur = z_ref[:, POOL_PAD:POOL_PAD + tl, lo:lo + POOL_GROUP]
        acc = cur
        for j in range(1, w):
            acc = acc + z_ref[:, POOL_PAD - j:POOL_PAD - j + tl, lo:lo + POOL_GROUP]
        cnt = jnp.minimum(float(w), t + (1.0 + n_valid))
        mixed = acc / cnt - cur
        parts.append(_mm(mixed.reshape(rows, POOL_GROUP), wgrp_ref[gi]))
    mixed = jnp.concatenate(parts, axis=-1) * scale_ref[...]
    y = _mm(mixed * jax.nn.silu(gate), wout_ref[...])
    o_ref[...] = (h + y).reshape(bb, tl, D_MODEL)

    @pl.when(l == pl.num_programs(1) - 1)
    def _():
        st_ref[...] = z_ref[:, tl + 1:tl + POOL_PAD, :]

    z_ref[:, 0:POOL_PAD, :] = z_ref[:, tl:tl + POOL_PAD, :]


def _const_spec(shape):
    nd = len(shape)
    return pl.BlockSpec(shape, lambda b, l: (0,) * nd)


def _state_spec(block, shared):
    nd = len(block)
    if shared:
        return pl.BlockSpec(block, lambda b, l: (0,) * nd)
    return pl.BlockSpec(block, lambda b, l: (b,) + (0,) * (nd - 1))


def _params():
    return pltpu.CompilerParams(dimension_semantics=("parallel", "arbitrary"),
                                vmem_limit_bytes=VMEM_LIMIT)


def _pool_layer(h, buf, n_valid, w, bb, tl):
    bsz, seq, _ = h.shape
    shared = buf.shape[0] != bsz
    hspec = pl.BlockSpec((bb, tl, D_MODEL), lambda b, l: (b, l, 0))
    return pl.pallas_call(
        functools.partial(_pool_kernel, float(n_valid)),
        grid=(bsz // bb, seq // tl),
        in_specs=[hspec, _state_spec((1 if shared else bb, POOL_BUF, D_MODEL), shared),
                  _const_spec((1, D_MODEL)), _const_spec((D_MODEL, 2 * D_MODEL)),
                  _const_spec((len(POOL_WINDOWS), POOL_GROUP, POOL_GROUP)),
                  _const_spec((1, D_MODEL)), _const_spec((D_MODEL, D_MODEL))],
        out_specs=[hspec, pl.BlockSpec((bb, POOL_BUF, D_MODEL), lambda b, l: (b, 0, 0))],
        out_shape=[jax.ShapeDtypeStruct(h.shape, F32),
                   jax.ShapeDtypeStruct((bsz, POOL_BUF, D_MODEL), F32)],
        scratch_shapes=[pltpu.VMEM((bb, POOL_PAD + tl, D_MODEL), F32)],
        compiler_params=_params(),
        name="pool_layer",
    )(h, buf, w["g"], w["w_in"], w["w_grp"], w["scale"], w["w_out"])


def _gla_kernel(h_ref, s0_ref, g_ref, wq_ref, wk_ref, wv_ref, wgate_ref, wglow_ref, wgk_ref,
                bgk_ref, ng_ref, wout_ref, o_ref, s_ref):
    bb, tl, _ = h_ref.shape
    rows = bb * tl

    @pl.when(pl.program_id(1) == 0)
    def _():
        s_ref[...] = jnp.broadcast_to(s0_ref[...], s_ref.shape)

    h = h_ref[...].reshape(rows, D_MODEL)
    xn = _rms_rows(h, g_ref[...]).astype(BF16)
    q = _mm(xn, wq_ref[...]) * GLA_DK ** -0.5
    k = _mm(xn, wk_ref[...])
    v = _mm(xn, wv_ref[...])
    glow = _mm(xn, wglow_ref[...])
    gk = jax.nn.log_sigmoid(_mm(glow, wgk_ref[...]) + bgk_ref[...]) / GLA_GATE_NORM

    incl, _, _ = _block_masks(rows, tl)
    tri = incl.astype(F32).astype(BF16)
    eye_k = (lax.broadcasted_iota(jnp.int32, (GLA_DK, GLA_DK), 0)
             == lax.broadcasted_iota(jnp.int32, (GLA_DK, GLA_DK), 1))
    heads = range(GLA_HEADS)
    seqs = range(bb)
    ksl = [slice(hd * GLA_DK, (hd + 1) * GLA_DK) for hd in heads]
    vsl = [slice(hd * GLA_DV, (hd + 1) * GLA_DV) for hd in heads]
    blks = [slice(b * tl, (b + 1) * tl) for b in seqs]

    bc = _exact_dot(gk, lambda p: _dot(tri, p))
    qg = q * jnp.exp(bc)
    kg = k * jnp.exp(-bc)
    kd, ebl = [], []
    for b in seqs:
        bl = bc[(b + 1) * tl - 1:(b + 1) * tl, :]
        kd.append(k[blks[b]] * jnp.exp(bl - bc[blks[b]]))
        ebl.append([jnp.exp(jnp.sum(jnp.where(eye_k, jnp.broadcast_to(bl[:, ksl[hd]], (GLA_DK, GLA_DK)), 0.0),
                                    axis=1, keepdims=True)) for hd in heads])
    qg_b, kg_b, v_b = _bf(qg, kg, v)
    att = [jnp.where(incl, _dot_nt(qg_b[:, ksl[hd]], kg_b[:, ksl[hd]]), 0.0) for hd in heads]
    o = [_dot(att[hd].astype(BF16), v_b[:, vsl[hd]]) for hd in heads]
    st = [[s_ref[b, hd] for hd in heads] for b in seqs]
    o_st = [[None] * GLA_HEADS for _ in seqs]
    for b in seqs:
        for hd in heads:
            qi, si = _cast_small(tl, qg[blks[b], ksl[hd]], st[b][hd])
            o_st[b][hd] = _dot(qi, si)
    for b in seqs:
        for hd in heads:
            kdi, vi = _cast_small(tl, kd[b][:, ksl[hd]], v[blks[b], vsl[hd]])
            s_ref[b, hd] = ebl[b][hd] * st[b][hd] + _dot_tn(kdi, vi)
    o_all = jnp.concatenate(
        [o[hd] + jnp.concatenate([o_st[b][hd] for b in seqs], axis=0) for hd in heads], axis=1)

    gate = _mm(xn, wgate_ref[...])
    parts = []
    for hd in range(GLA_HEADS):
        parts.append(_rms_rows(o_all[:, hd * GLA_DV:(hd + 1) * GLA_DV], ng_ref[...]))
    on = jnp.concatenate(parts, axis=-1)
    out = h + _mm(on * jax.nn.silu(gate), wout_ref[...])
    for i in range(bb):
        o_ref[:, i * D_MODEL:(i + 1) * D_MODEL] = out[i * tl:(i + 1) * tl]


def _gla_layer(h, s0, w, bb, tl):
    bsz, seq, _ = h.shape
    shared = s0.shape[0] != bsz
    sblock = (bb, GLA_HEADS, GLA_DK, GLA_DV)
    hspec = pl.BlockSpec((bb, tl, D_MODEL), lambda b, l: (b, l, 0))
    return pl.pallas_call(
        _gla_kernel,
        grid=(bsz // bb, seq // tl),
        in_specs=[hspec, _state_spec((1,) + sblock[1:] if shared else sblock, shared),
                  _const_spec((1, D_MODEL)),
                  _const_spec((D_MODEL, GLA_QK)), _const_spec((D_MODEL, GLA_QK)),
                  _const_spec((D_MODEL, GLA_V)), _const_spec((D_MODEL, GLA_V)),
                  _const_spec((D_MODEL, LANES)), _const_spec((LANES, GLA_QK)),
                  _const_spec((1, GLA_QK)), _const_spec((1, GLA_DV)),
                  _const_spec((GLA_V, D_MODEL))],
        out_specs=[pl.BlockSpec((tl, bb * D_MODEL), lambda b, l: (l, b)),
                   pl.BlockSpec(sblock, lambda b, l: (b, 0, 0, 0))],
        out_shape=[jax.ShapeDtypeStruct((seq, bsz * D_MODEL), F32),
                   jax.ShapeDtypeStruct((bsz,) + sblock[1:], F32)],
        compiler_params=_params(),
        name="gla_layer",
    )(h, s0, w["g"], w["wq"], w["wk"], w["wv"], w["wgate"], w["wglow"], w["wgk"], w["bgk"],
      w["ng"], w["w_out"])


def _s5_prep_kernel(logdt_ref, are_ref, aim_ref, bre_ref, bim_ref, abr_ref, abi_ref, bbr_ref, bbi_ref):
    dt = jnp.exp(logdt_ref[...])
    lr = are_ref[...]
    li = aim_ref[...]
    mag = jnp.exp(lr * dt)
    abr = mag * jnp.cos(li * dt)
    abi = mag * jnp.sin(li * dt)
    den = lr * lr + li * li
    cr = ((abr - 1.0) * lr + abi * li) / den
    ci = (abi * lr - (abr - 1.0) * li) / den
    abr_ref[...] = abr
    abi_ref[...] = abi
    br = bre_ref[...]
    bi = bim_ref[...]
    bbr_ref[...] = cr[:, None, :] * br - ci[:, None, :] * bi
    bbi_ref[...] = cr[:, None, :] * bi + ci[:, None, :] * br


def _s5_prep(log_dt, a_re, a_im, b_re, b_im):
    gn = jax.ShapeDtypeStruct((S5_G, S5_N), F32)
    gcn = jax.ShapeDtypeStruct((S5_G, S5_GROUP, S5_N), F32)
    return pl.pallas_call(_s5_prep_kernel, out_shape=[gn, gn, gcn, gcn], name="s5_discretize")(
        log_dt.reshape(S5_G, 1), a_re, a_im, b_re.transpose(0, 2, 1), b_im.transpose(0, 2, 1))


def _s5_kernel(h_ref, h0_ref, g_ref, win_ref, wb_ref, wc_ref, abr_ref, abi_ref, dskip_ref,
               wglu_ref, bglu_ref, wout_ref, o_ref, hs_ref):
    tl, nb, _ = h_ref.shape
    rows = tl * nb
    ngrp = nb // SUBLANES

    @pl.when(pl.program_id(1) == 0)
    def _():
        hs_ref[...] = h0_ref[...]

    h = h_ref[...].reshape(rows, D_MODEL)
    xn = _rms_rows(h, g_ref[...]).astype(BF16)
    ug = _mm(xn, win_ref[...])
    u = ug[:, :D_MODEL]
    gate = ug[:, D_MODEL:]
    ub = u.astype(BF16)
    width = 2 * S5_HALF
    in_w = S5_BUNDLE * S5_GROUP

    def project_in(j):
        return jnp.dot(ub[:, j * in_w:(j + 1) * in_w], wb_ref[j], preferred_element_type=F32)

    parts = []
    bu_next = project_in(0)
    for j in range(S5_NB):
        bu = bu_next
        if j + 1 < S5_NB:
            bu_next = project_in(j + 1)
        re = slice(j * width, j * width + S5_HALF)
        im = slice(j * width + S5_HALF, (j + 1) * width)
        ar = jnp.broadcast_to(abr_ref[:, j * S5_HALF:(j + 1) * S5_HALF], (SUBLANES, S5_HALF))
        ai = jnp.broadcast_to(abi_ref[:, j * S5_HALF:(j + 1) * S5_HALF], (SUBLANES, S5_HALF))
        tiles = [None] * (tl * ngrp)
        for bg in range(ngrp):
            grp = slice(bg * SUBLANES, (bg + 1) * SUBLANES)
            hr = hs_ref[grp, re]
            hi = hs_ref[grp, im]
            for t in range(tl):
                r = t * nb + bg * SUBLANES
                hr, hi = (ar * hr - ai * hi + bu[r:r + SUBLANES, :S5_HALF],
                          ar * hi + ai * hr + bu[r:r + SUBLANES, S5_HALF:])
                tiles[t * ngrp + bg] = jnp.concatenate([hr, hi], axis=1)
            hs_ref[grp, re] = hr
            hs_ref[grp, im] = hi
        parts.append(_mm(jnp.concatenate(tiles, axis=0), wc_ref[j]))
    y = jnp.concatenate(parts, axis=-1) + dskip_ref[...] * u
    z = jax.nn.gelu(y)
    z = z * jax.nn.sigmoid(_mm(z, wglu_ref[...]) + bglu_ref[...])
    out = h + _mm(z * jax.nn.silu(gate), wout_ref[...])
    o_ref[...] = out.reshape(tl, nb, D_MODEL)


def _s5_layer(h_tb, h0, w, nb, tl):
    seq, bsz, _ = h_tb.shape
    hspec = pl.BlockSpec((tl, nb, D_MODEL), lambda b, l: (l, b, 0))
    sspec = pl.BlockSpec((nb, S5_STATE), lambda b, l: (b, 0))
    in_w = S5_BUNDLE * S5_GROUP
    return pl.pallas_call(
        _s5_kernel,
        grid=(bsz // nb, seq // tl),
        in_specs=[hspec, sspec, _const_spec((1, D_MODEL)), _const_spec((D_MODEL, 2 * D_MODEL)),
                  _const_spec((S5_NB, in_w, 2 * S5_HALF)), _const_spec((S5_NB, 2 * S5_HALF, in_w)),
                  _const_spec((1, S5_G * S5_N)), _const_spec((1, S5_G * S5_N)),
                  _const_spec((1, D_MODEL)), _const_spec((D_MODEL, D_MODEL)),
                  _const_spec((1, D_MODEL)), _const_spec((D_MODEL, D_MODEL))],
        out_specs=[hspec, sspec],
        out_shape=[jax.ShapeDtypeStruct(h_tb.shape, F32), jax.ShapeDtypeStruct((bsz, S5_STATE), F32)],
        compiler_params=_params(),
        name="s5_layer",
    )(h_tb, h0, w["g"], w["w_in"], w["wb"], w["wc"], w["abr"], w["abi"], w["d"], w["w_glu"],
      w["b_glu"], w["w_out"])


def _s5_pack_state(re, im):
    b = re.shape[0]
    return jnp.concatenate([re.reshape(b, S5_NB, S5_HALF), im.reshape(b, S5_NB, S5_HALF)],
                           axis=-1).reshape(b, S5_STATE)


def _s5_unpack_state(st):
    b = st.shape[0]
    st = st.reshape(b, S5_NB, 2 * S5_HALF)
    return (st[..., :S5_HALF].reshape(b, S5_G, S5_N), st[..., S5_HALF:].reshape(b, S5_G, S5_N))


def _gdn_kernel(nbu, h_ref, s0_ref, cb_ref, g_ref, wqkv_ref, wz_ref, wab_ref, convw_ref, alog_ref,
                dtb_ref, ng_ref, fg_ref, wout_ref, o_ref, s_ref, cbout_ref, ext_ref):
    bb, tl, _ = o_ref.shape
    rows = bb * tl
    l = pl.program_id(1)

    @pl.when(l == 0)
    def _():
        s_ref[...] = jnp.broadcast_to(s0_ref[...], s_ref.shape)
        ext_ref[:, GDN_PAD - (GDN_CONV - 1):GDN_PAD, :] = jnp.broadcast_to(
            cb_ref[...], (bb, GDN_CONV - 1, GDN_QKV))

    h = jnp.concatenate([h_ref[:, i * D_MODEL:(i + 1) * D_MODEL] for i in range(bb)], axis=0)
    xn = _rms_rows(h, g_ref[...]).astype(BF16)
    ext_ref[:, GDN_PAD:GDN_PAD + tl, :] = _mm(xn, wqkv_ref[...]).reshape(bb, tl, GDN_QKV)
    first = GDN_PAD - (GDN_CONV - 1)
    conv = convw_ref[0:1, :] * ext_ref[:, first:first + tl, :]
    for j in range(1, GDN_CONV):
        conv = conv + convw_ref[j:j + 1, :] * ext_ref[:, first + j:first + j + tl, :]
    act = jax.nn.silu(conv).reshape(rows, GDN_QKV)
    heads = range(GDN_HEADS)
    seqs = range(bb)
    hsl = [slice(hd * GDN_DK, (hd + 1) * GDN_DK) for hd in heads]
    q_parts, k_parts = [], []
    for hd in heads:
        qh = act[:, hd * GDN_DK:(hd + 1) * GDN_DK]
        kh = act[:, GDN_QK + hd * GDN_DK:GDN_QK + (hd + 1) * GDN_DK]
        q_parts.append(qh * lax.rsqrt(jnp.sum(qh * qh, axis=-1, keepdims=True) + EPS) * GDN_DK ** -0.5)
        k_parts.append(kh * lax.rsqrt(jnp.sum(kh * kh, axis=-1, keepdims=True) + EPS))
    q = jnp.concatenate(q_parts, axis=1)
    k = jnp.concatenate(k_parts, axis=1)
    v = act[:, 2 * GDN_QK:]
    ab = _mm(xn, wab_ref[...])
    g = -jnp.exp(alog_ref[...]) * jax.nn.softplus(ab + dtb_ref[...])
    beta = jax.nn.sigmoid(ab)

    group = 4
    sup = nbu * tl
    width = group * sup
    n_units = bb // nbu
    n_sq = int(math.log2(tl)) - 1
    lg_sup = int(math.log2(sup))
    lg_tl = int(math.log2(tl))
    assert bb % nbu == 0 and 1 << lg_sup == sup and 1 << lg_tl == tl and n_sq >= 1
    tri = _block_masks(rows, tl)[0].astype(F32).astype(BF16)
    t4 = lax.broadcasted_iota(jnp.int32, (sup, width), 0)
    col4 = lax.broadcasted_iota(jnp.int32, (sup, width), 1)
    s4 = col4 & (sup - 1)
    same4 = (t4 >> lg_tl) == (s4 >> lg_tl)
    incl4 = same4 & (t4 >= s4)
    strict4 = same4 & (t4 > s4)
    eye4 = (t4 == s4).astype(F32)
    bd_mask = ((lax.broadcasted_iota(jnp.int32, (width, width), 0) >> lg_sup)
               == (lax.broadcasted_iota(jnp.int32, (width, width), 1) >> lg_sup))
    head4 = col4 >> lg_sup

    def per_head_lanes(x, first):
        return jnp.concatenate(
            [jnp.broadcast_to(x[:, first + hd:first + hd + 1], (rows, GDN_DK)) for hd in heads], axis=1)

    gcum = _exact_dot(g, lambda p: _dot(tri, p))
    gcx = per_head_lanes(gcum, 0)
    bx = per_head_lanes(beta, GDN_HEADS)
    egx = jnp.exp(gcx)
    kb = k * bx
    qe = q * egx
    q_b, k_b, kb_b = _bf(q, k, kb)
    rhs_v = v * bx
    rhs_k = kb * egx
    kend, egl = [], []
    for b in seqs:
        gl = gcx[(b + 1) * tl - 1:(b + 1) * tl, :]
        kend.append(k[b * tl:(b + 1) * tl] * jnp.exp(gl - gcx[b * tl:(b + 1) * tl]))
        egl.append(jnp.exp(gl))

    keys = [(u, hg) for u in range(n_units) for hg in range(GDN_HEADS // group)]
    m4, att4 = {}, {}
    for u, hg in keys:
        ru = slice(u * sup, (u + 1) * sup)
        lanes4 = slice(hg * group * GDN_DK, (hg + 1) * group * GDN_DK)
        kdiag = []
        for hh in range(group):
            pieces = [jnp.zeros((sup, GDN_DK), BF16)] * group
            pieces[hh] = k_b[ru, hsl[hg * group + hh]]
            kdiag.append(jnp.concatenate(pieces, axis=1))
        kdiag = jnp.concatenate(kdiag, axis=0)
        kk = _dot_nt(kb_b[ru, lanes4], kdiag)
        qk = _dot_nt(q_b[ru, lanes4], kdiag)
        gcol = None
        for hh in range(group):
            rep = gcx[ru, hsl[hg * group + hh]]
            rep = rep[:, :width] if width <= GDN_DK else jnp.concatenate([rep] * (width // GDN_DK), axis=1)
            gcol = rep if gcol is None else jnp.where(head4 == hh, rep, gcol)
        grow = jnp.sum(eye4 * gcol, axis=0, keepdims=True)
        decay = jnp.exp(gcol - grow)
        m4[u, hg] = -jnp.where(strict4, kk * decay, 0.0)
        att4[u, hg] = jnp.where(incl4, qk * decay, 0.0)

    def block_diag(m_b):
        return jnp.where(bd_mask, jnp.concatenate([m_b] * group, axis=0), jnp.zeros((), BF16))

    p4 = {key: eye4 + m4[key] for key in keys}
    m_b = {key: m4[key].astype(BF16) for key in keys}
    m4 = {key: _dot(m_b[key], block_diag(m_b[key])) for key in keys}
    for _ in range(1, n_sq):
        m_b = {key: m4[key].astype(BF16) for key in keys}
        x = {key: _dot(jnp.concatenate([p4[key].astype(BF16), m_b[key]], axis=0), block_diag(m_b[key]))
             for key in keys}
        p4 = {key: p4[key] + x[key][:sup] for key in keys}
        m4 = {key: x[key][sup:] for key in keys}
    m_b = {key: m4[key].astype(BF16) for key in keys}
    p4 = {key: p4[key] + _dot(p4[key].astype(BF16), block_diag(m_b[key])) for key in keys}

    uw = [[None] * GDN_HEADS for _ in range(n_units)]
    for u, hg in keys:
        ru = slice(u * sup, (u + 1) * sup)
        p_b = p4[u, hg].astype(BF16)
        for hh in range(group):
            hd = hg * group + hh
            rhs = jnp.concatenate([rhs_v[ru, hsl[hd]], rhs_k[ru, hsl[hd]]], axis=1)
            uw[u][hd] = _dot(p_b[:, hh * sup:(hh + 1) * sup], rhs.astype(BF16))

    st = [[s_ref[b, hd] for hd in heads] for b in seqs]
    v_new = [[None] * GDN_HEADS for _ in seqs]
    qs = [[None] * GDN_HEADS for _ in seqs]
    for b in seqs:
        u, i = divmod(b, nbu)
        for hd in heads:
            wq = jnp.concatenate([uw[u][hd][i * tl:(i + 1) * tl, GDN_DV:], qe[b * tl:(b + 1) * tl, hsl[hd]]],
                                 axis=0)
            ws = _dot(wq.astype(BF16), st[b][hd].astype(BF16))
            v_new[b][hd] = uw[u][hd][i * tl:(i + 1) * tl, :GDN_DV] - ws[:tl]
            qs[b][hd] = ws[tl:]
    o_units = []
    for u in range(n_units):
        o_heads = []
        for hd in heads:
            hg, hh = divmod(hd, group)
            members = range(u * nbu, (u + 1) * nbu)
            vn = jnp.concatenate([v_new[b][hd] for b in members], axis=0)
            att = att4[u, hg][:, hh * sup:(hh + 1) * sup]
            o_heads.append(jnp.concatenate([qs[b][hd] for b in members], axis=0)
                           + _dot(att.astype(BF16), vn.astype(BF16)))
        o_units.append(jnp.concatenate(o_heads, axis=1))
    for b in seqs:
        for hd in heads:
            ke, vn = _cast_small(tl, kend[b][:, hsl[hd]], v_new[b][hd])
            s_ref[b, hd] = egl[b][:, hsl[hd]] * st[b][hd] + _dot_tn(ke, vn)
    o_all = jnp.concatenate(o_units, axis=0)

    z = _mm(xn, wz_ref[...])
    parts = []
    for hd in range(GDN_HEADS):
        parts.append(_rms_rows(o_all[:, hd * GDN_DV:(hd + 1) * GDN_DV], ng_ref[...]))
    on = jnp.concatenate(parts, axis=-1)
    out = h + _mm(on * jax.nn.silu(z), wout_ref[...])
    o_ref[...] = _rms_rows(out, fg_ref[...]).reshape(bb, tl, D_MODEL)

    @pl.when(l == pl.num_programs(1) - 1)
    def _():
        cbout_ref[...] = ext_ref[:, tl + GDN_PAD - (GDN_CONV - 1):tl + GDN_PAD, :]

    ext_ref[:, 0:GDN_PAD, :] = ext_ref[:, tl:tl + GDN_PAD, :]


def _gdn_layer(h_tm, s0, cb, w, bb, tl, nbu):
    seq = h_tm.shape[0]
    bsz = h_tm.shape[1] // D_MODEL
    shared = s0.shape[0] != bsz
    sblock = (bb, GDN_HEADS, GDN_DK, GDN_DV)
    cblock = (bb, GDN_CONV - 1, GDN_QKV)
    hspec = pl.BlockSpec((bb, tl, D_MODEL), lambda b, l: (b, l, 0))
    return pl.pallas_call(
        functools.partial(_gdn_kernel, nbu),
        grid=(bsz // bb, seq // tl),
        in_specs=[pl.BlockSpec((tl, bb * D_MODEL), lambda b, l: (l, b)),
                  _state_spec((1,) + sblock[1:] if shared else sblock, shared),
                  _state_spec((1,) + cblock[1:] if shared else cblock, shared),
                  _const_spec((1, D_MODEL)), _const_spec((D_MODEL, GDN_QKV)),
                  _const_spec((D_MODEL, GDN_V)), _const_spec((D_MODEL, LANES)),
                  _const_spec((GDN_CONV, GDN_QKV)), _const_spec((1, LANES)), _const_spec((1, LANES)),
                  _const_spec((1, GDN_DV)), _const_spec((1, D_MODEL)), _const_spec((GDN_V, D_MODEL))],
        out_specs=[hspec, pl.BlockSpec(sblock, lambda b, l: (b, 0, 0, 0)),
                   pl.BlockSpec((bb, GDN_CONV - 1, GDN_QKV), lambda b, l: (b, 0, 0))],
        out_shape=[jax.ShapeDtypeStruct((bsz, seq, D_MODEL), F32),
                   jax.ShapeDtypeStruct((bsz,) + sblock[1:], F32),
                   jax.ShapeDtypeStruct((bsz, GDN_CONV - 1, GDN_QKV), F32)],
        scratch_shapes=[pltpu.VMEM((bb, GDN_PAD + tl, GDN_QKV), F32)],
        compiler_params=_params(),
        name="gdn_layer",
    )(h_tm, s0, cb, w["g"], w["wqkv"], w["wz"], w["wab"], w["conv_w"], w["a_log"], w["dt_bias"],
      w["ng"], w["fg"], w["w_out"])


def _row(x, width=None):
    x = x.reshape(1, -1).astype(F32)
    if width is not None and x.shape[1] < width:
        x = jnp.pad(x, ((0, 0), (0, width - x.shape[1])))
    return x


def _pad_cols(w, width):
    return jnp.pad(w, ((0, 0), (0, width - w.shape[1])))


def _block_diag(x):
    nb, k, r, c = x.shape
    eye = jnp.eye(k, dtype=x.dtype)
    return jnp.einsum("jgrc,gh->jgrhc", x, eye).reshape(nb, k * r, k * c)


def _s5_weights(j, norm_g, s5_w_in, s5_b_re, s5_b_im, s5_c_re, s5_c_im, s5_d, s5_log_dt, s5_a_re,
                s5_a_im, s5_w_glu, s5_b_glu, s5_w_out):
    abr, abi, bbr, bbi = _s5_prep(s5_log_dt[j], s5_a_re[j], s5_a_im[j], s5_b_re[j], s5_b_im[j])
    shp = (S5_NB, S5_BUNDLE, S5_GROUP, S5_N)
    wb = jnp.concatenate([_block_diag(bbr.reshape(shp)), _block_diag(bbi.reshape(shp))], axis=-1)
    shp = (S5_NB, S5_BUNDLE, S5_N, S5_GROUP)
    wc = jnp.concatenate([_block_diag(s5_c_re[j].transpose(0, 2, 1).reshape(shp)),
                          _block_diag(-s5_c_im[j].transpose(0, 2, 1).reshape(shp))], axis=1)
    return dict(g=_row(norm_g), w_in=s5_w_in[j].astype(BF16), wb=wb.astype(BF16), wc=wc.astype(BF16),
                abr=_row(abr), abi=_row(abi), d=_row(s5_d[j]), w_glu=s5_w_glu[j].astype(BF16),
                b_glu=_row(s5_b_glu[j]), w_out=s5_w_out[j].astype(BF16))


def kernel(x_prompt, x_sample, state_pool, state_gla, state_s5_re, state_s5_im, state_gdn, state_gdn_conv, meta_tokens, norm_g, final_norm_g, pool_w_in, pool_w_grp, pool_scale, pool_w_out, gla_w_in, gla_w_gk, gla_b_gk, gla_norm_g, gla_w_out, s5_w_in, s5_b_re, s5_b_im, s5_c_re, s5_c_im, s5_d, s5_log_dt, s5_a_re, s5_a_im, s5_w_glu, s5_b_glu, s5_w_out, gdn_w_in, gdn_conv_w, gdn_a_log, gdn_dt_bias, gdn_norm_g, gdn_w_out):
    bp = x_prompt.shape[0]
    bs, ls, _ = x_sample.shape

    wp = dict(g=_row(norm_g[0]), w_in=pool_w_in[0].astype(BF16), w_grp=pool_w_grp[0].astype(BF16),
              scale=_row(pool_scale[0]), w_out=pool_w_out[0].astype(BF16))
    gw = gla_w_in[0]
    wg = dict(g=_row(norm_g[1]), wq=gw[:, :GLA_QK].astype(BF16), wk=gw[:, GLA_QK:2 * GLA_QK].astype(BF16),
              wv=gw[:, 2 * GLA_QK:2 * GLA_QK + GLA_V].astype(BF16),
              wgate=gw[:, 2 * GLA_QK + GLA_V:2 * GLA_QK + 2 * GLA_V].astype(BF16),
              wglow=_pad_cols(gw[:, 2 * GLA_QK + 2 * GLA_V:], LANES).astype(BF16),
              wgk=jnp.pad(gla_w_gk[0], ((0, LANES - GLA_RANK), (0, 0))).astype(BF16),
              bgk=_row(gla_b_gk[0]), ng=_row(gla_norm_g[0]), w_out=gla_w_out[0].astype(BF16))
    ws = _s5_weights(0, norm_g[2], s5_w_in, s5_b_re, s5_b_im, s5_c_re, s5_c_im, s5_d, s5_log_dt,
                     s5_a_re, s5_a_im, s5_w_glu, s5_b_glu, s5_w_out)
    dw = gdn_w_in[0]
    wd = dict(g=_row(norm_g[3]), wqkv=dw[:, :GDN_QKV].astype(BF16),
              wz=dw[:, GDN_QKV:GDN_QKV + GDN_V].astype(BF16),
              wab=_pad_cols(dw[:, GDN_QKV + GDN_V:], LANES).astype(BF16), conv_w=gdn_conv_w[0],
              a_log=_row(gdn_a_log[0], LANES), dt_bias=_row(gdn_dt_bias[0], LANES),
              ng=_row(gdn_norm_g[0]), fg=_row(final_norm_g), w_out=gdn_w_out[0].astype(BF16))

    def run(h, pool_st, n_valid, gla_st, s5_st, gdn_st, conv_st, blocks):
        (pb, pt), (gb, gt), (sb, stl), (db, dtl, dn) = blocks
        bsz, seq, _ = h.shape
        h, pool_new = _pool_layer(h, pool_st, n_valid, wp, pb, pt)
        h_tm, gla_new = _gla_layer(h, gla_st, wg, gb, gt)
        h_tb, s5_new = _s5_layer(h_tm.reshape(seq, bsz, D_MODEL), s5_st, ws, sb, stl)
        y, gdn_new, conv_new = _gdn_layer(h_tb.reshape(seq, bsz * D_MODEL), gdn_st, conv_st,
                                          wd, db, dtl, dn)
        return y, pool_new, gla_new, s5_new, gdn_new, conv_new

    hm = meta_tokens.astype(F32)[None]
    zeros = lambda *s: jnp.zeros(s, F32)
    hm, m_pool = _pool_layer(hm, zeros(1, POOL_BUF, D_MODEL), 0, wp, 1, N_META)
    hm, m_gla = _gla_layer(hm, zeros(1, GLA_HEADS, GLA_DK, GLA_DV), wg, 1, N_META)
    hm_tb, m_s5 = _s5_layer(jnp.broadcast_to(hm[:, None, :], (N_META, SUBLANES, D_MODEL)),
                            zeros(SUBLANES, S5_STATE), ws, SUBLANES, N_META)
    _, m_gdn, m_conv = _gdn_layer(hm_tb[:, 0], zeros(1, GDN_HEADS, GDN_DK, GDN_DV),
                                  zeros(1, GDN_CONV - 1, GDN_QKV), wd, 1, N_META, 1)

    yp, pool_p, gla_p, s5_p, gdn_p, conv_p = run(
        x_prompt, m_pool, N_META, m_gla, jnp.broadcast_to(m_s5[0:1], (bp, S5_STATE)), m_gdn, m_conv,
        ((1, 512), (4, 64), (bp, 32), (4, 64, 1)))
    ys, pool_s, gla_s, s5_s, gdn_s, conv_s = run(
        x_sample, state_pool[0], POOL_BUF, state_gla[0], _s5_pack_state(state_s5_re[0], state_s5_im[0]),
        state_gdn[0], state_gdn_conv[0], ((16, ls), (8, ls), (32, ls), (8, ls, 8)))

    s5r_p, s5i_p = _s5_unpack_state(s5_p)
    s5r_s, s5i_s = _s5_unpack_state(s5_s)
    return (yp, ys, pool_p[None], pool_s[None], gla_p[None], gla_s[None],
            s5r_p[None], s5i_p[None], s5r_s[None], s5i_s[None],
            gdn_p[None], conv_p[None], gdn_s[None], conv_s[None])
```

```python
import functools
import math

import jax
import jax.numpy as jnp
from jax import lax
from jax.experimental import pallas as pl
from jax.experimental.pallas import tpu as pltpu

F32 = jnp.float32
BF16 = jnp.bfloat16
HIGHEST = lax.Precision.HIGHEST

D_MODEL = 1024
EPS = 1e-6
N_META = 16

POOL_WINDOWS = (2, 4, 8, 16)
POOL_GROUP = D_MODEL // len(POOL_WINDOWS)
POOL_BUF = max(POOL_WINDOWS) - 1
POOL_PAD = POOL_BUF + 1

GLA_HEADS = 4
GLA_DK = 128
GLA_DV = 256
GLA_QK = GLA_HEADS * GLA_DK
GLA_V = GLA_HEADS * GLA_DV
GLA_RANK = 16
GLA_GATE_NORM = 16.0

S5_GROUP = 16
S5_G = D_MODEL // S5_GROUP
S5_N = 64
S5_BUNDLE = 8
S5_NB = S5_G // S5_BUNDLE
S5_HALF = S5_BUNDLE * S5_N
S5_STATE = 2 * S5_G * S5_N

GDN_HEADS = 8
GDN_DK = 128
GDN_DV = 128
GDN_CONV = 4
GDN_QK = GDN_HEADS * GDN_DK
GDN_V = GDN_HEADS * GDN_DV
GDN_QKV = 2 * GDN_QK + GDN_V
GDN_PAD = 8

LANES = 128
SUBLANES = 8
VMEM_LIMIT = 52 * 1024 * 1024

_NT = (((1,), (1,)), ((), ()))
_TN = (((0,), (0,)), ((), ()))


def _rms_rows(x, g):
    return x * lax.rsqrt(jnp.mean(x * x, axis=-1, keepdims=True) + EPS) * g


def _mm(a, w):
    return jnp.dot(a.astype(BF16), w, preferred_element_type=F32)


def _cast_small(c, *xs):
    if c % 16 == 0:
        return tuple(x.astype(BF16) for x in xs)
    return xs


def _bf(*xs):
    return tuple(x.astype(BF16) for x in xs)


def _block_masks(n, c):
    r = lax.broadcasted_iota(jnp.int32, (n, n), 0)
    s = lax.broadcasted_iota(jnp.int32, (n, n), 1)
    sh = int(math.log2(c))
    same = (r >> sh) == (s >> sh)
    return same & (r >= s), same & (r > s), r == s


def _exact_dot(x, dot_piece):
    hi = x.astype(BF16)
    r = x - hi.astype(F32)
    mid = r.astype(BF16)
    lo = (r - mid.astype(F32)).astype(BF16)
    return dot_piece(hi) + dot_piece(mid) + dot_piece(lo)


def _dot(a, b):
    return jnp.dot(a, b, preferred_element_type=F32)


def _dot_nt(a, b):
    return lax.dot_general(a, b, _NT, preferred_element_type=F32)


def _dot_tn(a, b):
    return lax.dot_general(a, b, _TN, preferred_element_type=F32)


def _pool_kernel(n_valid, h_ref, buf_ref, g_ref, win_ref, wgrp_ref, scale_ref, wout_ref,
                 o_ref, st_ref, z_ref):
    bb, tl, _ = h_ref.shape
    rows = bb * tl
    l = pl.program_id(1)
    seqs = range(bb)
    slabs_per_group = POOL_GROUP // LANES
    pitch = POOL_PAD + tl

    @pl.when(l == 0)
    def _():
        for b in seqs:
            buf = buf_ref[b if buf_ref.shape[0] == bb else 0]
            for s in range(D_MODEL // LANES):
                z_ref[s, b * pitch + POOL_PAD - POOL_BUF:b * pitch + POOL_PAD, :] = buf[:, s * LANES:(s + 1) * LANES]

    h = h_ref[...].reshape(rows, D_MODEL)
    xn = _rms_rows(h, g_ref[...])
    xn = xn.astype(BF16)
    ug = _mm(xn, win_ref[:, :D_MODEL])
    gate = _mm(xn, win_ref[:, D_MODEL:])
    for b in seqs:
        for s in range(D_MODEL // LANES):
            z_ref[s, b * pitch + POOL_PAD:(b + 1) * pitch, :] = ug[b * tl:(b + 1) * tl, s * LANES:(s + 1) * LANES]

    t = (l * tl + lax.broadcasted_iota(jnp.int32, (tl, LANES), 0)).astype(F32)
    parts = []
    for gi, w in enumerate(POOL_WINDOWS):
        cnt = jnp.minimum(float(w), t + (1.0 + n_valid))
        mixed_rows = []
        for b in seqs:
            cols = []
            for s in range(gi * slabs_per_group, (gi + 1) * slabs_per_group):
                cur = ug[b * tl:(b + 1) * tl, s * LANES:(s + 1) * LANES]
                acc = cur
                for j in range(1, w):
                    acc = acc + z_ref[s, pl.ds(b * pitch + POOL_PAD - j, tl, stride=1), :]
                cols.append(acc / cnt - cur)
            mixed_rows.append(jnp.concatenate(cols, axis=1))
        parts.append(_mm(jnp.concatenate(mixed_rows, axis=0), wgrp_ref[gi]))
    mixed = jnp.concatenate(parts, axis=-1) * scale_ref[...]
    y = _mm(mixed * jax.nn.silu(gate), wout_ref[...])
    o_ref[...] = (h + y).reshape(bb, tl, D_MODEL)

    @pl.when(l == pl.num_programs(1) - 1)
    def _():
        for b in seqs:
            st_ref[b] = jnp.concatenate(
                [z_ref[s, b * pitch + tl + 1:(b + 1) * pitch, :] for s in range(D_MODEL // LANES)], axis=1)

    for b in seqs:
        for s in range(D_MODEL // LANES):
            z_ref[s, b * pitch:b * pitch + POOL_PAD, :] = z_ref[s, b * pitch + tl:(b + 1) * pitch, :]


def _const_spec(shape):
    nd = len(shape)
    return pl.BlockSpec(shape, lambda b, l: (0,) * nd)


def _state_spec(block, shared):
    nd = len(block)
    if shared:
        return pl.BlockSpec(block, lambda b, l: (0,) * nd)
    return pl.BlockSpec(block, lambda b, l: (b,) + (0,) * (nd - 1))


def _params():
    return pltpu.CompilerParams(dimension_semantics=("parallel", "arbitrary"),
                                vmem_limit_bytes=VMEM_LIMIT)


def _pool_layer(h, buf, n_valid, w, bb, tl):
    bsz, seq, _ = h.shape
    shared = buf.shape[0] != bsz
    hspec = pl.BlockSpec((bb, tl, D_MODEL), lambda b, l: (b, l, 0))
    return pl.pallas_call(
        functools.partial(_pool_kernel, float(n_valid)),
        grid=(bsz // bb, seq // tl),
        in_specs=[hspec, _state_spec((1 if shared else bb, POOL_BUF, D_MODEL), shared),
                  _const_spec((1, D_MODEL)), _const_spec((D_MODEL, 2 * D_MODEL)),
                  _const_spec((len(POOL_WINDOWS), POOL_GROUP, POOL_GROUP)),
                  _const_spec((1, D_MODEL)), _const_spec((D_MODEL, D_MODEL))],
        out_specs=[hspec, pl.BlockSpec((bb, POOL_BUF, D_MODEL), lambda b, l: (b, 0, 0))],
        out_shape=[jax.ShapeDtypeStruct(h.shape, F32),
                   jax.ShapeDtypeStruct((bsz, POOL_BUF, D_MODEL), F32)],
        scratch_shapes=[pltpu.VMEM((D_MODEL // LANES, bb * (POOL_PAD + tl), LANES), F32)],
        compiler_params=_params(),
        name="pool_layer",
    )(h, buf, w["g"], w["w_in"], w["w_grp"], w["scale"], w["w_out"])


def _gla_kernel(h_ref, s0_ref, g_ref, wq_ref, wk_ref, wv_ref, wgate_ref, wglow_ref, wgk_ref,
                bgk_ref, ng_ref, wout_ref, o_ref, s_ref):
    bb, tl, _ = h_ref.shape
    rows = bb * tl

    @pl.when(pl.program_id(1) == 0)
    def _():
        s_ref[...] = jnp.broadcast_to(s0_ref[...], s_ref.shape)

    h = h_ref[...].reshape(rows, D_MODEL)
    xn = _rms_rows(h, g_ref[...]).astype(BF16)
    glow = _mm(xn, wglow_ref[...])
    gk = jax.nn.log_sigmoid(_mm(glow, wgk_ref[...]) + bgk_ref[...]) / GLA_GATE_NORM
    q = _mm(xn, wq_ref[...]) * GLA_DK ** -0.5
    k = _mm(xn, wk_ref[...])
    v = _mm(xn, wv_ref[...])
    gate = _mm(xn, wgate_ref[...])

    incl, _, _ = _block_masks(rows, tl)
    tri = incl.astype(F32).astype(BF16)
    eye_k = (lax.broadcasted_iota(jnp.int32, (GLA_DK, GLA_DK), 0)
             == lax.broadcasted_iota(jnp.int32, (GLA_DK, GLA_DK), 1))
    heads = range(GLA_HEADS)
    seqs = range(bb)
    ksl = [slice(hd * GLA_DK, (hd + 1) * GLA_DK) for hd in heads]
    vsl = [slice(hd * GLA_DV, (hd + 1) * GLA_DV) for hd in heads]
    blks = [slice(b * tl, (b + 1) * tl) for b in seqs]

    bc = _exact_dot(gk, lambda p: _dot(tri, p))
    qg = q * jnp.exp(bc)
    kg = k * jnp.exp(-bc)
    kd, ebl = [], []
    for b in seqs:
        bl = bc[(b + 1) * tl - 1:(b + 1) * tl, :]
        kd.append(k[blks[b]] * jnp.exp(bl - bc[blks[b]]))
        ebl.append([jnp.exp(jnp.sum(jnp.where(eye_k, jnp.broadcast_to(bl[:, ksl[hd]], (GLA_DK, GLA_DK)), 0.0),
                                    axis=1, keepdims=True)) for hd in heads])
    qg_b, kg_b, v_b = _bf(qg, kg, v)
    att = [jnp.where(incl, _dot_nt(qg_b[:, ksl[hd]], kg_b[:, ksl[hd]]), 0.0) for hd in heads]
    o = [_dot(att[hd].astype(BF16), v_b[:, vsl[hd]]) for hd in heads]
    st = [[s_ref[b, hd] for hd in heads] for b in seqs]
    o_st = [[None] * GLA_HEADS for _ in seqs]
    for b in seqs:
        for hd in heads:
            qi, si = _cast_small(tl, qg[blks[b], ksl[hd]], st[b][hd])
            o_st[b][hd] = _dot(qi, si)
    for b in seqs:
        for hd in heads:
            kdi, vi = _cast_small(tl, kd[b][:, ksl[hd]], v[blks[b], vsl[hd]])
            s_ref[b, hd] = ebl[b][hd] * st[b][hd] + _dot_tn(kdi, vi)
    o_all = jnp.concatenate(
        [o[hd] + jnp.concatenate([o_st[b][hd] for b in seqs], axis=0) for hd in heads], axis=1)

    parts = []
    for hd in range(GLA_HEADS):
        parts.append(_rms_rows(o_all[:, hd * GLA_DV:(hd + 1) * GLA_DV], ng_ref[...]))
    on = jnp.concatenate(parts, axis=-1)
    out = h + _mm(on * jax.nn.silu(gate), wout_ref[...])
    for i in range(bb):
        o_ref[:, i * D_MODEL:(i + 1) * D_MODEL] = out[i * tl:(i + 1) * tl]


def _gla_layer(h, s0, w, bb, tl):
    bsz, seq, _ = h.shape
    shared = s0.shape[0] != bsz
    sblock = (bb, GLA_HEADS, GLA_DK, GLA_DV)
    hspec = pl.BlockSpec((bb, tl, D_MODEL), lambda b, l: (b, l, 0))
    return pl.pallas_call(
        _gla_kernel,
        grid=(bsz // bb, seq // tl),
        in_specs=[hspec, _state_spec((1,) + sblock[1:] if shared else sblock, shared),
                  _const_spec((1, D_MODEL)),
                  _const_spec((D_MODEL, GLA_QK)), _const_spec((D_MODEL, GLA_QK)),
                  _const_spec((D_MODEL, GLA_V)), _const_spec((D_MODEL, GLA_V)),
                  _const_spec((D_MODEL, LANES)), _const_spec((LANES, GLA_QK)),
                  _const_spec((1, GLA_QK)), _const_spec((1, GLA_DV)),
                  _const_spec((GLA_V, D_MODEL))],
        out_specs=[pl.BlockSpec((tl, bb * D_MODEL), lambda b, l: (l, b)),
                   pl.BlockSpec(sblock, lambda b, l: (b, 0, 0, 0))],
        out_shape=[jax.ShapeDtypeStruct((seq, bsz * D_MODEL), F32),
                   jax.ShapeDtypeStruct((bsz,) + sblock[1:], F32)],
        compiler_params=_params(),
        name="gla_layer",
    )(h, s0, w["g"], w["wq"], w["wk"], w["wv"], w["wgate"], w["wglow"], w["wgk"], w["bgk"],
      w["ng"], w["w_out"])


def _s5_prep_kernel(logdt_ref, are_ref, aim_ref, bre_ref, bim_ref, abr_ref, abi_ref, bbr_ref, bbi_ref):
    dt = jnp.exp(logdt_ref[...])
    lr = are_ref[...]
    li = aim_ref[...]
    mag = jnp.exp(lr * dt)
    abr = mag * jnp.cos(li * dt)
    abi = mag * jnp.sin(li * dt)
    den = lr * lr + li * li
    cr = ((abr - 1.0) * lr + abi * li) / den
    ci = (abi * lr - (abr - 1.0) * li) / den
    abr_ref[...] = abr
    abi_ref[...] = abi
    br = bre_ref[...]
    bi = bim_ref[...]
    bbr_ref[...] = cr[:, None, :] * br - ci[:, None, :] * bi
    bbi_ref[...] = cr[:, None, :] * bi + ci[:, None, :] * br


def _s5_prep(log_dt, a_re, a_im, b_re, b_im):
    gn = jax.ShapeDtypeStruct((S5_G, S5_N), F32)
    gcn = jax.ShapeDtypeStruct((S5_G, S5_GROUP, S5_N), F32)
    return pl.pallas_call(_s5_prep_kernel, out_shape=[gn, gn, gcn, gcn], name="s5_discretize")(
        log_dt.reshape(S5_G, 1), a_re, a_im, b_re.transpose(0, 2, 1), b_im.transpose(0, 2, 1))


def _s5_kernel(h_ref, h0_ref, g_ref, win_ref, wb_ref, wc_ref, abr_ref, abi_ref, dskip_ref,
               wglu_ref, bglu_ref, wout_ref, o_ref, hs_ref):
    tl, nb, _ = h_ref.shape
    rows = tl * nb
    ngrp = nb // SUBLANES

    @pl.when(pl.program_id(1) == 0)
    def _():
        hs_ref[...] = h0_ref[...]

    h = h_ref[...].reshape(rows, D_MODEL)
    xn = _rms_rows(h, g_ref[...]).astype(BF16)
    ug = _mm(xn, win_ref[...])
    u = ug[:, :D_MODEL]
    gate = ug[:, D_MODEL:]
    ub = u.astype(BF16)
    width = 2 * S5_HALF
    in_w = S5_BUNDLE * S5_GROUP

    def project_in(j):
        return jnp.dot(ub[:, j * in_w:(j + 1) * in_w], wb_ref[j], preferred_element_type=F32)

    parts = []
    bu_next = project_in(0)
    for j in range(S5_NB):
        bu = bu_next
        if j + 1 < S5_NB:
            bu_next = project_in(j + 1)
        re = slice(j * width, j * width + S5_HALF)
        im = slice(j * width + S5_HALF, (j + 1) * width)
        ar = jnp.broadcast_to(abr_ref[:, j * S5_HALF:(j + 1) * S5_HALF], (SUBLANES, S5_HALF))
        ai = jnp.broadcast_to(abi_ref[:, j * S5_HALF:(j + 1) * S5_HALF], (SUBLANES, S5_HALF))
        tiles = [None] * (tl * ngrp)
        for bg in range(ngrp):
            grp = slice(bg * SUBLANES, (bg + 1) * SUBLANES)
            hr = hs_ref[grp, re]
            hi = hs_ref[grp, im]
            for t in range(tl):
                r = t * nb + bg * SUBLANES
                hr, hi = (ar * hr - ai * hi + bu[r:r + SUBLANES, :S5_HALF],
                          ar * hi + ai * hr + bu[r:r + SUBLANES, S5_HALF:])
                tiles[t * ngrp + bg] = jnp.concatenate([hr, hi], axis=1)
            hs_ref[grp, re] = hr
            hs_ref[grp, im] = hi
        parts.append(_mm(jnp.concatenate(tiles, axis=0), wc_ref[j]))
    y = jnp.concatenate(parts, axis=-1) + dskip_ref[...] * u
    z = jax.nn.gelu(y)
    z = z * jax.nn.sigmoid(_mm(z, wglu_ref[...]) + bglu_ref[...])
    out = h + _mm(z * jax.nn.silu(gate), wout_ref[...])
    o_ref[...] = out.reshape(tl, nb, D_MODEL)


def _s5_layer(h_tb, h0, w, nb, tl):
    seq, bsz, _ = h_tb.shape
    hspec = pl.BlockSpec((tl, nb, D_MODEL), lambda b, l: (l, b, 0))
    sspec = pl.BlockSpec((nb, S5_STATE), lambda b, l: (b, 0))
    in_w = S5_BUNDLE * S5_GROUP
    return pl.pallas_call(
        _s5_kernel,
        grid=(bsz // nb, seq // tl),
        in_specs=[hspec, sspec, _const_spec((1, D_MODEL)), _const_spec((D_MODEL, 2 * D_MODEL)),
                  _const_spec((S5_NB, in_w, 2 * S5_HALF)), _const_spec((S5_NB, 2 * S5_HALF, in_w)),
                  _const_spec((1, S5_G * S5_N)), _const_spec((1, S5_G * S5_N)),
                  _const_spec((1, D_MODEL)), _const_spec((D_MODEL, D_MODEL)),
                  _const_spec((1, D_MODEL)), _const_spec((D_MODEL, D_MODEL))],
        out_specs=[hspec, sspec],
        out_shape=[jax.ShapeDtypeStruct(h_tb.shape, F32), jax.ShapeDtypeStruct((bsz, S5_STATE), F32)],
        compiler_params=_params(),
        name="s5_layer",
    )(h_tb, h0, w["g"], w["w_in"], w["wb"], w["wc"], w["abr"], w["abi"], w["d"], w["w_glu"],
      w["b_glu"], w["w_out"])


def _s5_pack_state(re, im):
    b = re.shape[0]
    return jnp.concatenate([re.reshape(b, S5_NB, S5_HALF), im.reshape(b, S5_NB, S5_HALF)],
                           axis=-1).reshape(b, S5_STATE)


def _s5_unpack_state(st):
    b = st.shape[0]
    st = st.reshape(b, S5_NB, 2 * S5_HALF)
    return (st[..., :S5_HALF].reshape(b, S5_G, S5_N), st[..., S5_HALF:].reshape(b, S5_G, S5_N))


def _gdn_kernel(nbu, h_ref, s0_ref, cb_ref, g_ref, wqkv_ref, wz_ref, wab_ref, convw_ref, alog_ref,
                dtb_ref, ng_ref, fg_ref, wout_ref, o_ref, s_ref, cbout_ref, ext_ref):
    bb, tl, _ = o_ref.shape
    rows = bb * tl
    l = pl.program_id(1)

    heads = range(GDN_HEADS)
    seqs = range(bb)
    slabs = range(GDN_QKV // LANES)
    pitch = GDN_PAD + tl
    hist = GDN_PAD - (GDN_CONV - 1)

    @pl.when(l == 0)
    def _():
        s_ref[...] = jnp.broadcast_to(s0_ref[...], s_ref.shape)
        for b in seqs:
            cb = cb_ref[b if cb_ref.shape[0] == bb else 0]
            for s in slabs:
                ext_ref[s, b * pitch + hist:b * pitch + GDN_PAD, :] = cb[:, s * LANES:(s + 1) * LANES]

    h = jnp.concatenate([h_ref[:, i * D_MODEL:(i + 1) * D_MODEL] for i in range(bb)], axis=0)
    xn = _rms_rows(h, g_ref[...]).astype(BF16)
    ab = _mm(xn, wab_ref[...])
    qkv = _mm(xn, wqkv_ref[...])
    z = _mm(xn, wz_ref[...])
    for b in seqs:
        for s in slabs:
            ext_ref[s, b * pitch + GDN_PAD:(b + 1) * pitch, :] = qkv[b * tl:(b + 1) * tl, s * LANES:(s + 1) * LANES]
    conv_rows = []
    for b in seqs:
        cols = []
        for s in slabs:
            lanes = slice(s * LANES, (s + 1) * LANES)
            acc = convw_ref[GDN_CONV - 1:GDN_CONV, lanes] * qkv[b * tl:(b + 1) * tl, lanes]
            for j in range(GDN_CONV - 1):
                acc = acc + convw_ref[j:j + 1, lanes] * ext_ref[s, pl.ds(b * pitch + hist + j, tl, stride=1), :]
            cols.append(acc)
        conv_rows.append(jnp.concatenate(cols, axis=1))
    act = jax.nn.silu(jnp.concatenate(conv_rows, axis=0))
    hsl = [slice(hd * GDN_DK, (hd + 1) * GDN_DK) for hd in heads]
    q_parts, k_parts = [], []
    for hd in heads:
        qh = act[:, hd * GDN_DK:(hd + 1) * GDN_DK]
        kh = act[:, GDN_QK + hd * GDN_DK:GDN_QK + (hd + 1) * GDN_DK]
        q_parts.append(qh * lax.rsqrt(jnp.sum(qh * qh, axis=-1, keepdims=True) + EPS) * GDN_DK ** -0.5)
        k_parts.append(kh * lax.rsqrt(jnp.sum(kh * kh, axis=-1, keepdims=True) + EPS))
    q = jnp.concatenate(q_parts, axis=1)
    k = jnp.concatenate(k_parts, axis=1)
    v = act[:, 2 * GDN_QK:]
    g = -jnp.exp(alog_ref[...]) * jax.nn.softplus(ab + dtb_ref[...])
    beta = jax.nn.sigmoid(ab)

    group = 4
    sup = nbu * tl
    width = group * sup
    n_units = bb // nbu
    n_sq = int(math.log2(tl)) - 1
    lg_sup = int(math.log2(sup))
    lg_tl = int(math.log2(tl))
    assert bb % nbu == 0 and 1 << lg_sup == sup and 1 << lg_tl == tl and n_sq >= 1
    tri = _block_masks(rows, tl)[0].astype(F32).astype(BF16)
    t4 = lax.broadcasted_iota(jnp.int32, (sup, width), 0)
    col4 = lax.broadcasted_iota(jnp.int32, (sup, width), 1)
    s4 = col4 & (sup - 1)
    same4 = (t4 >> lg_tl) == (s4 >> lg_tl)
    incl4 = same4 & (t4 >= s4)
    strict4 = same4 & (t4 > s4)
    eye4 = (t4 == s4).astype(F32)
    bd_mask = ((lax.broadcasted_iota(jnp.int32, (width, width), 0) >> lg_sup)
               == (lax.broadcasted_iota(jnp.int32, (width, width), 1) >> lg_sup))
    head4 = col4 >> lg_sup

    def per_head_lanes(x, first):
        return jnp.concatenate(
            [jnp.broadcast_to(x[:, first + hd:first + hd + 1], (rows, GDN_DK)) for hd in heads], axis=1)

    gcum = _exact_dot(g, lambda p: _dot(tri, p))
    gcx = per_head_lanes(gcum, 0)
    bx = per_head_lanes(beta, GDN_HEADS)
    egx = jnp.exp(gcx)
    kb = k * bx
    qe = q * egx
    q_b, k_b, kb_b = _bf(q, k, kb)
    rhs_v = v * bx
    rhs_k = kb * egx
    kend, egl = [], []
    for b in seqs:
        gl = gcx[(b + 1) * tl - 1:(b + 1) * tl, :]
        kend.append(k[b * tl:(b + 1) * tl] * jnp.exp(gl - gcx[b * tl:(b + 1) * tl]))
        egl.append(jnp.exp(gl))

    keys = [(u, hg) for u in range(n_units) for hg in range(GDN_HEADS // group)]
    m4, att4 = {}, {}
    for u, hg in keys:
        ru = slice(u * sup, (u + 1) * sup)
        lanes4 = slice(hg * group * GDN_DK, (hg + 1) * group * GDN_DK)
        kdiag = []
        for hh in range(group):
            pieces = [jnp.zeros((sup, GDN_DK), BF16)] * group
            pieces[hh] = k_b[ru, hsl[hg * group + hh]]
            kdiag.append(jnp.concatenate(pieces, axis=1))
        kdiag = jnp.concatenate(kdiag, axis=0)
        kk = _dot_nt(kb_b[ru, lanes4], kdiag)
        qk = _dot_nt(q_b[ru, lanes4], kdiag)
        gcol = None
        for hh in range(group):
            rep = gcx[ru, hsl[hg * group + hh]]
            rep = rep[:, :width] if width <= GDN_DK else jnp.concatenate([rep] * (width // GDN_DK), axis=1)
            gcol = rep if gcol is None else jnp.where(head4 == hh, rep, gcol)
        grow = jnp.sum(eye4 * gcol, axis=0, keepdims=True)
        decay = jnp.exp(gcol - grow)
        m4[u, hg] = -jnp.where(strict4, kk * decay, 0.0)
        att4[u, hg] = jnp.where(incl4, qk * decay, 0.0)

    def block_diag(m_b):
        return jnp.where(bd_mask, jnp.concatenate([m_b] * group, axis=0), jnp.zeros((), BF16))

    p4 = {key: eye4 + m4[key] for key in keys}
    m_b = {key: m4[key].astype(BF16) for key in keys}
    m4 = {key: _dot(m_b[key], block_diag(m_b[key])) for key in keys}
    for _ in range(1, n_sq):
        m_b = {key: m4[key].astype(BF16) for key in keys}
        x = {key: _dot(jnp.concatenate([p4[key].astype(BF16), m_b[key]], axis=0), block_diag(m_b[key]))
             for key in keys}
        p4 = {key: p4[key] + x[key][:sup] for key in keys}
        m4 = {key: x[key][sup:] for key in keys}
    m_b = {key: m4[key].astype(BF16) for key in keys}
    p4 = {key: p4[key] + _dot(p4[key].astype(BF16), block_diag(m_b[key])) for key in keys}

    uw = [[None] * GDN_HEADS for _ in range(n_units)]
    for u, hg in keys:
        ru = slice(u * sup, (u + 1) * sup)
        p_b = p4[u, hg].astype(BF16)
        for hh in range(group):
            hd = hg * group + hh
            rhs = jnp.concatenate([rhs_v[ru, hsl[hd]], rhs_k[ru, hsl[hd]]], axis=1)
            uw[u][hd] = _dot(p_b[:, hh * sup:(hh + 1) * sup], rhs.astype(BF16))

    st = [[s_ref[b, hd] for hd in heads] for b in seqs]
    v_new = [[None] * GDN_HEADS for _ in seqs]
    qs = [[None] * GDN_HEADS for _ in seqs]
    for b in seqs:
        u, i = divmod(b, nbu)
        for hd in heads:
            wq = jnp.concatenate([uw[u][hd][i * tl:(i + 1) * tl, GDN_DV:], qe[b * tl:(b + 1) * tl, hsl[hd]]],
                                 axis=0)
            ws = _dot(wq.astype(BF16), st[b][hd].astype(BF16))
            v_new[b][hd] = uw[u][hd][i * tl:(i + 1) * tl, :GDN_DV] - ws[:tl]
            qs[b][hd] = ws[tl:]
    o_units = []
    for u in range(n_units):
        o_heads = []
        for hd in heads:
            hg, hh = divmod(hd, group)
            members = range(u * nbu, (u + 1) * nbu)
            vn = jnp.concatenate([v_new[b][hd] for b in members], axis=0)
            att = att4[u, hg][:, hh * sup:(hh + 1) * sup]
            o_heads.append(jnp.concatenate([qs[b][hd] for b in members], axis=0)
                           + _dot(att.astype(BF16), vn.astype(BF16)))
        o_units.append(jnp.concatenate(o_heads, axis=1))
    for b in seqs:
        for hd in heads:
            ke, vn = _cast_small(tl, kend[b][:, hsl[hd]], v_new[b][hd])
            s_ref[b, hd] = egl[b][:, hsl[hd]] * st[b][hd] + _dot_tn(ke, vn)
    o_all = jnp.concatenate(o_units, axis=0)

    parts = []
    for hd in range(GDN_HEADS):
        parts.append(_rms_rows(o_all[:, hd * GDN_DV:(hd + 1) * GDN_DV], ng_ref[...]))
    on = jnp.concatenate(parts, axis=-1)
    out = h + _mm(on * jax.nn.silu(z), wout_ref[...])
    o_ref[...] = _rms_rows(out, fg_ref[...]).reshape(bb, tl, D_MODEL)

    @pl.when(l == pl.num_programs(1) - 1)
    def _():
        for b in seqs:
            cbout_ref[b] = jnp.concatenate(
                [ext_ref[s, b * pitch + tl + hist:(b + 1) * pitch, :] for s in slabs], axis=1)

    for b in seqs:
        for s in slabs:
            ext_ref[s, b * pitch:b * pitch + GDN_PAD, :] = ext_ref[s, b * pitch + tl:(b + 1) * pitch, :]


def _gdn_layer(h_tm, s0, cb, w, bb, tl, nbu):
    seq = h_tm.shape[0]
    bsz = h_tm.shape[1] // D_MODEL
    shared = s0.shape[0] != bsz
    sblock = (bb, GDN_HEADS, GDN_DK, GDN_DV)
    cblock = (bb, GDN_CONV - 1, GDN_QKV)
    hspec = pl.BlockSpec((bb, tl, D_MODEL), lambda b, l: (b, l, 0))
    return pl.pallas_call(
        functools.partial(_gdn_kernel, nbu),
        grid=(bsz // bb, seq // tl),
        in_specs=[pl.BlockSpec((tl, bb * D_MODEL), lambda b, l: (l, b)),
                  _state_spec((1,) + sblock[1:] if shared else sblock, shared),
                  _state_spec((1,) + cblock[1:] if shared else cblock, shared),
                  _const_spec((1, D_MODEL)), _const_spec((D_MODEL, GDN_QKV)),
                  _const_spec((D_MODEL, GDN_V)), _const_spec((D_MODEL, LANES)),
                  _const_spec((GDN_CONV, GDN_QKV)), _const_spec((1, LANES)), _const_spec((1, LANES)),
                  _const_spec((1, GDN_DV)), _const_spec((1, D_MODEL)), _const_spec((GDN_V, D_MODEL))],
        out_specs=[hspec, pl.BlockSpec(sblock, lambda b, l: (b, 0, 0, 0)),
                   pl.BlockSpec((bb, GDN_CONV - 1, GDN_QKV), lambda b, l: (b, 0, 0))],
        out_shape=[jax.ShapeDtypeStruct((bsz, seq, D_MODEL), F32),
                   jax.ShapeDtypeStruct((bsz,) + sblock[1:], F32),
                   jax.ShapeDtypeStruct((bsz, GDN_CONV - 1, GDN_QKV), F32)],
        scratch_shapes=[pltpu.VMEM((GDN_QKV // LANES, bb * (GDN_PAD + tl), LANES), F32)],
        compiler_params=_params(),
        name="gdn_layer",
    )(h_tm, s0, cb, w["g"], w["wqkv"], w["wz"], w["wab"], w["conv_w"], w["a_log"], w["dt_bias"],
      w["ng"], w["fg"], w["w_out"])


def _row(x, width=None):
    x = x.reshape(1, -1).astype(F32)
    if width is not None and x.shape[1] < width:
        x = jnp.pad(x, ((0, 0), (0, width - x.shape[1])))
    return x


def _pad_cols(w, width):
    return jnp.pad(w, ((0, 0), (0, width - w.shape[1])))


def _block_diag(x):
    nb, k, r, c = x.shape
    eye = jnp.eye(k, dtype=x.dtype)
    return jnp.einsum("jgrc,gh->jgrhc", x, eye).reshape(nb, k * r, k * c)


def _s5_weights(j, norm_g, s5_w_in, s5_b_re, s5_b_im, s5_c_re, s5_c_im, s5_d, s5_log_dt, s5_a_re,
                s5_a_im, s5_w_glu, s5_b_glu, s5_w_out):
    abr, abi, bbr, bbi = _s5_prep(s5_log_dt[j], s5_a_re[j], s5_a_im[j], s5_b_re[j], s5_b_im[j])
    shp = (S5_NB, S5_BUNDLE, S5_GROUP, S5_N)
    wb = jnp.concatenate([_block_diag(bbr.reshape(shp)), _block_diag(bbi.reshape(shp))], axis=-1)
    shp = (S5_NB, S5_BUNDLE, S5_N, S5_GROUP)
    wc = jnp.concatenate([_block_diag(s5_c_re[j].transpose(0, 2, 1).reshape(shp)),
                          _block_diag(-s5_c_im[j].transpose(0, 2, 1).reshape(shp))], axis=1)
    return dict(g=_row(norm_g), w_in=s5_w_in[j].astype(BF16), wb=wb.astype(BF16), wc=wc.astype(BF16),
                abr=_row(abr), abi=_row(abi), d=_row(s5_d[j]), w_glu=s5_w_glu[j].astype(BF16),
                b_glu=_row(s5_b_glu[j]), w_out=s5_w_out[j].astype(BF16))


def kernel(x_prompt, x_sample, state_pool, state_gla, state_s5_re, state_s5_im, state_gdn, state_gdn_conv, meta_tokens, norm_g, final_norm_g, pool_w_in, pool_w_grp, pool_scale, pool_w_out, gla_w_in, gla_w_gk, gla_b_gk, gla_norm_g, gla_w_out, s5_w_in, s5_b_re, s5_b_im, s5_c_re, s5_c_im, s5_d, s5_log_dt, s5_a_re, s5_a_im, s5_w_glu, s5_b_glu, s5_w_out, gdn_w_in, gdn_conv_w, gdn_a_log, gdn_dt_bias, gdn_norm_g, gdn_w_out):
    bp = x_prompt.shape[0]
    bs, ls, _ = x_sample.shape

    wp = dict(g=_row(norm_g[0]), w_in=pool_w_in[0].astype(BF16), w_grp=pool_w_grp[0].astype(BF16),
              scale=_row(pool_scale[0]), w_out=pool_w_out[0].astype(BF16))
    gw = gla_w_in[0]
    wg = dict(g=_row(norm_g[1]), wq=gw[:, :GLA_QK].astype(BF16), wk=gw[:, GLA_QK:2 * GLA_QK].astype(BF16),
              wv=gw[:, 2 * GLA_QK:2 * GLA_QK + GLA_V].astype(BF16),
              wgate=gw[:, 2 * GLA_QK + GLA_V:2 * GLA_QK + 2 * GLA_V].astype(BF16),
              wglow=_pad_cols(gw[:, 2 * GLA_QK + 2 * GLA_V:], LANES).astype(BF16),
              wgk=jnp.pad(gla_w_gk[0], ((0, LANES - GLA_RANK), (0, 0))).astype(BF16),
              bgk=_row(gla_b_gk[0]), ng=_row(gla_norm_g[0]), w_out=gla_w_out[0].astype(BF16))
    ws = _s5_weights(0, norm_g[2], s5_w_in, s5_b_re, s5_b_im, s5_c_re, s5_c_im, s5_d, s5_log_dt,
                     s5_a_re, s5_a_im, s5_w_glu, s5_b_glu, s5_w_out)
    dw = gdn_w_in[0]
    wd = dict(g=_row(norm_g[3]), wqkv=dw[:, :GDN_QKV].astype(BF16),
              wz=dw[:, GDN_QKV:GDN_QKV + GDN_V].astype(BF16),
              wab=_pad_cols(dw[:, GDN_QKV + GDN_V:], LANES).astype(BF16), conv_w=gdn_conv_w[0],
              a_log=_row(gdn_a_log[0], LANES), dt_bias=_row(gdn_dt_bias[0], LANES),
              ng=_row(gdn_norm_g[0]), fg=_row(final_norm_g), w_out=gdn_w_out[0].astype(BF16))

    def run(h, pool_st, n_valid, gla_st, s5_st, gdn_st, conv_st, blocks):
        (pb, pt), (gb, gt), (sb, stl), (db, dtl, dn) = blocks
        bsz, seq, _ = h.shape
        h, pool_new = _pool_layer(h, pool_st, n_valid, wp, pb, pt)
        h_tm, gla_new = _gla_layer(h, gla_st, wg, gb, gt)
        h_tb, s5_new = _s5_layer(h_tm.reshape(seq, bsz, D_MODEL), s5_st, ws, sb, stl)
        y, gdn_new, conv_new = _gdn_layer(h_tb.reshape(seq, bsz * D_MODEL), gdn_st, conv_st,
                                          wd, db, dtl, dn)
        return y, pool_new, gla_new, s5_new, gdn_new, conv_new

    hm = meta_tokens.astype(F32)[None]
    zeros = lambda *s: jnp.zeros(s, F32)
    hm, m_pool = _pool_layer(hm, zeros(1, POOL_BUF, D_MODEL), 0, wp, 1, N_META)
    hm, m_gla = _gla_layer(hm, zeros(1, GLA_HEADS, GLA_DK, GLA_DV), wg, 1, N_META)
    hm_tb, m_s5 = _s5_layer(jnp.broadcast_to(hm[:, None, :], (N_META, SUBLANES, D_MODEL)),
                            zeros(SUBLANES, S5_STATE), ws, SUBLANES, N_META)
    _, m_gdn, m_conv = _gdn_layer(hm_tb[:, 0], zeros(1, GDN_HEADS, GDN_DK, GDN_DV),
                                  zeros(1, GDN_CONV - 1, GDN_QKV), wd, 1, N_META, 1)

    yp, pool_p, gla_p, s5_p, gdn_p, conv_p = run(
        x_prompt, m_pool, N_META, m_gla, jnp.broadcast_to(m_s5[0:1], (bp, S5_STATE)), m_gdn, m_conv,
        ((1, 512), (4, 64), (bp, 64), (4, 64, 1)))
    ys, pool_s, gla_s, s5_s, gdn_s, conv_s = run(
        x_sample, state_pool[0], POOL_BUF, state_gla[0], _s5_pack_state(state_s5_re[0], state_s5_im[0]),
        state_gdn[0], state_gdn_conv[0], ((16, ls), (8, ls), (32, ls), (8, ls, 8)))

    s5r_p, s5i_p = _s5_unpack_state(s5_p)
    s5r_s, s5i_s = _s5_unpack_state(s5_s)
    return (yp, ys, pool_p[None], pool_s[None], gla_p[None], gla_s[None],
            s5r_p[None], s5i_p[None], s5r_s[None], s5i_s[None],
            gdn_p[None], conv_p[None], gdn_s[None], conv_s[None])
```

```python
import functools
import math

import jax
import jax.numpy as jnp
from jax import lax
from jax.experimental import pallas as pl
from jax.experimental.pallas import tpu as pltpu

F32 = jnp.float32
BF16 = jnp.bfloat16
HIGHEST = lax.Precision.HIGHEST

D_MODEL = 1024
EPS = 1e-6
N_META = 16

POOL_WINDOWS = (2, 4, 8, 16)
POOL_GROUP = D_MODEL // len(POOL_WINDOWS)
POOL_BUF = max(POOL_WINDOWS) - 1
POOL_PAD = POOL_BUF + 1

GLA_HEADS = 4
GLA_DK = 128
GLA_DV = 256
GLA_QK = GLA_HEADS * GLA_DK
GLA_V = GLA_HEADS * GLA_DV
GLA_RANK = 16
GLA_GATE_NORM = 16.0

S5_GROUP = 16
S5_G = D_MODEL // S5_GROUP
S5_N = 64
S5_BUNDLE = 8
S5_NB = S5_G // S5_BUNDLE
S5_HALF = S5_BUNDLE * S5_N
S5_STATE = 2 * S5_G * S5_N

GDN_HEADS = 8
GDN_DK = 128
GDN_DV = 128
GDN_CONV = 4
GDN_QK = GDN_HEADS * GDN_DK
GDN_V = GDN_HEADS * GDN_DV
GDN_QKV = 2 * GDN_QK + GDN_V
GDN_PAD = 8

LANES = 128
SUBLANES = 8
MXU_ROWS = 256
VMEM_LIMIT = 52 * 1024 * 1024

_NT = (((1,), (1,)), ((), ()))
_TN = (((0,), (0,)), ((), ()))


def _rms_rows(x, g):
    return x * lax.rsqrt(jnp.mean(x * x, axis=-1, keepdims=True) + EPS) * g


def _mm(a, w):
    return jnp.dot(a.astype(BF16), w, preferred_element_type=F32)


def _cast_small(c, *xs):
    if c % 16 == 0:
        return tuple(x.astype(BF16) for x in xs)
    return xs


def _bf(*xs):
    return tuple(x.astype(BF16) for x in xs)


def _block_masks(n, c):
    r = lax.broadcasted_iota(jnp.int32, (n, n), 0)
    s = lax.broadcasted_iota(jnp.int32, (n, n), 1)
    sh = int(math.log2(c))
    same = (r >> sh) == (s >> sh)
    return same & (r >= s), same & (r > s), r == s


def _exact_dot(x, dot_piece):
    hi = x.astype(BF16)
    r = x - hi.astype(F32)
    mid = r.astype(BF16)
    lo = (r - mid.astype(F32)).astype(BF16)
    return dot_piece(hi) + dot_piece(mid) + dot_piece(lo)


def _dot(a, b):
    return jnp.dot(a, b, preferred_element_type=F32)


def _dot_nt(a, b):
    return lax.dot_general(a, b, _NT, preferred_element_type=F32)


def _dot_tn(a, b):
    return lax.dot_general(a, b, _TN, preferred_element_type=F32)


def _pool_kernel(n_valid, h_ref, buf_ref, g_ref, win_ref, wgrp_ref, scale_ref, wout_ref,
                 o_ref, st_ref, z_ref):
    bb, tl, _ = h_ref.shape
    rows = bb * tl
    l = pl.program_id(1)
    seqs = range(bb)
    slabs_per_group = POOL_GROUP // LANES
    pitch = POOL_PAD + tl

    @pl.when(l == 0)
    def _():
        for b in seqs:
            buf = buf_ref[b if buf_ref.shape[0] == bb else 0]
            for s in range(D_MODEL // LANES):
                z_ref[s, b * pitch + POOL_PAD - POOL_BUF:b * pitch + POOL_PAD, :] = buf[:, s * LANES:(s + 1) * LANES]

    h = h_ref[...].reshape(rows, D_MODEL)
    xn = _rms_rows(h, g_ref[...])
    xn = xn.astype(BF16)
    ug = _mm(xn, win_ref[:, :D_MODEL])
    gate = _mm(xn, win_ref[:, D_MODEL:])
    for b in seqs:
        for s in range(D_MODEL // LANES):
            z_ref[s, b * pitch + POOL_PAD:(b + 1) * pitch, :] = ug[b * tl:(b + 1) * tl, s * LANES:(s + 1) * LANES]

    t = (l * tl + lax.broadcasted_iota(jnp.int32, (tl, LANES), 0)).astype(F32)
    parts = []
    for gi, w in enumerate(POOL_WINDOWS):
        cnt = jnp.minimum(float(w), t + (1.0 + n_valid))
        mixed_rows = []
        for b in seqs:
            cols = []
            for s in range(gi * slabs_per_group, (gi + 1) * slabs_per_group):
                cur = ug[b * tl:(b + 1) * tl, s * LANES:(s + 1) * LANES]
                acc = cur
                for j in range(1, w):
                    acc = acc + z_ref[s, pl.ds(b * pitch + POOL_PAD - j, tl, stride=1), :]
                cols.append(acc / cnt - cur)
            mixed_rows.append(jnp.concatenate(cols, axis=1))
        parts.append(_mm(jnp.concatenate(mixed_rows, axis=0), wgrp_ref[gi]))
    mixed = jnp.concatenate(parts, axis=-1) * scale_ref[...]
    y = _mm(mixed * jax.nn.silu(gate), wout_ref[...])
    o_ref[...] = (h + y).reshape(bb, tl, D_MODEL)

    @pl.when(l == pl.num_programs(1) - 1)
    def _():
        for b in seqs:
            st_ref[b] = jnp.concatenate(
                [z_ref[s, b * pitch + tl + 1:(b + 1) * pitch, :] for s in range(D_MODEL // LANES)], axis=1)

    for b in seqs:
        for s in range(D_MODEL // LANES):
            z_ref[s, b * pitch:b * pitch + POOL_PAD, :] = z_ref[s, b * pitch + tl:(b + 1) * pitch, :]


def _const_spec(shape):
    nd = len(shape)
    return pl.BlockSpec(shape, lambda b, l: (0,) * nd, pipeline_mode=pl.Buffered(1))


def _state_spec(block, shared):
    nd = len(block)
    if shared:
        return pl.BlockSpec(block, lambda b, l: (0,) * nd)
    return pl.BlockSpec(block, lambda b, l: (b,) + (0,) * (nd - 1))


def _params():
    return pltpu.CompilerParams(dimension_semantics=("parallel", "arbitrary"),
                                vmem_limit_bytes=VMEM_LIMIT)


def _pool_layer(h, buf, n_valid, w, bb, tl):
    bsz, seq, _ = h.shape
    shared = buf.shape[0] != bsz
    hspec = pl.BlockSpec((bb, tl, D_MODEL), lambda b, l: (b, l, 0))
    return pl.pallas_call(
        functools.partial(_pool_kernel, float(n_valid)),
        grid=(bsz // bb, seq // tl),
        in_specs=[hspec, _state_spec((1 if shared else bb, POOL_BUF, D_MODEL), shared),
                  _const_spec((1, D_MODEL)), _const_spec((D_MODEL, 2 * D_MODEL)),
                  _const_spec((len(POOL_WINDOWS), POOL_GROUP, POOL_GROUP)),
                  _const_spec((1, D_MODEL)), _const_spec((D_MODEL, D_MODEL))],
        out_specs=[hspec, pl.BlockSpec((bb, POOL_BUF, D_MODEL), lambda b, l: (b, 0, 0))],
        out_shape=[jax.ShapeDtypeStruct(h.shape, F32),
                   jax.ShapeDtypeStruct((bsz, POOL_BUF, D_MODEL), F32)],
        scratch_shapes=[pltpu.VMEM((D_MODEL // LANES, bb * (POOL_PAD + tl), LANES), F32)],
        compiler_params=_params(),
        name="pool_layer",
    )(h, buf, w["g"], w["w_in"], w["w_grp"], w["scale"], w["w_out"])


def _gla_kernel(h_ref, s0_ref, g_ref, wq_ref, wk_ref, wv_ref, wgate_ref, wglow_ref, wgk_ref,
                bgk_ref, ng_ref, wout_ref, o_ref, s_ref):
    bb, tl, _ = h_ref.shape
    rows = bb * tl

    @pl.when(pl.program_id(1) == 0)
    def _():
        s_ref[...] = jnp.broadcast_to(s0_ref[...], s_ref.shape)

    h = h_ref[...].reshape(rows, D_MODEL)
    xn = _rms_rows(h, g_ref[...]).astype(BF16)
    glow = _mm(xn, wglow_ref[...])
    gk = jax.nn.log_sigmoid(_mm(glow, wgk_ref[...]) + bgk_ref[...]) / GLA_GATE_NORM
    q = _mm(xn, wq_ref[...]) * GLA_DK ** -0.5
    k = _mm(xn, wk_ref[...])
    v = _mm(xn, wv_ref[...])
    gate = _mm(xn, wgate_ref[...])

    unit_seqs = max(1, min(bb, MXU_ROWS // tl))
    assert bb % unit_seqs == 0
    n_unit = unit_seqs * tl
    incl, _, _ = _block_masks(n_unit, tl)
    tri = incl.astype(F32).astype(BF16)
    eye_k = (lax.broadcasted_iota(jnp.int32, (GLA_DK, GLA_DK), 0)
             == lax.broadcasted_iota(jnp.int32, (GLA_DK, GLA_DK), 1))
    heads = range(GLA_HEADS)
    ksl = [slice(hd * GLA_DK, (hd + 1) * GLA_DK) for hd in heads]
    vsl = [slice(hd * GLA_DV, (hd + 1) * GLA_DV) for hd in heads]
    blks = [slice(i * tl, (i + 1) * tl) for i in range(unit_seqs)]

    o_units = []
    for u in range(bb // unit_seqs):
        ru = slice(u * n_unit, (u + 1) * n_unit)
        seqs = range(u * unit_seqs, (u + 1) * unit_seqs)
        ku, vu = k[ru], v[ru]
        bc = _exact_dot(gk[ru], lambda p: _dot(tri, p))
        qg = q[ru] * jnp.exp(bc)
        kg = ku * jnp.exp(-bc)
        kd, ebl = [], []
        for i in range(unit_seqs):
            bl = bc[(i + 1) * tl - 1:(i + 1) * tl, :]
            kd.append(ku[blks[i]] * jnp.exp(bl - bc[blks[i]]))
            ebl.append([jnp.exp(jnp.sum(
                jnp.where(eye_k, jnp.broadcast_to(bl[:, ksl[hd]], (GLA_DK, GLA_DK)), 0.0),
                axis=1, keepdims=True)) for hd in heads])
        qg_b, kg_b, v_b = _bf(qg, kg, vu)
        att = [jnp.where(incl, _dot_nt(qg_b[:, ksl[hd]], kg_b[:, ksl[hd]]), 0.0) for hd in heads]
        o = [_dot(att[hd].astype(BF16), v_b[:, vsl[hd]]) for hd in heads]
        st = [[s_ref[b, hd] for hd in heads] for b in seqs]
        o_st = [[None] * GLA_HEADS for _ in seqs]
        for i in range(unit_seqs):
            for hd in heads:
                qi, si = _cast_small(tl, qg[blks[i], ksl[hd]], st[i][hd])
                o_st[i][hd] = _dot(qi, si)
        for i, b in enumerate(seqs):
            for hd in heads:
                kdi, vi = _cast_small(tl, kd[i][:, ksl[hd]], vu[blks[i], vsl[hd]])
                s_ref[b, hd] = ebl[i][hd] * st[i][hd] + _dot_tn(kdi, vi)
        o_units.append(jnp.concatenate(
            [o[hd] + jnp.concatenate([o_st[i][hd] for i in range(unit_seqs)], axis=0) for hd in heads], axis=1))
    o_all = jnp.concatenate(o_units, axis=0)

    parts = []
    for hd in range(GLA_HEADS):
        parts.append(_rms_rows(o_all[:, hd * GLA_DV:(hd + 1) * GLA_DV], ng_ref[...]))
    on = jnp.concatenate(parts, axis=-1)
    out = h + _mm(on * jax.nn.silu(gate), wout_ref[...])
    for i in range(bb):
        o_ref[:, i * D_MODEL:(i + 1) * D_MODEL] = out[i * tl:(i + 1) * tl]


def _gla_layer(h, s0, w, bb, tl):
    bsz, seq, _ = h.shape
    shared = s0.shape[0] != bsz
    sblock = (bb, GLA_HEADS, GLA_DK, GLA_DV)
    hspec = pl.BlockSpec((bb, tl, D_MODEL), lambda b, l: (b, l, 0))
    return pl.pallas_call(
        _gla_kernel,
        grid=(bsz // bb, seq // tl),
        in_specs=[hspec, _state_spec((1,) + sblock[1:] if shared else sblock, shared),
                  _const_spec((1, D_MODEL)),
                  _const_spec((D_MODEL, GLA_QK)), _const_spec((D_MODEL, GLA_QK)),
                  _const_spec((D_MODEL, GLA_V)), _const_spec((D_MODEL, GLA_V)),
                  _const_spec((D_MODEL, LANES)), _const_spec((LANES, GLA_QK)),
                  _const_spec((1, GLA_QK)), _const_spec((1, GLA_DV)),
                  _const_spec((GLA_V, D_MODEL))],
        out_specs=[pl.BlockSpec((tl, bb * D_MODEL), lambda b, l: (l, b)),
                   pl.BlockSpec(sblock, lambda b, l: (b, 0, 0, 0))],
        out_shape=[jax.ShapeDtypeStruct((seq, bsz * D_MODEL), F32),
                   jax.ShapeDtypeStruct((bsz,) + sblock[1:], F32)],
        compiler_params=_params(),
        name="gla_layer",
    )(h, s0, w["g"], w["wq"], w["wk"], w["wv"], w["wgate"], w["wglow"], w["wgk"], w["bgk"],
      w["ng"], w["w_out"])


def _s5_prep_kernel(logdt_ref, are_ref, aim_ref, bre_ref, bim_ref, cre_ref, cim_ref,
                    abr_ref, abi_ref, wb_ref, wc_ref):
    dt = jnp.exp(logdt_ref[...])
    lr = are_ref[...]
    li = aim_ref[...]
    mag = jnp.exp(lr * dt)
    abr = mag * jnp.cos(li * dt)
    abi = mag * jnp.sin(li * dt)
    den = lr * lr + li * li
    cr = ((abr - 1.0) * lr + abi * li) / den
    ci = (abi * lr - (abr - 1.0) * li) / den
    abr_ref[...] = abr
    abi_ref[...] = abi
    br = bre_ref[...]
    bi = bim_ref[...]
    bbr = (cr[:, None, :] * br - ci[:, None, :] * bi).astype(BF16)
    bbi = (cr[:, None, :] * bi + ci[:, None, :] * br).astype(BF16)
    c_re = cre_ref[...].astype(BF16)
    c_im = (-cim_ref[...]).astype(BF16)
    wb_ref[...] = jnp.zeros(wb_ref.shape, BF16)
    wc_ref[...] = jnp.zeros(wc_ref.shape, BF16)
    for j in range(S5_NB):
        for gl in range(S5_BUNDLE):
            g = j * S5_BUNDLE + gl
            ch = slice(gl * S5_GROUP, (gl + 1) * S5_GROUP)
            re = slice(gl * S5_N, (gl + 1) * S5_N)
            im = slice(S5_HALF + gl * S5_N, S5_HALF + (gl + 1) * S5_N)
            wb_ref[j, ch, re] = bbr[g]
            wb_ref[j, ch, im] = bbi[g]
            wc_ref[j, re, ch] = c_re[g]
            wc_ref[j, im, ch] = c_im[g]


def _s5_prep(log_dt, a_re, a_im, b_re, b_im, c_re, c_im):
    gn = jax.ShapeDtypeStruct((S5_G, S5_N), F32)
    in_w = S5_BUNDLE * S5_GROUP
    return pl.pallas_call(
        _s5_prep_kernel,
        out_shape=[gn, gn, jax.ShapeDtypeStruct((S5_NB, in_w, 2 * S5_HALF), BF16),
                   jax.ShapeDtypeStruct((S5_NB, 2 * S5_HALF, in_w), BF16)],
        name="s5_discretize")(
        log_dt.reshape(S5_G, 1), a_re, a_im, b_re.transpose(0, 2, 1), b_im.transpose(0, 2, 1),
        c_re.transpose(0, 2, 1), c_im.transpose(0, 2, 1))


def _s5_kernel(h_ref, h0_ref, g_ref, win_ref, wb_ref, wc_ref, abr_ref, abi_ref, dskip_ref,
               wglu_ref, bglu_ref, wout_ref, o_ref, hs_ref):
    tl, nb, _ = h_ref.shape
    rows = tl * nb
    ngrp = nb // SUBLANES

    @pl.when(pl.program_id(1) == 0)
    def _():
        hs_ref[...] = h0_ref[...]

    h = h_ref[...].reshape(rows, D_MODEL)
    xn = _rms_rows(h, g_ref[...]).astype(BF16)
    ug = _mm(xn, win_ref[...])
    u = ug[:, :D_MODEL]
    gate = ug[:, D_MODEL:]
    ub = u.astype(BF16)
    width = 2 * S5_HALF
    in_w = S5_BUNDLE * S5_GROUP

    def project_in(j):
        return jnp.dot(ub[:, j * in_w:(j + 1) * in_w], wb_ref[j], preferred_element_type=F32)

    parts = []
    bu_next = project_in(0)
    for j in range(S5_NB):
        bu = bu_next
        if j + 1 < S5_NB:
            bu_next = project_in(j + 1)
        re = slice(j * width, j * width + S5_HALF)
        im = slice(j * width + S5_HALF, (j + 1) * width)
        ar = jnp.broadcast_to(abr_ref[:, j * S5_HALF:(j + 1) * S5_HALF], (SUBLANES, S5_HALF))
        ai = jnp.broadcast_to(abi_ref[:, j * S5_HALF:(j + 1) * S5_HALF], (SUBLANES, S5_HALF))
        tiles = [None] * (tl * ngrp)
        for bg in range(ngrp):
            grp = slice(bg * SUBLANES, (bg + 1) * SUBLANES)
            hr = hs_ref[grp, re]
            hi = hs_ref[grp, im]
            for t in range(tl):
                r = t * nb + bg * SUBLANES
                hr, hi = (ar * hr - ai * hi + bu[r:r + SUBLANES, :S5_HALF],
                          ar * hi + ai * hr + bu[r:r + SUBLANES, S5_HALF:])
                tiles[t * ngrp + bg] = jnp.concatenate([hr, hi], axis=1)
            hs_ref[grp, re] = hr
            hs_ref[grp, im] = hi
        parts.append(_mm(jnp.concatenate(tiles, axis=0), wc_ref[j]))
    y = jnp.concatenate(parts, axis=-1) + dskip_ref[...] * u
    z = jax.nn.gelu(y)
    z = z * jax.nn.sigmoid(_mm(z, wglu_ref[...]) + bglu_ref[...])
    out = h + _mm(z * jax.nn.silu(gate), wout_ref[...])
    o_ref[...] = out.reshape(tl, nb, D_MODEL)


def _s5_layer(h_tb, h0, w, nb, tl):
    seq, bsz, _ = h_tb.shape
    hspec = pl.BlockSpec((tl, nb, D_MODEL), lambda b, l: (l, b, 0))
    sspec = pl.BlockSpec((nb, S5_STATE), lambda b, l: (b, 0))
    in_w = S5_BUNDLE * S5_GROUP
    return pl.pallas_call(
        _s5_kernel,
        grid=(bsz // nb, seq // tl),
        in_specs=[hspec, sspec, _const_spec((1, D_MODEL)), _const_spec((D_MODEL, 2 * D_MODEL)),
                  _const_spec((S5_NB, in_w, 2 * S5_HALF)), _const_spec((S5_NB, 2 * S5_HALF, in_w)),
                  _const_spec((1, S5_G * S5_N)), _const_spec((1, S5_G * S5_N)),
                  _const_spec((1, D_MODEL)), _const_spec((D_MODEL, D_MODEL)),
                  _const_spec((1, D_MODEL)), _const_spec((D_MODEL, D_MODEL))],
        out_specs=[hspec, sspec],
        out_shape=[jax.ShapeDtypeStruct(h_tb.shape, F32), jax.ShapeDtypeStruct((bsz, S5_STATE), F32)],
        compiler_params=_params(),
        name="s5_layer",
    )(h_tb, h0, w["g"], w["w_in"], w["wb"], w["wc"], w["abr"], w["abi"], w["d"], w["w_glu"],
      w["b_glu"], w["w_out"])


def _s5_pack_state(re, im):
    b = re.shape[0]
    return jnp.concatenate([re.reshape(b, S5_NB, S5_HALF), im.reshape(b, S5_NB, S5_HALF)],
                           axis=-1).reshape(b, S5_STATE)


def _s5_unpack_state(st):
    b = st.shape[0]
    st = st.reshape(b, S5_NB, 2 * S5_HALF)
    return (st[..., :S5_HALF].reshape(b, S5_G, S5_N), st[..., S5_HALF:].reshape(b, S5_G, S5_N))


def _gdn_kernel(nbu, h_ref, s0_ref, cb_ref, g_ref, wqkv_ref, wz_ref, wab_ref, convw_ref, alog_ref,
                dtb_ref, ng_ref, fg_ref, wout_ref, o_ref, s_ref, cbout_ref, ext_ref):
    bb, tl, _ = o_ref.shape
    rows = bb * tl
    l = pl.program_id(1)

    heads = range(GDN_HEADS)
    seqs = range(bb)
    slabs = range(GDN_QKV // LANES)
    pitch = GDN_PAD + tl
    hist = GDN_PAD - (GDN_CONV - 1)

    @pl.when(l == 0)
    def _():
        s_ref[...] = jnp.broadcast_to(s0_ref[...], s_ref.shape)
        for b in seqs:
            cb = cb_ref[b if cb_ref.shape[0] == bb else 0]
            for s in slabs:
                ext_ref[s, b * pitch + hist:b * pitch + GDN_PAD, :] = cb[:, s * LANES:(s + 1) * LANES]

    h = jnp.concatenate([h_ref[:, i * D_MODEL:(i + 1) * D_MODEL] for i in range(bb)], axis=0)
    xn = _rms_rows(h, g_ref[...]).astype(BF16)
    ab = _mm(xn, wab_ref[...])
    qkv = _mm(xn, wqkv_ref[...])
    z = _mm(xn, wz_ref[...])
    for b in seqs:
        for s in slabs:
            ext_ref[s, b * pitch + GDN_PAD:(b + 1) * pitch, :] = qkv[b * tl:(b + 1) * tl, s * LANES:(s + 1) * LANES]
    conv_rows = []
    for b in seqs:
        cols = []
        for s in slabs:
            lanes = slice(s * LANES, (s + 1) * LANES)
            acc = convw_ref[GDN_CONV - 1:GDN_CONV, lanes] * qkv[b * tl:(b + 1) * tl, lanes]
            for j in range(GDN_CONV - 1):
                acc = acc + convw_ref[j:j + 1, lanes] * ext_ref[s, pl.ds(b * pitch + hist + j, tl, stride=1), :]
            cols.append(acc)
        conv_rows.append(jnp.concatenate(cols, axis=1))
    act = jax.nn.silu(jnp.concatenate(conv_rows, axis=0))
    hsl = [slice(hd * GDN_DK, (hd + 1) * GDN_DK) for hd in heads]
    q_parts, k_parts = [], []
    for hd in heads:
        qh = act[:, hd * GDN_DK:(hd + 1) * GDN_DK]
        kh = act[:, GDN_QK + hd * GDN_DK:GDN_QK + (hd + 1) * GDN_DK]
        q_parts.append(qh * lax.rsqrt(jnp.sum(qh * qh, axis=-1, keepdims=True) + EPS) * GDN_DK ** -0.5)
        k_parts.append(kh * lax.rsqrt(jnp.sum(kh * kh, axis=-1, keepdims=True) + EPS))
    q = jnp.concatenate(q_parts, axis=1)
    k = jnp.concatenate(k_parts, axis=1)
    v = act[:, 2 * GDN_QK:]
    g = -jnp.exp(alog_ref[...]) * jax.nn.softplus(ab + dtb_ref[...])
    beta = jax.nn.sigmoid(ab)

    group = 4
    sup = nbu * tl
    width = group * sup
    n_units = bb // nbu
    n_sq = int(math.log2(tl)) - 1
    lg_sup = int(math.log2(sup))
    lg_tl = int(math.log2(tl))
    assert bb % nbu == 0 and 1 << lg_sup == sup and 1 << lg_tl == tl and n_sq >= 1
    cum_rows = min(rows, max(tl, MXU_ROWS))
    assert rows % cum_rows == 0
    tri = _block_masks(cum_rows, tl)[0].astype(F32).astype(BF16)
    t4 = lax.broadcasted_iota(jnp.int32, (sup, width), 0)
    col4 = lax.broadcasted_iota(jnp.int32, (sup, width), 1)
    s4 = col4 & (sup - 1)
    same4 = (t4 >> lg_tl) == (s4 >> lg_tl)
    incl4 = same4 & (t4 >= s4)
    strict4 = same4 & (t4 > s4)
    eye4 = (t4 == s4).astype(F32)
    bd_mask = ((lax.broadcasted_iota(jnp.int32, (width, width), 0) >> lg_sup)
               == (lax.broadcasted_iota(jnp.int32, (width, width), 1) >> lg_sup))
    head4 = col4 >> lg_sup

    def per_head_lanes(x, first):
        return jnp.concatenate(
            [jnp.broadcast_to(x[:, first + hd:first + hd + 1], (rows, GDN_DK)) for hd in heads], axis=1)

    gcum = jnp.concatenate([_exact_dot(g[r0:r0 + cum_rows], lambda p: _dot(tri, p))
                            for r0 in range(0, rows, cum_rows)], axis=0)
    gcx = per_head_lanes(gcum, 0)
    bx = per_head_lanes(beta, GDN_HEADS)
    egx = jnp.exp(gcx)
    kb = k * bx
    qe = q * egx
    q_b, k_b, kb_b = _bf(q, k, kb)
    rhs_v = v * bx
    rhs_k = kb * egx
    kend, egl = [], []
    for b in seqs:
        gl = gcx[(b + 1) * tl - 1:(b + 1) * tl, :]
        kend.append(k[b * tl:(b + 1) * tl] * jnp.exp(gl - gcx[b * tl:(b + 1) * tl]))
        egl.append(jnp.exp(gl))

    keys = [(u, hg) for u in range(n_units) for hg in range(GDN_HEADS // group)]
    m4, att4 = {}, {}
    for u, hg in keys:
        ru = slice(u * sup, (u + 1) * sup)
        lanes4 = slice(hg * group * GDN_DK, (hg + 1) * group * GDN_DK)
        kdiag = []
        for hh in range(group):
            pieces = [jnp.zeros((sup, GDN_DK), BF16)] * group
            pieces[hh] = k_b[ru, hsl[hg * group + hh]]
            kdiag.append(jnp.concatenate(pieces, axis=1))
        kdiag = jnp.concatenate(kdiag, axis=0)
        kk = _dot_nt(kb_b[ru, lanes4], kdiag)
        qk = _dot_nt(q_b[ru, lanes4], kdiag)
        gcol = None
        for hh in range(group):
            rep = gcx[ru, hsl[hg * group + hh]]
            rep = rep[:, :width] if width <= GDN_DK else jnp.concatenate([rep] * (width // GDN_DK), axis=1)
            gcol = rep if gcol is None else jnp.where(head4 == hh, rep, gcol)
        grow = jnp.sum(eye4 * gcol, axis=0, keepdims=True)
        decay = jnp.exp(gcol - grow)
        m4[u, hg] = -jnp.where(strict4, kk * decay, 0.0)
        att4[u, hg] = jnp.where(incl4, qk * decay, 0.0)

    def block_diag(m_b):
        return jnp.where(bd_mask, jnp.concatenate([m_b] * group, axis=0), jnp.zeros((), BF16))

    p4 = {key: eye4 + m4[key] for key in keys}
    m_b = {key: m4[key].astype(BF16) for key in keys}
    m4 = {key: _dot(m_b[key], block_diag(m_b[key])) for key in keys}
    for _ in range(1, n_sq):
        m_b = {key: m4[key].astype(BF16) for key in keys}
        x = {key: _dot(jnp.concatenate([p4[key].astype(BF16), m_b[key]], axis=0), block_diag(m_b[key]))
             for key in keys}
        p4 = {key: p4[key] + x[key][:sup] for key in keys}
        m4 = {key: x[key][sup:] for key in keys}
    m_b = {key: m4[key].astype(BF16) for key in keys}
    p4 = {key: p4[key] + _dot(p4[key].astype(BF16), block_diag(m_b[key])) for key in keys}

    uw = [[None] * GDN_HEADS for _ in range(n_units)]
    for u, hg in keys:
        ru = slice(u * sup, (u + 1) * sup)
        p_b = p4[u, hg].astype(BF16)
        for hh in range(group):
            hd = hg * group + hh
            rhs = jnp.concatenate([rhs_v[ru, hsl[hd]], rhs_k[ru, hsl[hd]]], axis=1)
            uw[u][hd] = _dot(p_b[:, hh * sup:(hh + 1) * sup], rhs.astype(BF16))

    st = [[s_ref[b, hd] for hd in heads] for b in seqs]
    v_new = [[None] * GDN_HEADS for _ in seqs]
    qs = [[None] * GDN_HEADS for _ in seqs]
    for b in seqs:
        u, i = divmod(b, nbu)
        for hd in heads:
            wq = jnp.concatenate([uw[u][hd][i * tl:(i + 1) * tl, GDN_DV:], qe[b * tl:(b + 1) * tl, hsl[hd]]],
                                 axis=0)
            ws = _dot(wq.astype(BF16), st[b][hd].astype(BF16))
            v_new[b][hd] = uw[u][hd][i * tl:(i + 1) * tl, :GDN_DV] - ws[:tl]
            qs[b][hd] = ws[tl:]
    o_units = []
    for u in range(n_units):
        o_heads = []
        for hd in heads:
            hg, hh = divmod(hd, group)
            members = range(u * nbu, (u + 1) * nbu)
            vn = jnp.concatenate([v_new[b][hd] for b in members], axis=0)
            att = att4[u, hg][:, hh * sup:(hh + 1) * sup]
            o_heads.append(jnp.concatenate([qs[b][hd] for b in members], axis=0)
                           + _dot(att.astype(BF16), vn.astype(BF16)))
        o_units.append(jnp.concatenate(o_heads, axis=1))
    for b in seqs:
        for hd in heads:
            ke, vn = _cast_small(tl, kend[b][:, hsl[hd]], v_new[b][hd])
            s_ref[b, hd] = egl[b][:, hsl[hd]] * st[b][hd] + _dot_tn(ke, vn)
    o_all = jnp.concatenate(o_units, axis=0)

    parts = []
    for hd in range(GDN_HEADS):
        parts.append(_rms_rows(o_all[:, hd * GDN_DV:(hd + 1) * GDN_DV], ng_ref[...]))
    on = jnp.concatenate(parts, axis=-1)
    out = h + _mm(on * jax.nn.silu(z), wout_ref[...])
    o_ref[...] = _rms_rows(out, fg_ref[...]).reshape(bb, tl, D_MODEL)

    @pl.when(l == pl.num_programs(1) - 1)
    def _():
        for b in seqs:
            cbout_ref[b] = jnp.concatenate(
                [ext_ref[s, b * pitch + tl + hist:(b + 1) * pitch, :] for s in slabs], axis=1)

    for b in seqs:
        for s in slabs:
            ext_ref[s, b * pitch:b * pitch + GDN_PAD, :] = ext_ref[s, b * pitch + tl:(b + 1) * pitch, :]


def _gdn_layer(h_tm, s0, cb, w, bb, tl, nbu):
    seq = h_tm.shape[0]
    bsz = h_tm.shape[1] // D_MODEL
    shared = s0.shape[0] != bsz
    sblock = (bb, GDN_HEADS, GDN_DK, GDN_DV)
    cblock = (bb, GDN_CONV - 1, GDN_QKV)
    hspec = pl.BlockSpec((bb, tl, D_MODEL), lambda b, l: (b, l, 0))
    return pl.pallas_call(
        functools.partial(_gdn_kernel, nbu),
        grid=(bsz // bb, seq // tl),
        in_specs=[pl.BlockSpec((tl, bb * D_MODEL), lambda b, l: (l, b)),
                  _state_spec((1,) + sblock[1:] if shared else sblock, shared),
                  _state_spec((1,) + cblock[1:] if shared else cblock, shared),
                  _const_spec((1, D_MODEL)), _const_spec((D_MODEL, GDN_QKV)),
                  _const_spec((D_MODEL, GDN_V)), _const_spec((D_MODEL, LANES)),
                  _const_spec((GDN_CONV, GDN_QKV)), _const_spec((1, LANES)), _const_spec((1, LANES)),
                  _const_spec((1, GDN_DV)), _const_spec((1, D_MODEL)), _const_spec((GDN_V, D_MODEL))],
        out_specs=[hspec, pl.BlockSpec(sblock, lambda b, l: (b, 0, 0, 0)),
                   pl.BlockSpec((bb, GDN_CONV - 1, GDN_QKV), lambda b, l: (b, 0, 0))],
        out_shape=[jax.ShapeDtypeStruct((bsz, seq, D_MODEL), F32),
                   jax.ShapeDtypeStruct((bsz,) + sblock[1:], F32),
                   jax.ShapeDtypeStruct((bsz, GDN_CONV - 1, GDN_QKV), F32)],
        scratch_shapes=[pltpu.VMEM((GDN_QKV // LANES, bb * (GDN_PAD + tl), LANES), F32)],
        compiler_params=_params(),
        name="gdn_layer",
    )(h_tm, s0, cb, w["g"], w["wqkv"], w["wz"], w["wab"], w["conv_w"], w["a_log"], w["dt_bias"],
      w["ng"], w["fg"], w["w_out"])


def _row(x, width=None):
    x = x.reshape(1, -1).astype(F32)
    if width is not None and x.shape[1] < width:
        x = jnp.pad(x, ((0, 0), (0, width - x.shape[1])))
    return x


def _pad_cols(w, width):
    return jnp.pad(w, ((0, 0), (0, width - w.shape[1])))


def _s5_weights(j, norm_g, s5_w_in, s5_b_re, s5_b_im, s5_c_re, s5_c_im, s5_d, s5_log_dt, s5_a_re,
                s5_a_im, s5_w_glu, s5_b_glu, s5_w_out):
    abr, abi, wb, wc = _s5_prep(s5_log_dt[j], s5_a_re[j], s5_a_im[j], s5_b_re[j], s5_b_im[j],
                                s5_c_re[j], s5_c_im[j])
    return dict(g=_row(norm_g), w_in=s5_w_in[j].astype(BF16), wb=wb, wc=wc,
                abr=_row(abr), abi=_row(abi), d=_row(s5_d[j]), w_glu=s5_w_glu[j].astype(BF16),
                b_glu=_row(s5_b_glu[j]), w_out=s5_w_out[j].astype(BF16))


def kernel(x_prompt, x_sample, state_pool, state_gla, state_s5_re, state_s5_im, state_gdn, state_gdn_conv, meta_tokens, norm_g, final_norm_g, pool_w_in, pool_w_grp, pool_scale, pool_w_out, gla_w_in, gla_w_gk, gla_b_gk, gla_norm_g, gla_w_out, s5_w_in, s5_b_re, s5_b_im, s5_c_re, s5_c_im, s5_d, s5_log_dt, s5_a_re, s5_a_im, s5_w_glu, s5_b_glu, s5_w_out, gdn_w_in, gdn_conv_w, gdn_a_log, gdn_dt_bias, gdn_norm_g, gdn_w_out):
    bp = x_prompt.shape[0]
    bs, ls, _ = x_sample.shape

    wp = dict(g=_row(norm_g[0]), w_in=pool_w_in[0].astype(BF16), w_grp=pool_w_grp[0].astype(BF16),
              scale=_row(pool_scale[0]), w_out=pool_w_out[0].astype(BF16))
    gw = gla_w_in[0]
    wg = dict(g=_row(norm_g[1]), wq=gw[:, :GLA_QK].astype(BF16), wk=gw[:, GLA_QK:2 * GLA_QK].astype(BF16),
              wv=gw[:, 2 * GLA_QK:2 * GLA_QK + GLA_V].astype(BF16),
              wgate=gw[:, 2 * GLA_QK + GLA_V:2 * GLA_QK + 2 * GLA_V].astype(BF16),
              wglow=_pad_cols(gw[:, 2 * GLA_QK + 2 * GLA_V:], LANES).astype(BF16),
              wgk=jnp.pad(gla_w_gk[0], ((0, LANES - GLA_RANK), (0, 0))).astype(BF16),
              bgk=_row(gla_b_gk[0]), ng=_row(gla_norm_g[0]), w_out=gla_w_out[0].astype(BF16))
    ws = _s5_weights(0, norm_g[2], s5_w_in, s5_b_re, s5_b_im, s5_c_re, s5_c_im, s5_d, s5_log_dt,
                     s5_a_re, s5_a_im, s5_w_glu, s5_b_glu, s5_w_out)
    dw = gdn_w_in[0]
    wd = dict(g=_row(norm_g[3]), wqkv=dw[:, :GDN_QKV].astype(BF16),
              wz=dw[:, GDN_QKV:GDN_QKV + GDN_V].astype(BF16),
              wab=_pad_cols(dw[:, GDN_QKV + GDN_V:], LANES).astype(BF16), conv_w=gdn_conv_w[0],
              a_log=_row(gdn_a_log[0], LANES), dt_bias=_row(gdn_dt_bias[0], LANES),
              ng=_row(gdn_norm_g[0]), fg=_row(final_norm_g), w_out=gdn_w_out[0].astype(BF16))

    def run(h, pool_st, n_valid, gla_st, s5_st, gdn_st, conv_st, blocks):
        (pb, pt), (gb, gt), (sb, stl), (db, dtl, dn) = blocks
        bsz, seq, _ = h.shape
        h, pool_new = _pool_layer(h, pool_st, n_valid, wp, pb, pt)
        h_tm, gla_new = _gla_layer(h, gla_st, wg, gb, gt)
        h_tb, s5_new = _s5_layer(h_tm.reshape(seq, bsz, D_MODEL), s5_st, ws, sb, stl)
        y, gdn_new, conv_new = _gdn_layer(h_tb.reshape(seq, bsz * D_MODEL), gdn_st, conv_st,
                                          wd, db, dtl, dn)
        return y, pool_new, gla_new, s5_new, gdn_new, conv_new

    hm = meta_tokens.astype(F32)[None]
    zeros = lambda *s: jnp.zeros(s, F32)
    hm, m_pool = _pool_layer(hm, zeros(1, POOL_BUF, D_MODEL), 0, wp, 1, N_META)
    hm, m_gla = _gla_layer(hm, zeros(1, GLA_HEADS, GLA_DK, GLA_DV), wg, 1, N_META)
    hm_tb, m_s5 = _s5_layer(jnp.broadcast_to(hm[:, None, :], (N_META, SUBLANES, D_MODEL)),
                            zeros(SUBLANES, S5_STATE), ws, SUBLANES, N_META)
    _, m_gdn, m_conv = _gdn_layer(hm_tb[:, 0], zeros(1, GDN_HEADS, GDN_DK, GDN_DV),
                                  zeros(1, GDN_CONV - 1, GDN_QKV), wd, 1, N_META, 1)

    yp, pool_p, gla_p, s5_p, gdn_p, conv_p = run(
        x_prompt, m_pool, N_META, m_gla, jnp.broadcast_to(m_s5[0:1], (bp, S5_STATE)), m_gdn, m_conv,
        ((1, 512), (8, 64), (bp, 64), (8, 64, 1)))
    ys, pool_s, gla_s, s5_s, gdn_s, conv_s = run(
        x_sample, state_pool[0], POOL_BUF, state_gla[0], _s5_pack_state(state_s5_re[0], state_s5_im[0]),
        state_gdn[0], state_gdn_conv[0], ((32, ls), (16, ls), (32, ls), (8, ls, 8)))

    s5r_p, s5i_p = _s5_unpack_state(s5_p)
    s5r_s, s5i_s = _s5_unpack_state(s5_s)
    return (yp, ys, pool_p[None], pool_s[None], gla_p[None], gla_s[None],
            s5r_p[None], s5i_p[None], s5r_s[None], s5i_s[None],
            gdn_p[None], conv_p[None], gdn_s[None], conv_s[None])
```

```python
import functools
import math

import jax
import jax.numpy as jnp
from jax import lax
from jax.experimental import pallas as pl
from jax.experimental.pallas import tpu as pltpu

F32 = jnp.float32
BF16 = jnp.bfloat16
HIGHEST = lax.Precision.HIGHEST

D_MODEL = 1024
EPS = 1e-6
N_META = 16

POOL_WINDOWS = (2, 4, 8, 16)
POOL_GROUP = D_MODEL // len(POOL_WINDOWS)
POOL_BUF = max(POOL_WINDOWS) - 1
POOL_PAD = POOL_BUF + 1

GLA_HEADS = 4
GLA_DK = 128
GLA_DV = 256
GLA_QK = GLA_HEADS * GLA_DK
GLA_V = GLA_HEADS * GLA_DV
GLA_RANK = 16
GLA_GATE_NORM = 16.0

S5_GROUP = 16
S5_G = D_MODEL // S5_GROUP
S5_N = 64
S5_BUNDLE = 8
S5_NB = S5_G // S5_BUNDLE
S5_HALF = S5_BUNDLE * S5_N
S5_STATE = 2 * S5_G * S5_N

GDN_HEADS = 8
GDN_DK = 128
GDN_DV = 128
GDN_CONV = 4
GDN_QK = GDN_HEADS * GDN_DK
GDN_V = GDN_HEADS * GDN_DV
GDN_QKV = 2 * GDN_QK + GDN_V
GDN_PAD = 8

LANES = 128
SUBLANES = 8
MXU_ROWS = 256
VMEM_LIMIT = 52 * 1024 * 1024

_NT = (((1,), (1,)), ((), ()))
_TN = (((0,), (0,)), ((), ()))


def _rms_rows(x, g):
    return x * lax.rsqrt(jnp.mean(x * x, axis=-1, keepdims=True) + EPS) * g


def _mm(a, w):
    return jnp.dot(a.astype(BF16), w, preferred_element_type=F32)


def _cast_small(c, *xs):
    if c % 16 == 0:
        return tuple(x.astype(BF16) for x in xs)
    return xs


def _bf(*xs):
    return tuple(x.astype(BF16) for x in xs)


def _block_masks(n, c):
    r = lax.broadcasted_iota(jnp.int32, (n, n), 0)
    s = lax.broadcasted_iota(jnp.int32, (n, n), 1)
    sh = int(math.log2(c))
    same = (r >> sh) == (s >> sh)
    return same & (r >= s), same & (r > s), r == s


def _exact_dot(x, dot_piece):
    hi = x.astype(BF16)
    r = x - hi.astype(F32)
    mid = r.astype(BF16)
    lo = (r - mid.astype(F32)).astype(BF16)
    return dot_piece(hi) + dot_piece(mid) + dot_piece(lo)


def _dot(a, b):
    return jnp.dot(a, b, preferred_element_type=F32)


def _dot_nt(a, b):
    return lax.dot_general(a, b, _NT, preferred_element_type=F32)


def _dot_tn(a, b):
    return lax.dot_general(a, b, _TN, preferred_element_type=F32)


def _pool_kernel(n_valid, h_ref, buf_ref, g_ref, win_ref, wgrp_ref, scale_ref, wout_ref,
                 o_ref, st_ref, z_ref):
    bb, tl, _ = h_ref.shape
    rows = bb * tl
    l = pl.program_id(1)
    seqs = range(bb)
    slabs_per_group = POOL_GROUP // LANES
    pitch = POOL_PAD + tl

    @pl.when(l == 0)
    def _():
        for b in seqs:
            buf = buf_ref[b if buf_ref.shape[0] == bb else 0]
            for s in range(D_MODEL // LANES):
                z_ref[s, b * pitch + POOL_PAD - POOL_BUF:b * pitch + POOL_PAD, :] = buf[:, s * LANES:(s + 1) * LANES]

    h = h_ref[...].reshape(rows, D_MODEL)
    xn = _rms_rows(h, g_ref[...])
    xn = xn.astype(BF16)
    ug = _mm(xn, win_ref[:, :D_MODEL])
    gate = _mm(xn, win_ref[:, D_MODEL:])
    for b in seqs:
        for s in range(D_MODEL // LANES):
            z_ref[s, b * pitch + POOL_PAD:(b + 1) * pitch, :] = ug[b * tl:(b + 1) * tl, s * LANES:(s + 1) * LANES]

    t = (l * tl + lax.broadcasted_iota(jnp.int32, (tl, LANES), 0)).astype(F32)
    parts = []
    for gi, w in enumerate(POOL_WINDOWS):
        cnt = jnp.minimum(float(w), t + (1.0 + n_valid))
        mixed_rows = []
        for b in seqs:
            cols = []
            for s in range(gi * slabs_per_group, (gi + 1) * slabs_per_group):
                cur = ug[b * tl:(b + 1) * tl, s * LANES:(s + 1) * LANES]
                acc = cur
                for j in range(1, w):
                    acc = acc + z_ref[s, pl.ds(b * pitch + POOL_PAD - j, tl, stride=1), :]
                cols.append(acc / cnt - cur)
            mixed_rows.append(jnp.concatenate(cols, axis=1))
        parts.append(_mm(jnp.concatenate(mixed_rows, axis=0), wgrp_ref[gi]))
    mixed = jnp.concatenate(parts, axis=-1) * scale_ref[...]
    y = _mm(mixed * jax.nn.silu(gate), wout_ref[...])
    o_ref[...] = (h + y).reshape(bb, tl, D_MODEL)

    @pl.when(l == pl.num_programs(1) - 1)
    def _():
        for b in seqs:
            st_ref[b] = jnp.concatenate(
                [z_ref[s, b * pitch + tl + 1:(b + 1) * pitch, :] for s in range(D_MODEL // LANES)], axis=1)

    for b in seqs:
        for s in range(D_MODEL // LANES):
            z_ref[s, b * pitch:b * pitch + POOL_PAD, :] = z_ref[s, b * pitch + tl:(b + 1) * pitch, :]


def _const_spec(shape):
    nd = len(shape)
    return pl.BlockSpec(shape, lambda b, l: (0,) * nd, pipeline_mode=pl.Buffered(1))


def _state_spec(block, shared=False):
    nd = len(block)
    if shared:
        return pl.BlockSpec((None, 1) + tuple(block[1:]), lambda b, l: (0,) * (nd + 1))
    return pl.BlockSpec((None,) + tuple(block), lambda b, l: (0, b) + (0,) * (nd - 1))


def _state_shape(bsz, block):
    return jax.ShapeDtypeStruct((1, bsz) + tuple(block[1:]), F32)


def _params():
    return pltpu.CompilerParams(dimension_semantics=("parallel", "arbitrary"),
                                vmem_limit_bytes=VMEM_LIMIT)


def _pool_layer(h, buf, n_valid, w, bb, tl):
    bsz, seq, _ = h.shape
    sblock = (bb, POOL_BUF, D_MODEL)
    hspec = pl.BlockSpec((bb, tl, D_MODEL), lambda b, l: (b, l, 0))
    return pl.pallas_call(
        functools.partial(_pool_kernel, float(n_valid)),
        grid=(bsz // bb, seq // tl),
        in_specs=[hspec, _state_spec(sblock, buf.shape[1] != bsz),
                  _const_spec((1, D_MODEL)), _const_spec((D_MODEL, 2 * D_MODEL)),
                  _const_spec((len(POOL_WINDOWS), POOL_GROUP, POOL_GROUP)),
                  _const_spec((1, D_MODEL)), _const_spec((D_MODEL, D_MODEL))],
        out_specs=[hspec, _state_spec(sblock)],
        out_shape=[jax.ShapeDtypeStruct(h.shape, F32), _state_shape(bsz, sblock)],
        scratch_shapes=[pltpu.VMEM((D_MODEL // LANES, bb * (POOL_PAD + tl), LANES), F32)],
        compiler_params=_params(),
        name="pool_layer",
    )(h, buf, w["g"], w["w_in"], w["w_grp"], w["scale"], w["w_out"])


def _gla_kernel(h_ref, s0_ref, g_ref, wq_ref, wk_ref, wv_ref, wgate_ref, wglow_ref, wgk_ref,
                bgk_ref, ng_ref, wout_ref, o_ref, s_ref):
    bb, tl, _ = h_ref.shape
    rows = bb * tl

    @pl.when(pl.program_id(1) == 0)
    def _():
        s_ref[...] = jnp.broadcast_to(s0_ref[...], s_ref.shape)

    h = h_ref[...].reshape(rows, D_MODEL)
    xn = _rms_rows(h, g_ref[...]).astype(BF16)
    glow = _mm(xn, wglow_ref[...])
    gk = jax.nn.log_sigmoid(_mm(glow, wgk_ref[...]) + bgk_ref[...]) / GLA_GATE_NORM
    q = _mm(xn, wq_ref[...]) * GLA_DK ** -0.5
    k = _mm(xn, wk_ref[...])
    v = _mm(xn, wv_ref[...])
    gate = _mm(xn, wgate_ref[...])

    unit_seqs = max(1, min(bb, MXU_ROWS // tl))
    assert bb % unit_seqs == 0
    n_unit = unit_seqs * tl
    incl, _, _ = _block_masks(n_unit, tl)
    tri = incl.astype(F32).astype(BF16)
    eye_k = (lax.broadcasted_iota(jnp.int32, (GLA_DK, GLA_DK), 0)
             == lax.broadcasted_iota(jnp.int32, (GLA_DK, GLA_DK), 1))
    heads = range(GLA_HEADS)
    ksl = [slice(hd * GLA_DK, (hd + 1) * GLA_DK) for hd in heads]
    vsl = [slice(hd * GLA_DV, (hd + 1) * GLA_DV) for hd in heads]
    blks = [slice(i * tl, (i + 1) * tl) for i in range(unit_seqs)]

    o_units = []
    for u in range(bb // unit_seqs):
        ru = slice(u * n_unit, (u + 1) * n_unit)
        seqs = range(u * unit_seqs, (u + 1) * unit_seqs)
        ku, vu = k[ru], v[ru]
        bc = _exact_dot(gk[ru], lambda p: _dot(tri, p))
        qg = q[ru] * jnp.exp(bc)
        kg = ku * jnp.exp(-bc)
        kd, ebl = [], []
        for i in range(unit_seqs):
            bl = bc[(i + 1) * tl - 1:(i + 1) * tl, :]
            kd.append(ku[blks[i]] * jnp.exp(bl - bc[blks[i]]))
            ebl.append([jnp.exp(jnp.sum(
                jnp.where(eye_k, jnp.broadcast_to(bl[:, ksl[hd]], (GLA_DK, GLA_DK)), 0.0),
                axis=1, keepdims=True)) for hd in heads])
        qg_b, kg_b, v_b = _bf(qg, kg, vu)
        att = [jnp.where(incl, _dot_nt(qg_b[:, ksl[hd]], kg_b[:, ksl[hd]]), 0.0) for hd in heads]
        o = [_dot(att[hd].astype(BF16), v_b[:, vsl[hd]]) for hd in heads]
        st = [[s_ref[b, hd] for hd in heads] for b in seqs]
        o_st = [[None] * GLA_HEADS for _ in seqs]
        for i in range(unit_seqs):
            for hd in heads:
                qi, si = _cast_small(tl, qg[blks[i], ksl[hd]], st[i][hd])
                o_st[i][hd] = _dot(qi, si)
        for i, b in enumerate(seqs):
            for hd in heads:
                kdi, vi = _cast_small(tl, kd[i][:, ksl[hd]], vu[blks[i], vsl[hd]])
                s_ref[b, hd] = ebl[i][hd] * st[i][hd] + _dot_tn(kdi, vi)
        o_units.append(jnp.concatenate(
            [o[hd] + jnp.concatenate([o_st[i][hd] for i in range(unit_seqs)], axis=0) for hd in heads], axis=1))
    o_all = jnp.concatenate(o_units, axis=0)

    parts = []
    for hd in range(GLA_HEADS):
        parts.append(_rms_rows(o_all[:, hd * GLA_DV:(hd + 1) * GLA_DV], ng_ref[...]))
    on = jnp.concatenate(parts, axis=-1)
    out = h + _mm(on * jax.nn.silu(gate), wout_ref[...])
    for i in range(bb):
        o_ref[:, i * D_MODEL:(i + 1) * D_MODEL] = out[i * tl:(i + 1) * tl]


def _gla_layer(h, s0, w, bb, tl):
    bsz, seq, _ = h.shape
    sblock = (bb, GLA_HEADS, GLA_DK, GLA_DV)
    hspec = pl.BlockSpec((bb, tl, D_MODEL), lambda b, l: (b, l, 0))
    return pl.pallas_call(
        _gla_kernel,
        grid=(bsz // bb, seq // tl),
        in_specs=[hspec, _state_spec(sblock, s0.shape[1] != bsz),
                  _const_spec((1, D_MODEL)),
                  _const_spec((D_MODEL, GLA_QK)), _const_spec((D_MODEL, GLA_QK)),
                  _const_spec((D_MODEL, GLA_V)), _const_spec((D_MODEL, GLA_V)),
                  _const_spec((D_MODEL, LANES)), _const_spec((LANES, GLA_QK)),
                  _const_spec((1, GLA_QK)), _const_spec((1, GLA_DV)),
                  _const_spec((GLA_V, D_MODEL))],
        out_specs=[pl.BlockSpec((tl, bb * D_MODEL), lambda b, l: (l, b)), _state_spec(sblock)],
        out_shape=[jax.ShapeDtypeStruct((seq, bsz * D_MODEL), F32), _state_shape(bsz, sblock)],
        compiler_params=_params(),
        name="gla_layer",
    )(h, s0, w["g"], w["wq"], w["wk"], w["wv"], w["wgate"], w["wglow"], w["wgk"], w["bgk"],
      w["ng"], w["w_out"])


def _s5_prep_kernel(logdt_ref, are_ref, aim_ref, bre_ref, bim_ref, cre_ref, cim_ref,
                    abr_ref, abi_ref, wb_ref, wc_ref):
    dt = jnp.exp(logdt_ref[...])
    lr = are_ref[...]
    li = aim_ref[...]
    mag = jnp.exp(lr * dt)
    abr = mag * jnp.cos(li * dt)
    abi = mag * jnp.sin(li * dt)
    den = lr * lr + li * li
    cr = ((abr - 1.0) * lr + abi * li) / den
    ci = (abi * lr - (abr - 1.0) * li) / den
    abr_ref[...] = abr
    abi_ref[...] = abi
    br = bre_ref[...]
    bi = bim_ref[...]
    bbr = (cr[:, None, :] * br - ci[:, None, :] * bi).astype(BF16)
    bbi = (cr[:, None, :] * bi + ci[:, None, :] * br).astype(BF16)
    c_re = cre_ref[...].astype(BF16)
    c_im = (-cim_ref[...]).astype(BF16)
    wb_ref[...] = jnp.zeros(wb_ref.shape, BF16)
    wc_ref[...] = jnp.zeros(wc_ref.shape, BF16)
    for j in range(S5_NB):
        for gl in range(S5_BUNDLE):
            g = j * S5_BUNDLE + gl
            ch = slice(gl * S5_GROUP, (gl + 1) * S5_GROUP)
            re = slice(gl * S5_N, (gl + 1) * S5_N)
            im = slice(S5_HALF + gl * S5_N, S5_HALF + (gl + 1) * S5_N)
            wb_ref[j, ch, re] = bbr[g]
            wb_ref[j, ch, im] = bbi[g]
            wc_ref[j, re, ch] = c_re[g]
            wc_ref[j, im, ch] = c_im[g]


def _s5_prep(log_dt, a_re, a_im, b_re, b_im, c_re, c_im):
    gn = jax.ShapeDtypeStruct((S5_G, S5_N), F32)
    in_w = S5_BUNDLE * S5_GROUP
    return pl.pallas_call(
        _s5_prep_kernel,
        out_shape=[gn, gn, jax.ShapeDtypeStruct((S5_NB, in_w, 2 * S5_HALF), BF16),
                   jax.ShapeDtypeStruct((S5_NB, 2 * S5_HALF, in_w), BF16)],
        name="s5_discretize")(
        log_dt.reshape(S5_G, 1), a_re, a_im, b_re.transpose(0, 2, 1), b_im.transpose(0, 2, 1),
        c_re.transpose(0, 2, 1), c_im.transpose(0, 2, 1))


def _s5_kernel(h_ref, h0_ref, g_ref, win_ref, wb_ref, wc_ref, abr_ref, abi_ref, dskip_ref,
               wglu_ref, bglu_ref, wout_ref, o_ref, hs_ref):
    tl, nb, _ = h_ref.shape
    rows = tl * nb
    ngrp = nb // SUBLANES

    @pl.when(pl.program_id(1) == 0)
    def _():
        hs_ref[...] = h0_ref[...]

    h = h_ref[...].reshape(rows, D_MODEL)
    xn = _rms_rows(h, g_ref[...]).astype(BF16)
    ug = _mm(xn, win_ref[...])
    u = ug[:, :D_MODEL]
    gate = ug[:, D_MODEL:]
    ub = u.astype(BF16)
    width = 2 * S5_HALF
    in_w = S5_BUNDLE * S5_GROUP

    def project_in(j):
        return jnp.dot(ub[:, j * in_w:(j + 1) * in_w], wb_ref[j], preferred_element_type=F32)

    parts = []
    bu_next = project_in(0)
    for j in range(S5_NB):
        bu = bu_next
        if j + 1 < S5_NB:
            bu_next = project_in(j + 1)
        re = slice(j * width, j * width + S5_HALF)
        im = slice(j * width + S5_HALF, (j + 1) * width)
        ar = jnp.broadcast_to(abr_ref[:, j * S5_HALF:(j + 1) * S5_HALF], (SUBLANES, S5_HALF))
        ai = jnp.broadcast_to(abi_ref[:, j * S5_HALF:(j + 1) * S5_HALF], (SUBLANES, S5_HALF))
        tiles = [None] * (tl * ngrp)
        for bg in range(ngrp):
            grp = slice(bg * SUBLANES, (bg + 1) * SUBLANES)
            hr = hs_ref[grp, re]
            hi = hs_ref[grp, im]
            for t in range(tl):
                r = t * nb + bg * SUBLANES
                hr, hi = (ar * hr - ai * hi + bu[r:r + SUBLANES, :S5_HALF],
                          ar * hi + ai * hr + bu[r:r + SUBLANES, S5_HALF:])
                tiles[t * ngrp + bg] = jnp.concatenate([hr, hi], axis=1)
            hs_ref[grp, re] = hr
            hs_ref[grp, im] = hi
        parts.append(_mm(jnp.concatenate(tiles, axis=0), wc_ref[j]))
    y = jnp.concatenate(parts, axis=-1) + dskip_ref[...] * u
    z = jax.nn.gelu(y)
    z = z * jax.nn.sigmoid(_mm(z, wglu_ref[...]) + bglu_ref[...])
    out = h + _mm(z * jax.nn.silu(gate), wout_ref[...])
    o_ref[...] = out.reshape(tl, nb, D_MODEL)


def _s5_layer(h_tb, h0, w, nb, tl):
    seq, bsz, _ = h_tb.shape
    hspec = pl.BlockSpec((tl, nb, D_MODEL), lambda b, l: (l, b, 0))
    sspec = pl.BlockSpec((nb, S5_STATE), lambda b, l: (b, 0))
    in_w = S5_BUNDLE * S5_GROUP
    return pl.pallas_call(
        _s5_kernel,
        grid=(bsz // nb, seq // tl),
        in_specs=[hspec, sspec, _const_spec((1, D_MODEL)), _const_spec((D_MODEL, 2 * D_MODEL)),
                  _const_spec((S5_NB, in_w, 2 * S5_HALF)), _const_spec((S5_NB, 2 * S5_HALF, in_w)),
                  _const_spec((1, S5_G * S5_N)), _const_spec((1, S5_G * S5_N)),
                  _const_spec((1, D_MODEL)), _const_spec((D_MODEL, D_MODEL)),
                  _const_spec((1, D_MODEL)), _const_spec((D_MODEL, D_MODEL))],
        out_specs=[hspec, sspec],
        out_shape=[jax.ShapeDtypeStruct(h_tb.shape, F32), jax.ShapeDtypeStruct((bsz, S5_STATE), F32)],
        compiler_params=_params(),
        name="s5_layer",
    )(h_tb, h0, w["g"], w["w_in"], w["wb"], w["wc"], w["abr"], w["abi"], w["d"], w["w_glu"],
      w["b_glu"], w["w_out"])


def _s5_pack_state(re, im):
    b = re.shape[0]
    return jnp.concatenate([re.reshape(b, S5_NB, S5_HALF), im.reshape(b, S5_NB, S5_HALF)],
                           axis=-1).reshape(b, S5_STATE)


def _s5_unpack_state(st):
    b = st.shape[0]
    st = st.reshape(b, S5_NB, 2 * S5_HALF)
    return (st[..., :S5_HALF].reshape(b, S5_G, S5_N), st[..., S5_HALF:].reshape(b, S5_G, S5_N))


def _gdn_kernel(nbu, h_ref, s0_ref, cb_ref, g_ref, wqkv_ref, wz_ref, wab_ref, convw_ref, alog_ref,
                dtb_ref, ng_ref, fg_ref, wout_ref, o_ref, s_ref, cbout_ref, ext_ref):
    bb, tl, _ = o_ref.shape
    rows = bb * tl
    l = pl.program_id(1)

    heads = range(GDN_HEADS)
    seqs = range(bb)
    slabs = range(GDN_QKV // LANES)
    pitch = GDN_PAD + tl
    hist = GDN_PAD - (GDN_CONV - 1)

    @pl.when(l == 0)
    def _():
        s_ref[...] = jnp.broadcast_to(s0_ref[...], s_ref.shape)
        for b in seqs:
            cb = cb_ref[b if cb_ref.shape[0] == bb else 0]
            for s in slabs:
                ext_ref[s, b * pitch + hist:b * pitch + GDN_PAD, :] = cb[:, s * LANES:(s + 1) * LANES]

    h = jnp.concatenate([h_ref[:, i * D_MODEL:(i + 1) * D_MODEL] for i in range(bb)], axis=0)
    xn = _rms_rows(h, g_ref[...]).astype(BF16)
    ab = _mm(xn, wab_ref[...])
    qkv = _mm(xn, wqkv_ref[...])
    z = _mm(xn, wz_ref[...])
    for b in seqs:
        for s in slabs:
            ext_ref[s, b * pitch + GDN_PAD:(b + 1) * pitch, :] = qkv[b * tl:(b + 1) * tl, s * LANES:(s + 1) * LANES]
    conv_rows = []
    for b in seqs:
        cols = []
        for s in slabs:
            lanes = slice(s * LANES, (s + 1) * LANES)
            acc = convw_ref[GDN_CONV - 1:GDN_CONV, lanes] * qkv[b * tl:(b + 1) * tl, lanes]
            for j in range(GDN_CONV - 1):
                acc = acc + convw_ref[j:j + 1, lanes] * ext_ref[s, pl.ds(b * pitch + hist + j, tl, stride=1), :]
            cols.append(acc)
        conv_rows.append(jnp.concatenate(cols, axis=1))
    act = jax.nn.silu(jnp.concatenate(conv_rows, axis=0))
    hsl = [slice(hd * GDN_DK, (hd + 1) * GDN_DK) for hd in heads]
    q_parts, k_parts = [], []
    for hd in heads:
        qh = act[:, hd * GDN_DK:(hd + 1) * GDN_DK]
        kh = act[:, GDN_QK + hd * GDN_DK:GDN_QK + (hd + 1) * GDN_DK]
        q_parts.append(qh * lax.rsqrt(jnp.sum(qh * qh, axis=-1, keepdims=True) + EPS) * GDN_DK ** -0.5)
        k_parts.append(kh * lax.rsqrt(jnp.sum(kh * kh, axis=-1, keepdims=True) + EPS))
    q = jnp.concatenate(q_parts, axis=1)
    k = jnp.concatenate(k_parts, axis=1)
    v = act[:, 2 * GDN_QK:]
    g = -jnp.exp(alog_ref[...]) * jax.nn.softplus(ab + dtb_ref[...])
    beta = jax.nn.sigmoid(ab)

    group = 4
    sup = nbu * tl
    width = group * sup
    n_units = bb // nbu
    n_sq = int(math.log2(tl)) - 1
    lg_sup = int(math.log2(sup))
    lg_tl = int(math.log2(tl))
    assert bb % nbu == 0 and 1 << lg_sup == sup and 1 << lg_tl == tl and n_sq >= 1
    cum_rows = min(rows, max(tl, MXU_ROWS))
    assert rows % cum_rows == 0
    tri = _block_masks(cum_rows, tl)[0].astype(F32).astype(BF16)
    t4 = lax.broadcasted_iota(jnp.int32, (sup, width), 0)
    col4 = lax.broadcasted_iota(jnp.int32, (sup, width), 1)
    s4 = col4 & (sup - 1)
    same4 = (t4 >> lg_tl) == (s4 >> lg_tl)
    incl4 = same4 & (t4 >= s4)
    strict4 = same4 & (t4 > s4)
    eye4 = (t4 == s4).astype(F32)
    bd_mask = ((lax.broadcasted_iota(jnp.int32, (width, width), 0) >> lg_sup)
               == (lax.broadcasted_iota(jnp.int32, (width, width), 1) >> lg_sup))
    head4 = col4 >> lg_sup

    def per_head_lanes(x, first):
        return jnp.concatenate(
            [jnp.broadcast_to(x[:, first + hd:first + hd + 1], (rows, GDN_DK)) for hd in heads], axis=1)

    gcum = jnp.concatenate([_exact_dot(g[r0:r0 + cum_rows], lambda p: _dot(tri, p))
                            for r0 in range(0, rows, cum_rows)], axis=0)
    gcx = per_head_lanes(gcum, 0)
    bx = per_head_lanes(beta, GDN_HEADS)
    egx = jnp.exp(gcx)
    kb = k * bx
    qe = q * egx
    q_b, k_b, kb_b = _bf(q, k, kb)
    rhs_v = v * bx
    rhs_k = kb * egx
    kend, egl = [], []
    for b in seqs:
        gl = gcx[(b + 1) * tl - 1:(b + 1) * tl, :]
        kend.append(k[b * tl:(b + 1) * tl] * jnp.exp(gl - gcx[b * tl:(b + 1) * tl]))
        egl.append(jnp.exp(gl))

    keys = [(u, hg) for u in range(n_units) for hg in range(GDN_HEADS // group)]
    m4, att4 = {}, {}
    for u, hg in keys:
        ru = slice(u * sup, (u + 1) * sup)
        lanes4 = slice(hg * group * GDN_DK, (hg + 1) * group * GDN_DK)
        kdiag = []
        for hh in range(group):
            pieces = [jnp.zeros((sup, GDN_DK), BF16)] * group
            pieces[hh] = k_b[ru, hsl[hg * group + hh]]
            kdiag.append(jnp.concatenate(pieces, axis=1))
        kdiag = jnp.concatenate(kdiag, axis=0)
        kk = _dot_nt(kb_b[ru, lanes4], kdiag)
        qk = _dot_nt(q_b[ru, lanes4], kdiag)
        gcol = None
        for hh in range(group):
            rep = gcx[ru, hsl[hg * group + hh]]
            rep = rep[:, :width] if width <= GDN_DK else jnp.concatenate([rep] * (width // GDN_DK), axis=1)
            gcol = rep if gcol is None else jnp.where(head4 == hh, rep, gcol)
        grow = jnp.sum(eye4 * gcol, axis=0, keepdims=True)
        decay = jnp.exp(gcol - grow)
        m4[u, hg] = -jnp.where(strict4, kk * decay, 0.0)
        att4[u, hg] = jnp.where(incl4, qk * decay, 0.0)

    def block_diag(m_b):
        return jnp.where(bd_mask, jnp.concatenate([m_b] * group, axis=0), jnp.zeros((), BF16))

    p4 = {key: eye4 + m4[key] for key in keys}
    m_b = {key: m4[key].astype(BF16) for key in keys}
    m4 = {key: _dot(m_b[key], block_diag(m_b[key])) for key in keys}
    for _ in range(1, n_sq):
        m_b = {key: m4[key].astype(BF16) for key in keys}
        x = {key: _dot(jnp.concatenate([p4[key].astype(BF16), m_b[key]], axis=0), block_diag(m_b[key]))
             for key in keys}
        p4 = {key: p4[key] + x[key][:sup] for key in keys}
        m4 = {key: x[key][sup:] for key in keys}
    m_b = {key: m4[key].astype(BF16) for key in keys}
    p4 = {key: p4[key] + _dot(p4[key].astype(BF16), block_diag(m_b[key])) for key in keys}

    uw = [[None] * GDN_HEADS for _ in range(n_units)]
    for u, hg in keys:
        ru = slice(u * sup, (u + 1) * sup)
        p_b = p4[u, hg].astype(BF16)
        for hh in range(group):
            hd = hg * group + hh
            rhs = jnp.concatenate([rhs_v[ru, hsl[hd]], rhs_k[ru, hsl[hd]]], axis=1)
            uw[u][hd] = _dot(p_b[:, hh * sup:(hh + 1) * sup], rhs.astype(BF16))

    st = [[s_ref[b, hd] for hd in heads] for b in seqs]
    v_new = [[None] * GDN_HEADS for _ in seqs]
    qs = [[None] * GDN_HEADS for _ in seqs]
    for b in seqs:
        u, i = divmod(b, nbu)
        for hd in heads:
            wq = jnp.concatenate([uw[u][hd][i * tl:(i + 1) * tl, GDN_DV:], qe[b * tl:(b + 1) * tl, hsl[hd]]],
                                 axis=0)
            ws = _dot(wq.astype(BF16), st[b][hd].astype(BF16))
            v_new[b][hd] = uw[u][hd][i * tl:(i + 1) * tl, :GDN_DV] - ws[:tl]
            qs[b][hd] = ws[tl:]
    o_units = []
    for u in range(n_units):
        o_heads = []
        for hd in heads:
            hg, hh = divmod(hd, group)
            members = range(u * nbu, (u + 1) * nbu)
            vn = jnp.concatenate([v_new[b][hd] for b in members], axis=0)
            att = att4[u, hg][:, hh * sup:(hh + 1) * sup]
            o_heads.append(jnp.concatenate([qs[b][hd] for b in members], axis=0)
                           + _dot(att.astype(BF16), vn.astype(BF16)))
        o_units.append(jnp.concatenate(o_heads, axis=1))
    for b in seqs:
        for hd in heads:
            ke, vn = _cast_small(tl, kend[b][:, hsl[hd]], v_new[b][hd])
            s_ref[b, hd] = egl[b][:, hsl[hd]] * st[b][hd] + _dot_tn(ke, vn)
    o_all = jnp.concatenate(o_units, axis=0)

    parts = []
    for hd in range(GDN_HEADS):
        parts.append(_rms_rows(o_all[:, hd * GDN_DV:(hd + 1) * GDN_DV], ng_ref[...]))
    on = jnp.concatenate(parts, axis=-1)
    out = h + _mm(on * jax.nn.silu(z), wout_ref[...])
    o_ref[...] = _rms_rows(out, fg_ref[...]).reshape(bb, tl, D_MODEL)

    @pl.when(l == pl.num_programs(1) - 1)
    def _():
        for b in seqs:
            cbout_ref[b] = jnp.concatenate(
                [ext_ref[s, b * pitch + tl + hist:(b + 1) * pitch, :] for s in slabs], axis=1)

    for b in seqs:
        for s in slabs:
            ext_ref[s, b * pitch:b * pitch + GDN_PAD, :] = ext_ref[s, b * pitch + tl:(b + 1) * pitch, :]


def _gdn_layer(h_tm, s0, cb, w, bb, tl, nbu):
    seq = h_tm.shape[0]
    bsz = h_tm.shape[1] // D_MODEL
    shared = s0.shape[1] != bsz
    sblock = (bb, GDN_HEADS, GDN_DK, GDN_DV)
    cblock = (bb, GDN_CONV - 1, GDN_QKV)
    hspec = pl.BlockSpec((bb, tl, D_MODEL), lambda b, l: (b, l, 0))
    return pl.pallas_call(
        functools.partial(_gdn_kernel, nbu),
        grid=(bsz // bb, seq // tl),
        in_specs=[pl.BlockSpec((tl, bb * D_MODEL), lambda b, l: (l, b)),
                  _state_spec(sblock, shared), _state_spec(cblock, shared),
                  _const_spec((1, D_MODEL)), _const_spec((D_MODEL, GDN_QKV)),
                  _const_spec((D_MODEL, GDN_V)), _const_spec((D_MODEL, LANES)),
                  _const_spec((GDN_CONV, GDN_QKV)), _const_spec((1, LANES)), _const_spec((1, LANES)),
                  _const_spec((1, GDN_DV)), _const_spec((1, D_MODEL)), _const_spec((GDN_V, D_MODEL))],
        out_specs=[hspec, _state_spec(sblock), _state_spec(cblock)],
        out_shape=[jax.ShapeDtypeStruct((bsz, seq, D_MODEL), F32), _state_shape(bsz, sblock),
                   _state_shape(bsz, cblock)],
        scratch_shapes=[pltpu.VMEM((GDN_QKV // LANES, bb * (GDN_PAD + tl), LANES), F32)],
        compiler_params=_params(),
        name="gdn_layer",
    )(h_tm, s0, cb, w["g"], w["wqkv"], w["wz"], w["wab"], w["conv_w"], w["a_log"], w["dt_bias"],
      w["ng"], w["fg"], w["w_out"])


_GLA_SPLITS = (GLA_QK, GLA_QK, GLA_V, GLA_V, GLA_RANK)
_GDN_SPLITS = (GDN_QKV, GDN_V, 2 * GDN_HEADS)
CAST_ROWS = 128


def _cast_kernel(n_in, splits, *refs):
    ins, outs = refs[:n_in], refs[n_in:]
    o = 0
    for ref, cols in zip(ins, splits):
        w = ref[...]
        lo = 0
        for width in cols:
            piece = w[:, lo:lo + width]
            if width % LANES:
                piece = jnp.concatenate(
                    [piece, jnp.zeros((piece.shape[0], LANES - width % LANES), piece.dtype)], axis=1)
            outs[o][...] = piece.astype(BF16)
            lo += width
            o += 1


def _cast_weights(weights, splits):
    k = weights[0].shape[1]
    out_widths = [wd + (-wd) % LANES for cols in splits for wd in cols]
    return pl.pallas_call(
        functools.partial(_cast_kernel, len(weights), splits),
        grid=(k // CAST_ROWS,),
        in_specs=[pl.BlockSpec((None, CAST_ROWS, w.shape[2]), lambda i: (0, i, 0)) for w in weights],
        out_specs=[pl.BlockSpec((CAST_ROWS, wd), lambda i: (i, 0)) for wd in out_widths],
        out_shape=[jax.ShapeDtypeStruct((k, wd), BF16) for wd in out_widths],
        compiler_params=pltpu.CompilerParams(dimension_semantics=("parallel",), vmem_limit_bytes=VMEM_LIMIT),
        name="cast_weights",
    )(*weights)


def _row(x, width=None):
    x = x.reshape(1, -1).astype(F32)
    if width is not None and x.shape[1] < width:
        x = jnp.pad(x, ((0, 0), (0, width - x.shape[1])))
    return x


def _s5_weights(j, norm_g, w_in, s5_b_re, s5_b_im, s5_c_re, s5_c_im, s5_d, s5_log_dt, s5_a_re,
                s5_a_im, w_glu, s5_b_glu, w_out):
    abr, abi, wb, wc = _s5_prep(s5_log_dt[j], s5_a_re[j], s5_a_im[j], s5_b_re[j], s5_b_im[j],
                                s5_c_re[j], s5_c_im[j])
    return dict(g=_row(norm_g), w_in=w_in, wb=wb, wc=wc, abr=_row(abr), abi=_row(abi), d=_row(s5_d[j]),
                w_glu=w_glu, b_glu=_row(s5_b_glu[j]), w_out=w_out)


def kernel(x_prompt, x_sample, state_pool, state_gla, state_s5_re, state_s5_im, state_gdn, state_gdn_conv, meta_tokens, norm_g, final_norm_g, pool_w_in, pool_w_grp, pool_scale, pool_w_out, gla_w_in, gla_w_gk, gla_b_gk, gla_norm_g, gla_w_out, s5_w_in, s5_b_re, s5_b_im, s5_c_re, s5_c_im, s5_d, s5_log_dt, s5_a_re, s5_a_im, s5_w_glu, s5_b_glu, s5_w_out, gdn_w_in, gdn_conv_w, gdn_a_log, gdn_dt_bias, gdn_norm_g, gdn_w_out):
    bp = x_prompt.shape[0]
    bs, ls, _ = x_sample.shape

    one = (D_MODEL,)
    (p_in, p_out, g_q, g_k, g_v, g_gate, g_low, g_out, s_in, s_glu, s_out, d_qkv, d_z, d_ab, d_out) = _cast_weights(
        [pool_w_in, pool_w_out, gla_w_in, gla_w_out, s5_w_in, s5_w_glu, s5_w_out, gdn_w_in, gdn_w_out],
        [(2 * D_MODEL,), one, _GLA_SPLITS, one, (2 * D_MODEL,), one, one, _GDN_SPLITS, one])
    wp = dict(g=_row(norm_g[0]), w_in=p_in, w_grp=pool_w_grp[0].astype(BF16),
              scale=_row(pool_scale[0]), w_out=p_out)
    wg = dict(g=_row(norm_g[1]), wq=g_q, wk=g_k, wv=g_v, wgate=g_gate, wglow=g_low,
              wgk=jnp.pad(gla_w_gk[0], ((0, LANES - GLA_RANK), (0, 0))).astype(BF16),
              bgk=_row(gla_b_gk[0]), ng=_row(gla_norm_g[0]), w_out=g_out)
    ws = _s5_weights(0, norm_g[2], s_in, s5_b_re, s5_b_im, s5_c_re, s5_c_im, s5_d, s5_log_dt,
                     s5_a_re, s5_a_im, s_glu, s5_b_glu, s_out)
    wd = dict(g=_row(norm_g[3]), wqkv=d_qkv, wz=d_z, wab=d_ab, conv_w=gdn_conv_w[0],
              a_log=_row(gdn_a_log[0], LANES), dt_bias=_row(gdn_dt_bias[0], LANES),
              ng=_row(gdn_norm_g[0]), fg=_row(final_norm_g), w_out=d_out)

    def run(h, pool_st, n_valid, gla_st, s5_st, gdn_st, conv_st, blocks):
        (pb, pt), (gb, gt), (sb, stl), (db, dtl, dn) = blocks
        bsz, seq, _ = h.shape
        h, pool_new = _pool_layer(h, pool_st, n_valid, wp, pb, pt)
        h_tm, gla_new = _gla_layer(h, gla_st, wg, gb, gt)
        h_tb, s5_new = _s5_layer(h_tm.reshape(seq, bsz, D_MODEL), s5_st, ws, sb, stl)
        y, gdn_new, conv_new = _gdn_layer(h_tb.reshape(seq, bsz * D_MODEL), gdn_st, conv_st,
                                          wd, db, dtl, dn)
        return y, pool_new, gla_new, s5_new, gdn_new, conv_new

    hm = meta_tokens.astype(F32)[None]
    zeros = lambda *s: jnp.zeros(s, F32)
    hm, m_pool = _pool_layer(hm, zeros(1, 1, POOL_BUF, D_MODEL), 0, wp, 1, N_META)
    hm, m_gla = _gla_layer(hm, zeros(1, 1, GLA_HEADS, GLA_DK, GLA_DV), wg, 1, N_META)
    hm_tb, m_s5 = _s5_layer(jnp.broadcast_to(hm[:, None, :], (N_META, SUBLANES, D_MODEL)),
                            zeros(SUBLANES, S5_STATE), ws, SUBLANES, N_META)
    _, m_gdn, m_conv = _gdn_layer(hm_tb[:, 0], zeros(1, 1, GDN_HEADS, GDN_DK, GDN_DV),
                                  zeros(1, 1, GDN_CONV - 1, GDN_QKV), wd, 1, N_META, 1)

    yp, pool_p, gla_p, s5_p, gdn_p, conv_p = run(
        x_prompt, m_pool, N_META, m_gla, jnp.broadcast_to(m_s5[0:1], (bp, S5_STATE)), m_gdn, m_conv,
        ((1, 512), (8, 64), (bp, 64), (8, 64, 1)))
    ys, pool_s, gla_s, s5_s, gdn_s, conv_s = run(
        x_sample, state_pool, POOL_BUF, state_gla, _s5_pack_state(state_s5_re[0], state_s5_im[0]),
        state_gdn, state_gdn_conv, ((32, ls), (16, ls), (32, ls), (8, ls, 8)))

    s5r_p, s5i_p = _s5_unpack_state(s5_p)
    s5r_s, s5i_s = _s5_unpack_state(s5_s)
    return (yp, ys, pool_p, pool_s, gla_p, gla_s, s5r_p[None], s5i_p[None], s5r_s[None], s5i_s[None],
            gdn_p, conv_p, gdn_s, conv_s)
```

```python
import functools
import math

import jax
import jax.numpy as jnp
from jax import lax
from jax.experimental import pallas as pl
from jax.experimental.pallas import tpu as pltpu

F32 = jnp.float32
BF16 = jnp.bfloat16
HIGHEST = lax.Precision.HIGHEST

D_MODEL = 1024
EPS = 1e-6
N_META = 16

POOL_WINDOWS = (2, 4, 8, 16)
POOL_GROUP = D_MODEL // len(POOL_WINDOWS)
POOL_BUF = max(POOL_WINDOWS) - 1
POOL_PAD = POOL_BUF + 1

GLA_HEADS = 4
GLA_DK = 128
GLA_DV = 256
GLA_QK = GLA_HEADS * GLA_DK
GLA_V = GLA_HEADS * GLA_DV
GLA_RANK = 16
GLA_GATE_NORM = 16.0

S5_GROUP = 16
S5_G = D_MODEL // S5_GROUP
S5_N = 64
S5_BUNDLE = 8
S5_NB = S5_G // S5_BUNDLE
S5_HALF = S5_BUNDLE * S5_N
S5_STATE = 2 * S5_G * S5_N

GDN_HEADS = 8
GDN_DK = 128
GDN_DV = 128
GDN_CONV = 4
GDN_QK = GDN_HEADS * GDN_DK
GDN_V = GDN_HEADS * GDN_DV
GDN_QKV = 2 * GDN_QK + GDN_V
GDN_PAD = 8

LANES = 128
SUBLANES = 8
MXU_ROWS = 256
VMEM_LIMIT = 58 * 1024 * 1024

_NT = (((1,), (1,)), ((), ()))
_TN = (((0,), (0,)), ((), ()))


def _rms_rows(x, g):
    return x * lax.rsqrt(jnp.mean(x * x, axis=-1, keepdims=True) + EPS) * g


def _mm(a, w):
    return jnp.dot(a.astype(BF16), w, preferred_element_type=F32)


def _cast_small(c, *xs):
    if c % 16 == 0:
        return tuple(x.astype(BF16) for x in xs)
    return xs


def _bf(*xs):
    return tuple(x.astype(BF16) for x in xs)


def _block_masks(n, c):
    r = lax.broadcasted_iota(jnp.int32, (n, n), 0)
    s = lax.broadcasted_iota(jnp.int32, (n, n), 1)
    sh = int(math.log2(c))
    same = (r >> sh) == (s >> sh)
    return same & (r >= s), same & (r > s), r == s


def _exact_dot(x, dot_piece):
    hi = x.astype(BF16)
    r = x - hi.astype(F32)
    mid = r.astype(BF16)
    lo = (r - mid.astype(F32)).astype(BF16)
    return dot_piece(hi) + dot_piece(mid) + dot_piece(lo)


def _dot(a, b):
    return jnp.dot(a, b, preferred_element_type=F32)


def _dot_nt(a, b):
    return lax.dot_general(a, b, _NT, preferred_element_type=F32)


def _dot_tn(a, b):
    return lax.dot_general(a, b, _TN, preferred_element_type=F32)


def _pool_kernel(n_valid, h_ref, buf_ref, g_ref, win_ref, wgrp_ref, scale_ref, wout_ref,
                 o_ref, st_ref, z_ref):
    bb, tl, _ = h_ref.shape
    rows = bb * tl
    l = pl.program_id(1)
    seqs = range(bb)
    slabs_per_group = POOL_GROUP // LANES
    pitch = POOL_PAD + tl

    @pl.when(l == 0)
    def _():
        for b in seqs:
            buf = buf_ref[b if buf_ref.shape[0] == bb else 0]
            for s in range(D_MODEL // LANES):
                z_ref[s, b * pitch + POOL_PAD - POOL_BUF:b * pitch + POOL_PAD, :] = buf[:, s * LANES:(s + 1) * LANES]

    h = h_ref[...].reshape(rows, D_MODEL)
    xn = _rms_rows(h, g_ref[...])
    xn = xn.astype(BF16)
    ug = _mm(xn, win_ref[:, :D_MODEL])
    gate = _mm(xn, win_ref[:, D_MODEL:])
    for b in seqs:
        for s in range(D_MODEL // LANES):
            z_ref[s, b * pitch + POOL_PAD:(b + 1) * pitch, :] = ug[b * tl:(b + 1) * tl, s * LANES:(s + 1) * LANES]

    t = (l * tl + lax.broadcasted_iota(jnp.int32, (tl, LANES), 0)).astype(F32)
    parts = []
    for gi, w in enumerate(POOL_WINDOWS):
        cnt = jnp.minimum(float(w), t + (1.0 + n_valid))
        mixed_rows = []
        for b in seqs:
            cols = []
            for s in range(gi * slabs_per_group, (gi + 1) * slabs_per_group):
                cur = ug[b * tl:(b + 1) * tl, s * LANES:(s + 1) * LANES]
                acc = cur
                for j in range(1, w):
                    acc = acc + z_ref[s, pl.ds(b * pitch + POOL_PAD - j, tl, stride=1), :]
                cols.append(acc / cnt - cur)
            mixed_rows.append(jnp.concatenate(cols, axis=1))
        parts.append(_mm(jnp.concatenate(mixed_rows, axis=0), wgrp_ref[gi]))
    mixed = jnp.concatenate(parts, axis=-1) * scale_ref[...]
    y = _mm(mixed * jax.nn.silu(gate), wout_ref[...])
    o_ref[...] = (h + y).reshape(bb, tl, D_MODEL)

    @pl.when(l == pl.num_programs(1) - 1)
    def _():
        for b in seqs:
            st_ref[b] = jnp.concatenate(
                [z_ref[s, b * pitch + tl + 1:(b + 1) * pitch, :] for s in range(D_MODEL // LANES)], axis=1)

    for b in seqs:
        for s in range(D_MODEL // LANES):
            z_ref[s, b * pitch:b * pitch + POOL_PAD, :] = z_ref[s, b * pitch + tl:(b + 1) * pitch, :]


def _const_spec(shape):
    nd = len(shape)
    return pl.BlockSpec(shape, lambda b, l: (0,) * nd, pipeline_mode=pl.Buffered(1))


def _state_spec(block, shared=False):
    nd = len(block)
    if shared:
        return pl.BlockSpec((None, 1) + tuple(block[1:]), lambda b, l: (0,) * (nd + 1))
    return pl.BlockSpec((None,) + tuple(block), lambda b, l: (0, b) + (0,) * (nd - 1))


def _reorders_rows(bb, tl, seq):
    return seq == tl and bb % SUBLANES == 0


def _state_shape(bsz, block):
    return jax.ShapeDtypeStruct((1, bsz) + tuple(block[1:]), F32)


def _params():
    return pltpu.CompilerParams(dimension_semantics=("parallel", "arbitrary"),
                                vmem_limit_bytes=VMEM_LIMIT)


def _pool_layer(h, buf, n_valid, w, bb, tl):
    bsz, seq, _ = h.shape
    sblock = (bb, POOL_BUF, D_MODEL)
    hspec = pl.BlockSpec((bb, tl, D_MODEL), lambda b, l: (b, l, 0))
    return pl.pallas_call(
        functools.partial(_pool_kernel, float(n_valid)),
        grid=(bsz // bb, seq // tl),
        in_specs=[hspec, _state_spec(sblock, buf.shape[1] != bsz),
                  _const_spec((1, D_MODEL)), _const_spec((D_MODEL, 2 * D_MODEL)),
                  _const_spec((len(POOL_WINDOWS), POOL_GROUP, POOL_GROUP)),
                  _const_spec((1, D_MODEL)), _const_spec((D_MODEL, D_MODEL))],
        out_specs=[hspec, _state_spec(sblock)],
        out_shape=[jax.ShapeDtypeStruct(h.shape, F32), _state_shape(bsz, sblock)],
        scratch_shapes=[pltpu.VMEM((D_MODEL // LANES, bb * (POOL_PAD + tl), LANES), F32)],
        compiler_params=_params(),
        name="pool_layer",
    )(h, buf, w["g"], w["w_in"], w["w_grp"], w["scale"], w["w_out"])


def _gla_kernel(h_ref, s0_ref, g_ref, wq_ref, wk_ref, wv_ref, wgate_ref, wglow_ref, wgk_ref,
                bgk_ref, ng_ref, wout_ref, o_ref, s_ref, *tm_ref):
    bb, tl, _ = h_ref.shape
    rows = bb * tl

    @pl.when(pl.program_id(1) == 0)
    def _():
        s_ref[...] = jnp.broadcast_to(s0_ref[...], s_ref.shape)

    h = h_ref[...].reshape(rows, D_MODEL)
    xn = _rms_rows(h, g_ref[...]).astype(BF16)
    glow = _mm(xn, wglow_ref[...])
    gk = jax.nn.log_sigmoid(_mm(glow, wgk_ref[...]) + bgk_ref[...]) / GLA_GATE_NORM
    q = _mm(xn, wq_ref[...]) * GLA_DK ** -0.5
    k = _mm(xn, wk_ref[...])
    v = _mm(xn, wv_ref[...])
    gate = _mm(xn, wgate_ref[...])

    unit_seqs = max(1, min(bb, MXU_ROWS // tl))
    assert bb % unit_seqs == 0
    n_unit = unit_seqs * tl
    incl, _, _ = _block_masks(n_unit, tl)
    tri = incl.astype(F32).astype(BF16)
    eye_k = (lax.broadcasted_iota(jnp.int32, (GLA_DK, GLA_DK), 0)
             == lax.broadcasted_iota(jnp.int32, (GLA_DK, GLA_DK), 1))
    heads = range(GLA_HEADS)
    ksl = [slice(hd * GLA_DK, (hd + 1) * GLA_DK) for hd in heads]
    vsl = [slice(hd * GLA_DV, (hd + 1) * GLA_DV) for hd in heads]
    blks = [slice(i * tl, (i + 1) * tl) for i in range(unit_seqs)]

    o_units = []
    for u in range(bb // unit_seqs):
        ru = slice(u * n_unit, (u + 1) * n_unit)
        seqs = range(u * unit_seqs, (u + 1) * unit_seqs)
        ku, vu = k[ru], v[ru]
        bc = _exact_dot(gk[ru], lambda p: _dot(tri, p))
        qg = q[ru] * jnp.exp(bc)
        kg = ku * jnp.exp(-bc)
        kd, ebl = [], []
        for i in range(unit_seqs):
            bl = bc[(i + 1) * tl - 1:(i + 1) * tl, :]
            kd.append(ku[blks[i]] * jnp.exp(bl - bc[blks[i]]))
            ebl.append([jnp.exp(jnp.sum(
                jnp.where(eye_k, jnp.broadcast_to(bl[:, ksl[hd]], (GLA_DK, GLA_DK)), 0.0),
                axis=1, keepdims=True)) for hd in heads])
        qg_b, kg_b, v_b = _bf(qg, kg, vu)
        att = [jnp.where(incl, _dot_nt(qg_b[:, ksl[hd]], kg_b[:, ksl[hd]]), 0.0) for hd in heads]
        o = [_dot(att[hd].astype(BF16), v_b[:, vsl[hd]]) for hd in heads]
        st = [[s_ref[b, hd] for hd in heads] for b in seqs]
        o_st = [[None] * GLA_HEADS for _ in seqs]
        for i in range(unit_seqs):
            for hd in heads:
                qi, si = _cast_small(tl, qg[blks[i], ksl[hd]], st[i][hd])
                o_st[i][hd] = _dot(qi, si)
        for i, b in enumerate(seqs):
            for hd in heads:
                kdi, vi = _cast_small(tl, kd[i][:, ksl[hd]], vu[blks[i], vsl[hd]])
                s_ref[b, hd] = ebl[i][hd] * st[i][hd] + _dot_tn(kdi, vi)
        o_units.append(jnp.concatenate(
            [o[hd] + jnp.concatenate([o_st[i][hd] for i in range(unit_seqs)], axis=0) for hd in heads], axis=1))
    o_all = jnp.concatenate(o_units, axis=0)

    parts = []
    for hd in range(GLA_HEADS):
        parts.append(_rms_rows(o_all[:, hd * GLA_DV:(hd + 1) * GLA_DV], ng_ref[...]))
    on = jnp.concatenate(parts, axis=-1)
    out = h + _mm(on * jax.nn.silu(gate), wout_ref[...])
    if not tm_ref:
        for i in range(bb):
            o_ref[:, i * D_MODEL:(i + 1) * D_MODEL] = out[i * tl:(i + 1) * tl]
    else:
        for s in range(D_MODEL // LANES):
            for i in range(bb):
                tm_ref[0][s, pl.ds(i, tl, stride=bb), :] = out[i * tl:(i + 1) * tl, s * LANES:(s + 1) * LANES]
            o_ref[:, :, s * LANES:(s + 1) * LANES] = tm_ref[0][s].reshape(tl, bb, LANES)


def _gla_layer(h, s0, w, bb, tl):
    bsz, seq, _ = h.shape
    sblock = (bb, GLA_HEADS, GLA_DK, GLA_DV)
    hspec = pl.BlockSpec((bb, tl, D_MODEL), lambda b, l: (b, l, 0))
    if _reorders_rows(bb, tl, seq):
        ospec = pl.BlockSpec((tl, bb, D_MODEL), lambda b, l: (l, b, 0))
        oshape = jax.ShapeDtypeStruct((seq, bsz, D_MODEL), F32)
        scratch = [pltpu.VMEM((D_MODEL // LANES, tl * bb, LANES), F32)]
    else:
        ospec = pl.BlockSpec((tl, bb * D_MODEL), lambda b, l: (l, b))
        oshape = jax.ShapeDtypeStruct((seq, bsz * D_MODEL), F32)
        scratch = []
    return pl.pallas_call(
        _gla_kernel,
        grid=(bsz // bb, seq // tl),
        in_specs=[hspec, _state_spec(sblock, s0.shape[1] != bsz),
                  _const_spec((1, D_MODEL)),
                  _const_spec((D_MODEL, GLA_QK)), _const_spec((D_MODEL, GLA_QK)),
                  _const_spec((D_MODEL, GLA_V)), _const_spec((D_MODEL, GLA_V)),
                  _const_spec((D_MODEL, LANES)), _const_spec((LANES, GLA_QK)),
                  _const_spec((1, GLA_QK)), _const_spec((1, GLA_DV)),
                  _const_spec((GLA_V, D_MODEL))],
        out_specs=[ospec, _state_spec(sblock)],
        out_shape=[oshape, _state_shape(bsz, sblock)],
        scratch_shapes=scratch,
        compiler_params=_params(),
        name="gla_layer",
    )(h, s0, w["g"], w["wq"], w["wk"], w["wv"], w["wgate"], w["wglow"], w["wgk"], w["bgk"],
      w["ng"], w["w_out"])


def _s5_prep_kernel(logdt_ref, are_ref, aim_ref, bre_ref, bim_ref, cre_ref, cim_ref,
                    abr_ref, abi_ref, wb_ref, wc_ref):
    dt = jnp.exp(logdt_ref[...])
    lr = are_ref[...]
    li = aim_ref[...]
    mag = jnp.exp(lr * dt)
    abr = mag * jnp.cos(li * dt)
    abi = mag * jnp.sin(li * dt)
    den = lr * lr + li * li
    cr = ((abr - 1.0) * lr + abi * li) / den
    ci = (abi * lr - (abr - 1.0) * li) / den
    abr_ref[...] = abr
    abi_ref[...] = abi
    br = bre_ref[...]
    bi = bim_ref[...]
    bbr = (cr[:, None, :] * br - ci[:, None, :] * bi).astype(BF16)
    bbi = (cr[:, None, :] * bi + ci[:, None, :] * br).astype(BF16)
    c_re = cre_ref[...].astype(BF16)
    c_im = (-cim_ref[...]).astype(BF16)
    wb_ref[...] = jnp.zeros(wb_ref.shape, BF16)
    wc_ref[...] = jnp.zeros(wc_ref.shape, BF16)
    for j in range(S5_NB):
        for gl in range(S5_BUNDLE):
            g = j * S5_BUNDLE + gl
            ch = slice(gl * S5_GROUP, (gl + 1) * S5_GROUP)
            re = slice(gl * S5_N, (gl + 1) * S5_N)
            im = slice(S5_HALF + gl * S5_N, S5_HALF + (gl + 1) * S5_N)
            wb_ref[j, ch, re] = bbr[g]
            wb_ref[j, ch, im] = bbi[g]
            wc_ref[j, re, ch] = c_re[g]
            wc_ref[j, im, ch] = c_im[g]


def _s5_prep(log_dt, a_re, a_im, b_re, b_im, c_re, c_im):
    gn = jax.ShapeDtypeStruct((S5_G, S5_N), F32)
    in_w = S5_BUNDLE * S5_GROUP
    return pl.pallas_call(
        _s5_prep_kernel,
        out_shape=[gn, gn, jax.ShapeDtypeStruct((S5_NB, in_w, 2 * S5_HALF), BF16),
                   jax.ShapeDtypeStruct((S5_NB, 2 * S5_HALF, in_w), BF16)],
        name="s5_discretize")(
        log_dt.reshape(S5_G, 1), a_re, a_im, b_re.transpose(0, 2, 1), b_im.transpose(0, 2, 1),
        c_re.transpose(0, 2, 1), c_im.transpose(0, 2, 1))


def _s5_kernel(h_ref, h0_ref, g_ref, win_ref, wb_ref, wc_ref, abr_ref, abi_ref, dskip_ref,
               wglu_ref, bglu_ref, wout_ref, o_ref, hs_ref):
    tl, nb, _ = h_ref.shape
    rows = tl * nb
    ngrp = nb // SUBLANES

    @pl.when(pl.program_id(1) == 0)
    def _():
        hs_ref[...] = h0_ref[...]

    h = h_ref[...].reshape(rows, D_MODEL)
    xn = _rms_rows(h, g_ref[...]).astype(BF16)
    ug = _mm(xn, win_ref[...])
    u = ug[:, :D_MODEL]
    gate = ug[:, D_MODEL:]
    ub = u.astype(BF16)
    width = 2 * S5_HALF
    in_w = S5_BUNDLE * S5_GROUP

    def project_in(j):
        return jnp.dot(ub[:, j * in_w:(j + 1) * in_w], wb_ref[j], preferred_element_type=F32)

    parts = []
    bu_next = project_in(0)
    for j in range(S5_NB):
        bu = bu_next
        if j + 1 < S5_NB:
            bu_next = project_in(j + 1)
        re = slice(j * width, j * width + S5_HALF)
        im = slice(j * width + S5_HALF, (j + 1) * width)
        ar = jnp.broadcast_to(abr_ref[:, j * S5_HALF:(j + 1) * S5_HALF], (SUBLANES, S5_HALF))
        ai = jnp.broadcast_to(abi_ref[:, j * S5_HALF:(j + 1) * S5_HALF], (SUBLANES, S5_HALF))
        tiles = [None] * (tl * ngrp)
        for bg in range(ngrp):
            grp = slice(bg * SUBLANES, (bg + 1) * SUBLANES)
            hr = hs_ref[grp, re]
            hi = hs_ref[grp, im]
            for t in range(tl):
                r = t * nb + bg * SUBLANES
                hr, hi = (ar * hr - ai * hi + bu[r:r + SUBLANES, :S5_HALF],
                          ar * hi + ai * hr + bu[r:r + SUBLANES, S5_HALF:])
                tiles[t * ngrp + bg] = jnp.concatenate([hr, hi], axis=1)
            hs_ref[grp, re] = hr
            hs_ref[grp, im] = hi
        parts.append(_mm(jnp.concatenate(tiles, axis=0), wc_ref[j]))
    y = jnp.concatenate(parts, axis=-1) + dskip_ref[...] * u
    z = jax.nn.gelu(y)
    z = z * jax.nn.sigmoid(_mm(z, wglu_ref[...]) + bglu_ref[...])
    out = h + _mm(z * jax.nn.silu(gate), wout_ref[...])
    o_ref[...] = out.reshape(tl, nb, D_MODEL)


def _s5_layer(h_tb, h0, w, nb, tl):
    seq, bsz, _ = h_tb.shape
    hspec = pl.BlockSpec((tl, nb, D_MODEL), lambda b, l: (l, b, 0))
    sspec = pl.BlockSpec((nb, S5_STATE), lambda b, l: (b, 0))
    in_w = S5_BUNDLE * S5_GROUP
    return pl.pallas_call(
        _s5_kernel,
        grid=(bsz // nb, seq // tl),
        in_specs=[hspec, sspec, _const_spec((1, D_MODEL)), _const_spec((D_MODEL, 2 * D_MODEL)),
                  _const_spec((S5_NB, in_w, 2 * S5_HALF)), _const_spec((S5_NB, 2 * S5_HALF, in_w)),
                  _const_spec((1, S5_G * S5_N)), _const_spec((1, S5_G * S5_N)),
                  _const_spec((1, D_MODEL)), _const_spec((D_MODEL, D_MODEL)),
                  _const_spec((1, D_MODEL)), _const_spec((D_MODEL, D_MODEL))],
        out_specs=[hspec, sspec],
        out_shape=[jax.ShapeDtypeStruct(h_tb.shape, F32), jax.ShapeDtypeStruct((bsz, S5_STATE), F32)],
        compiler_params=_params(),
        name="s5_layer",
    )(h_tb, h0, w["g"], w["w_in"], w["wb"], w["wc"], w["abr"], w["abi"], w["d"], w["w_glu"],
      w["b_glu"], w["w_out"])


def _s5_pack_state(re, im):
    b = re.shape[0]
    return jnp.concatenate([re.reshape(b, S5_NB, S5_HALF), im.reshape(b, S5_NB, S5_HALF)],
                           axis=-1).reshape(b, S5_STATE)


def _s5_unpack_state(st):
    b = st.shape[0]
    st = st.reshape(b, S5_NB, 2 * S5_HALF)
    return (st[..., :S5_HALF].reshape(b, S5_G, S5_N), st[..., S5_HALF:].reshape(b, S5_G, S5_N))


def _gdn_kernel(nbu, h_ref, s0_ref, cb_ref, g_ref, wqkv_ref, wz_ref, wab_ref, convw_ref, alog_ref,
                dtb_ref, ng_ref, fg_ref, wout_ref, o_ref, s_ref, cbout_ref, ext_ref, *tm_ref):
    bb, tl, _ = o_ref.shape
    rows = bb * tl
    l = pl.program_id(1)

    heads = range(GDN_HEADS)
    seqs = range(bb)
    slabs = range(GDN_QKV // LANES)
    pitch = GDN_PAD + tl
    hist = GDN_PAD - (GDN_CONV - 1)

    @pl.when(l == 0)
    def _():
        s_ref[...] = jnp.broadcast_to(s0_ref[...], s_ref.shape)
        for b in seqs:
            cb = cb_ref[b if cb_ref.shape[0] == bb else 0]
            for s in slabs:
                ext_ref[s, b * pitch + hist:b * pitch + GDN_PAD, :] = cb[:, s * LANES:(s + 1) * LANES]

    if not tm_ref:
        h = jnp.concatenate([h_ref[:, i * D_MODEL:(i + 1) * D_MODEL] for i in range(bb)], axis=0)
    else:
        for s in range(D_MODEL // LANES):
            tm_ref[0][s] = h_ref[:, :, s * LANES:(s + 1) * LANES].reshape(tl * bb, LANES)
        h = jnp.concatenate(
            [jnp.concatenate([tm_ref[0][s, pl.ds(i, tl, stride=bb), :] for s in range(D_MODEL // LANES)], axis=1)
             for i in range(bb)], axis=0)
    xn = _rms_rows(h, g_ref[...]).astype(BF16)
    ab = _mm(xn, wab_ref[...])
    qkv = _mm(xn, wqkv_ref[...])
    z = _mm(xn, wz_ref[...])
    for b in seqs:
        for s in slabs:
            ext_ref[s, b * pitch + GDN_PAD:(b + 1) * pitch, :] = qkv[b * tl:(b + 1) * tl, s * LANES:(s + 1) * LANES]
    conv_rows = []
    for b in seqs:
        cols = []
        for s in slabs:
            lanes = slice(s * LANES, (s + 1) * LANES)
            acc = convw_ref[GDN_CONV - 1:GDN_CONV, lanes] * qkv[b * tl:(b + 1) * tl, lanes]
            for j in range(GDN_CONV - 1):
                acc = acc + convw_ref[j:j + 1, lanes] * ext_ref[s, pl.ds(b * pitch + hist + j, tl, stride=1), :]
            cols.append(acc)
        conv_rows.append(jnp.concatenate(cols, axis=1))
    act = jax.nn.silu(jnp.concatenate(conv_rows, axis=0))
    hsl = [slice(hd * GDN_DK, (hd + 1) * GDN_DK) for hd in heads]
    q_parts, k_parts = [], []
    for hd in heads:
        qh = act[:, hd * GDN_DK:(hd + 1) * GDN_DK]
        kh = act[:, GDN_QK + hd * GDN_DK:GDN_QK + (hd + 1) * GDN_DK]
        q_parts.append(qh * lax.rsqrt(jnp.sum(qh * qh, axis=-1, keepdims=True) + EPS) * GDN_DK ** -0.5)
        k_parts.append(kh * lax.rsqrt(jnp.sum(kh * kh, axis=-1, keepdims=True) + EPS))
    q = jnp.concatenate(q_parts, axis=1)
    k = jnp.concatenate(k_parts, axis=1)
    v = act[:, 2 * GDN_QK:]
    g = -jnp.exp(alog_ref[...]) * jax.nn.softplus(ab + dtb_ref[...])
    beta = jax.nn.sigmoid(ab)

    group = 4
    sup = nbu * tl
    width = group * sup
    n_units = bb // nbu
    n_sq = int(math.log2(tl)) - 1
    lg_sup = int(math.log2(sup))
    lg_tl = int(math.log2(tl))
    assert bb % nbu == 0 and 1 << lg_sup == sup and 1 << lg_tl == tl and n_sq >= 1
    cum_rows = min(rows, max(tl, MXU_ROWS))
    assert rows % cum_rows == 0
    tri = _block_masks(cum_rows, tl)[0].astype(F32).astype(BF16)
    t4 = lax.broadcasted_iota(jnp.int32, (sup, width), 0)
    col4 = lax.broadcasted_iota(jnp.int32, (sup, width), 1)
    s4 = col4 & (sup - 1)
    same4 = (t4 >> lg_tl) == (s4 >> lg_tl)
    incl4 = same4 & (t4 >= s4)
    strict4 = same4 & (t4 > s4)
    eye4 = (t4 == s4).astype(F32)
    bd_mask = ((lax.broadcasted_iota(jnp.int32, (width, width), 0) >> lg_sup)
               == (lax.broadcasted_iota(jnp.int32, (width, width), 1) >> lg_sup))
    head4 = col4 >> lg_sup

    def per_head_lanes(x, first):
        return jnp.concatenate(
            [jnp.broadcast_to(x[:, first + hd:first + hd + 1], (rows, GDN_DK)) for hd in heads], axis=1)

    gcum = jnp.concatenate([_exact_dot(g[r0:r0 + cum_rows], lambda p: _dot(tri, p))
                            for r0 in range(0, rows, cum_rows)], axis=0)
    gcx = per_head_lanes(gcum, 0)
    bx = per_head_lanes(beta, GDN_HEADS)
    egx = jnp.exp(gcx)
    kb = k * bx
    qe = q * egx
    q_b, k_b, kb_b = _bf(q, k, kb)
    rhs_v = v * bx
    rhs_k = kb * egx
    kend, egl = [], []
    for b in seqs:
        gl = gcx[(b + 1) * tl - 1:(b + 1) * tl, :]
        kend.append(k[b * tl:(b + 1) * tl] * jnp.exp(gl - gcx[b * tl:(b + 1) * tl]))
        egl.append(jnp.exp(gl))

    keys = [(u, hg) for u in range(n_units) for hg in range(GDN_HEADS // group)]
    m4, att4 = {}, {}
    for u, hg in keys:
        ru = slice(u * sup, (u + 1) * sup)
        lanes4 = slice(hg * group * GDN_DK, (hg + 1) * group * GDN_DK)
        kdiag = []
        for hh in range(group):
            pieces = [jnp.zeros((sup, GDN_DK), BF16)] * group
            pieces[hh] = k_b[ru, hsl[hg * group + hh]]
            kdiag.append(jnp.concatenate(pieces, axis=1))
        kdiag = jnp.concatenate(kdiag, axis=0)
        kk = _dot_nt(kb_b[ru, lanes4], kdiag)
        qk = _dot_nt(q_b[ru, lanes4], kdiag)
        gcol = None
        for hh in range(group):
            rep = gcx[ru, hsl[hg * group + hh]]
            rep = rep[:, :width] if width <= GDN_DK else jnp.concatenate([rep] * (width // GDN_DK), axis=1)
            gcol = rep if gcol is None else jnp.where(head4 == hh, rep, gcol)
        grow = jnp.sum(eye4 * gcol, axis=0, keepdims=True)
        decay = jnp.exp(gcol - grow)
        m4[u, hg] = -jnp.where(strict4, kk * decay, 0.0)
        att4[u, hg] = jnp.where(incl4, qk * decay, 0.0)

    def block_diag(m_b):
        return jnp.where(bd_mask, jnp.concatenate([m_b] * group, axis=0), jnp.zeros((), BF16))

    p4 = {key: eye4 + m4[key] for key in keys}
    m_b = {key: m4[key].astype(BF16) for key in keys}
    m4 = {key: _dot(m_b[key], block_diag(m_b[key])) for key in keys}
    for _ in range(1, n_sq):
        m_b = {key: m4[key].astype(BF16) for key in keys}
        x = {key: _dot(jnp.concatenate([p4[key].astype(BF16), m_b[key]], axis=0), block_diag(m_b[key]))
             for key in keys}
        p4 = {key: p4[key] + x[key][:sup] for key in keys}
        m4 = {key: x[key][sup:] for key in keys}
    m_b = {key: m4[key].astype(BF16) for key in keys}
    p4 = {key: p4[key] + _dot(p4[key].astype(BF16), block_diag(m_b[key])) for key in keys}

    uw = [[None] * GDN_HEADS for _ in range(n_units)]
    for u, hg in keys:
        ru = slice(u * sup, (u + 1) * sup)
        p_b = p4[u, hg].astype(BF16)
        for hh in range(group):
            hd = hg * group + hh
            rhs = jnp.concatenate([rhs_v[ru, hsl[hd]], rhs_k[ru, hsl[hd]]], axis=1)
            uw[u][hd] = _dot(p_b[:, hh * sup:(hh + 1) * sup], rhs.astype(BF16))

    st = [[s_ref[b, hd] for hd in heads] for b in seqs]
    v_new = [[None] * GDN_HEADS for _ in seqs]
    qs = [[None] * GDN_HEADS for _ in seqs]
    for b in seqs:
        u, i = divmod(b, nbu)
        for hd in heads:
            wq = jnp.concatenate([uw[u][hd][i * tl:(i + 1) * tl, GDN_DV:], qe[b * tl:(b + 1) * tl, hsl[hd]]],
                                 axis=0)
            ws = _dot(wq.astype(BF16), st[b][hd].astype(BF16))
            v_new[b][hd] = uw[u][hd][i * tl:(i + 1) * tl, :GDN_DV] - ws[:tl]
            qs[b][hd] = ws[tl:]
    o_units = []
    for u in range(n_units):
        o_heads = []
        for hd in heads:
            hg, hh = divmod(hd, group)
            members = range(u * nbu, (u + 1) * nbu)
            vn = jnp.concatenate([v_new[b][hd] for b in members], axis=0)
            att = att4[u, hg][:, hh * sup:(hh + 1) * sup]
            o_heads.append(jnp.concatenate([qs[b][hd] for b in members], axis=0)
                           + _dot(att.astype(BF16), vn.astype(BF16)))
        o_units.append(jnp.concatenate(o_heads, axis=1))
    for b in seqs:
        for hd in heads:
            ke, vn = _cast_small(tl, kend[b][:, hsl[hd]], v_new[b][hd])
            s_ref[b, hd] = egl[b][:, hsl[hd]] * st[b][hd] + _dot_tn(ke, vn)
    o_all = jnp.concatenate(o_units, axis=0)

    parts = []
    for hd in range(GDN_HEADS):
        parts.append(_rms_rows(o_all[:, hd * GDN_DV:(hd + 1) * GDN_DV], ng_ref[...]))
    on = jnp.concatenate(parts, axis=-1)
    out = h + _mm(on * jax.nn.silu(z), wout_ref[...])
    o_ref[...] = _rms_rows(out, fg_ref[...]).reshape(bb, tl, D_MODEL)

    @pl.when(l == pl.num_programs(1) - 1)
    def _():
        for b in seqs:
            cbout_ref[b] = jnp.concatenate(
                [ext_ref[s, b * pitch + tl + hist:(b + 1) * pitch, :] for s in slabs], axis=1)

    for b in seqs:
        for s in slabs:
            ext_ref[s, b * pitch:b * pitch + GDN_PAD, :] = ext_ref[s, b * pitch + tl:(b + 1) * pitch, :]


def _gdn_layer(h_tm, s0, cb, w, bb, tl, nbu):
    seq = h_tm.shape[0]
    bsz = h_tm.size // (seq * D_MODEL)
    shared = s0.shape[1] != bsz
    sblock = (bb, GDN_HEADS, GDN_DK, GDN_DV)
    cblock = (bb, GDN_CONV - 1, GDN_QKV)
    hspec = pl.BlockSpec((bb, tl, D_MODEL), lambda b, l: (b, l, 0))
    scratch = [pltpu.VMEM((GDN_QKV // LANES, bb * (GDN_PAD + tl), LANES), F32)]
    if h_tm.ndim == 3:
        assert _reorders_rows(bb, tl, seq)
        in_spec = pl.BlockSpec((tl, bb, D_MODEL), lambda b, l: (l, b, 0))
        scratch.append(pltpu.VMEM((D_MODEL // LANES, tl * bb, LANES), F32))
    else:
        in_spec = pl.BlockSpec((tl, bb * D_MODEL), lambda b, l: (l, b))
    return pl.pallas_call(
        functools.partial(_gdn_kernel, nbu),
        grid=(bsz // bb, seq // tl),
        in_specs=[in_spec, _state_spec(sblock, shared), _state_spec(cblock, shared),
                  _const_spec((1, D_MODEL)), _const_spec((D_MODEL, GDN_QKV)),
                  _const_spec((D_MODEL, GDN_V)), _const_spec((D_MODEL, LANES)),
                  _const_spec((GDN_CONV, GDN_QKV)), _const_spec((1, LANES)), _const_spec((1, LANES)),
                  _const_spec((1, GDN_DV)), _const_spec((1, D_MODEL)), _const_spec((GDN_V, D_MODEL))],
        out_specs=[hspec, _state_spec(sblock), _state_spec(cblock)],
        out_shape=[jax.ShapeDtypeStruct((bsz, seq, D_MODEL), F32), _state_shape(bsz, sblock),
                   _state_shape(bsz, cblock)],
        scratch_shapes=scratch,
        compiler_params=_params(),
        name="gdn_layer",
    )(h_tm, s0, cb, w["g"], w["wqkv"], w["wz"], w["wab"], w["conv_w"], w["a_log"], w["dt_bias"],
      w["ng"], w["fg"], w["w_out"])


_GLA_SPLITS = (GLA_QK, GLA_QK, GLA_V, GLA_V, GLA_RANK)
_GDN_SPLITS = (GDN_QKV, GDN_V, 2 * GDN_HEADS)
CAST_ROWS = 128


def _cast_kernel(n_in, splits, *refs):
    ins, outs = refs[:n_in], refs[n_in:]
    o = 0
    for ref, cols in zip(ins, splits):
        w = ref[...]
        lo = 0
        for width in cols:
            piece = w[:, lo:lo + width]
            if width % LANES:
                piece = jnp.concatenate(
                    [piece, jnp.zeros((piece.shape[0], LANES - width % LANES), piece.dtype)], axis=1)
            outs[o][...] = piece.astype(BF16)
            lo += width
            o += 1


def _cast_weights(weights, splits):
    k = weights[0].shape[1]
    out_widths = [wd + (-wd) % LANES for cols in splits for wd in cols]
    return pl.pallas_call(
        functools.partial(_cast_kernel, len(weights), splits),
        grid=(k // CAST_ROWS,),
        in_specs=[pl.BlockSpec((None, CAST_ROWS, w.shape[2]), lambda i: (0, i, 0)) for w in weights],
        out_specs=[pl.BlockSpec((CAST_ROWS, wd), lambda i: (i, 0)) for wd in out_widths],
        out_shape=[jax.ShapeDtypeStruct((k, wd), BF16) for wd in out_widths],
        compiler_params=pltpu.CompilerParams(dimension_semantics=("parallel",), vmem_limit_bytes=VMEM_LIMIT),
        name="cast_weights",
    )(*weights)


def _row(x, width=None):
    x = x.reshape(1, -1).astype(F32)
    if width is not None and x.shape[1] < width:
        x = jnp.pad(x, ((0, 0), (0, width - x.shape[1])))
    return x


def _s5_weights(j, norm_g, w_in, s5_b_re, s5_b_im, s5_c_re, s5_c_im, s5_d, s5_log_dt, s5_a_re,
                s5_a_im, w_glu, s5_b_glu, w_out):
    abr, abi, wb, wc = _s5_prep(s5_log_dt[j], s5_a_re[j], s5_a_im[j], s5_b_re[j], s5_b_im[j],
                                s5_c_re[j], s5_c_im[j])
    return dict(g=_row(norm_g), w_in=w_in, wb=wb, wc=wc, abr=_row(abr), abi=_row(abi), d=_row(s5_d[j]),
                w_glu=w_glu, b_glu=_row(s5_b_glu[j]), w_out=w_out)


def kernel(x_prompt, x_sample, state_pool, state_gla, state_s5_re, state_s5_im, state_gdn, state_gdn_conv, meta_tokens, norm_g, final_norm_g, pool_w_in, pool_w_grp, pool_scale, pool_w_out, gla_w_in, gla_w_gk, gla_b_gk, gla_norm_g, gla_w_out, s5_w_in, s5_b_re, s5_b_im, s5_c_re, s5_c_im, s5_d, s5_log_dt, s5_a_re, s5_a_im, s5_w_glu, s5_b_glu, s5_w_out, gdn_w_in, gdn_conv_w, gdn_a_log, gdn_dt_bias, gdn_norm_g, gdn_w_out):
    bp = x_prompt.shape[0]
    bs, ls, _ = x_sample.shape

    one = (D_MODEL,)
    (p_in, p_out, g_q, g_k, g_v, g_gate, g_low, g_out, s_in, s_glu, s_out, d_qkv, d_z, d_ab, d_out) = _cast_weights(
        [pool_w_in, pool_w_out, gla_w_in, gla_w_out, s5_w_in, s5_w_glu, s5_w_out, gdn_w_in, gdn_w_out],
        [(2 * D_MODEL,), one, _GLA_SPLITS, one, (2 * D_MODEL,), one, one, _GDN_SPLITS, one])
    wp = dict(g=_row(norm_g[0]), w_in=p_in, w_grp=pool_w_grp[0].astype(BF16),
              scale=_row(pool_scale[0]), w_out=p_out)
    wg = dict(g=_row(norm_g[1]), wq=g_q, wk=g_k, wv=g_v, wgate=g_gate, wglow=g_low,
              wgk=jnp.pad(gla_w_gk[0], ((0, LANES - GLA_RANK), (0, 0))).astype(BF16),
              bgk=_row(gla_b_gk[0]), ng=_row(gla_norm_g[0]), w_out=g_out)
    ws = _s5_weights(0, norm_g[2], s_in, s5_b_re, s5_b_im, s5_c_re, s5_c_im, s5_d, s5_log_dt,
                     s5_a_re, s5_a_im, s_glu, s5_b_glu, s_out)
    wd = dict(g=_row(norm_g[3]), wqkv=d_qkv, wz=d_z, wab=d_ab, conv_w=gdn_conv_w[0],
              a_log=_row(gdn_a_log[0], LANES), dt_bias=_row(gdn_dt_bias[0], LANES),
              ng=_row(gdn_norm_g[0]), fg=_row(final_norm_g), w_out=d_out)

    def run(h, pool_st, n_valid, gla_st, s5_st, gdn_st, conv_st, blocks):
        (pb, pt), (gb, gt), (sb, stl), (db, dtl, dn) = blocks
        bsz, seq, _ = h.shape
        h, pool_new = _pool_layer(h, pool_st, n_valid, wp, pb, pt)
        h_tm, gla_new = _gla_layer(h, gla_st, wg, gb, gt)
        h_tb, s5_new = _s5_layer(h_tm.reshape(seq, bsz, D_MODEL), s5_st, ws, sb, stl)
        if not _reorders_rows(db, dtl, seq):
            h_tb = h_tb.reshape(seq, bsz * D_MODEL)
        y, gdn_new, conv_new = _gdn_layer(h_tb, gdn_st, conv_st, wd, db, dtl, dn)
        return y, pool_new, gla_new, s5_new, gdn_new, conv_new

    hm = meta_tokens.astype(F32)[None]
    zeros = lambda *s: jnp.zeros(s, F32)
    hm, m_pool = _pool_layer(hm, zeros(1, 1, POOL_BUF, D_MODEL), 0, wp, 1, N_META)
    hm, m_gla = _gla_layer(hm, zeros(1, 1, GLA_HEADS, GLA_DK, GLA_DV), wg, 1, N_META)
    hm_tb, m_s5 = _s5_layer(jnp.broadcast_to(hm[:, None, :], (N_META, SUBLANES, D_MODEL)),
                            zeros(SUBLANES, S5_STATE), ws, SUBLANES, N_META)
    _, m_gdn, m_conv = _gdn_layer(hm_tb[:, 0], zeros(1, 1, GDN_HEADS, GDN_DK, GDN_DV),
                                  zeros(1, 1, GDN_CONV - 1, GDN_QKV), wd, 1, N_META, 1)

    yp, pool_p, gla_p, s5_p, gdn_p, conv_p = run(
        x_prompt, m_pool, N_META, m_gla, jnp.broadcast_to(m_s5[0:1], (bp, S5_STATE)), m_gdn, m_conv,
        ((1, 512), (8, 64), (bp, 64), (8, 64, 1)))
    ys, pool_s, gla_s, s5_s, gdn_s, conv_s = run(
        x_sample, state_pool, POOL_BUF, state_gla, _s5_pack_state(state_s5_re[0], state_s5_im[0]),
        state_gdn, state_gdn_conv, ((32, ls), (16, ls), (32, ls), (16, ls, 8)))

    s5r_p, s5i_p = _s5_unpack_state(s5_p)
    s5r_s, s5i_s = _s5_unpack_state(s5_s)
    return (yp, ys, pool_p, pool_s, gla_p, gla_s, s5r_p[None], s5i_p[None], s5r_s[None], s5i_s[None],
            gdn_p, conv_p, gdn_s, conv_s)
```

```python
import functools
import math

import jax
import jax.numpy as jnp
from jax import lax
from jax.experimental import pallas as pl
from jax.experimental.pallas import tpu as pltpu

F32 = jnp.float32
BF16 = jnp.bfloat16
HIGHEST = lax.Precision.HIGHEST

D_MODEL = 1024
EPS = 1e-6
N_META = 16

POOL_WINDOWS = (2, 4, 8, 16)
POOL_GROUP = D_MODEL // len(POOL_WINDOWS)
POOL_BUF = max(POOL_WINDOWS) - 1
POOL_PAD = POOL_BUF + 1

GLA_HEADS = 4
GLA_DK = 128
GLA_DV = 256
GLA_QK = GLA_HEADS * GLA_DK
GLA_V = GLA_HEADS * GLA_DV
GLA_RANK = 16
GLA_GATE_NORM = 16.0

S5_GROUP = 16
S5_G = D_MODEL // S5_GROUP
S5_N = 64
S5_BUNDLE = 8
S5_NB = S5_G // S5_BUNDLE
S5_HALF = S5_BUNDLE * S5_N
S5_STATE = 2 * S5_G * S5_N

GDN_HEADS = 8
GDN_DK = 128
GDN_DV = 128
GDN_CONV = 4
GDN_QK = GDN_HEADS * GDN_DK
GDN_V = GDN_HEADS * GDN_DV
GDN_QKV = 2 * GDN_QK + GDN_V
GDN_PAD = 8

LANES = 128
SUBLANES = 8
MXU_ROWS = 256
VMEM_LIMIT = 58 * 1024 * 1024

_NT = (((1,), (1,)), ((), ()))
_TN = (((0,), (0,)), ((), ()))


def _rms_rows(x, g):
    return x * lax.rsqrt(jnp.mean(x * x, axis=-1, keepdims=True) + EPS) * g


def _mm(a, w):
    return jnp.dot(a.astype(BF16), w, preferred_element_type=F32)


def _cast_small(c, *xs):
    if c % 16 == 0:
        return tuple(x.astype(BF16) for x in xs)
    return xs


def _bf(*xs):
    return tuple(x.astype(BF16) for x in xs)


def _block_masks(n, c):
    r = lax.broadcasted_iota(jnp.int32, (n, n), 0)
    s = lax.broadcasted_iota(jnp.int32, (n, n), 1)
    sh = int(math.log2(c))
    same = (r >> sh) == (s >> sh)
    return same & (r >= s), same & (r > s), r == s


def _exact_dot(x, dot_piece):
    hi = x.astype(BF16)
    r = x - hi.astype(F32)
    mid = r.astype(BF16)
    lo = (r - mid.astype(F32)).astype(BF16)
    return dot_piece(hi) + dot_piece(mid) + dot_piece(lo)


def _dot(a, b):
    return jnp.dot(a, b, preferred_element_type=F32)


def _dot_nt(a, b):
    return lax.dot_general(a, b, _NT, preferred_element_type=F32)


def _dot_tn(a, b):
    return lax.dot_general(a, b, _TN, preferred_element_type=F32)


def _history_to_slabs(src_ref, dst_ref, first, pitch, bb):
    n, nb, w = src_ref.shape
    for j in range(n):
        row = src_ref[j]
        if nb != bb:
            row = jnp.broadcast_to(row, (bb, w))
        for s in range(w // LANES):
            piece = row[:, s * LANES:(s + 1) * LANES]
            if bb == 1:
                dst_ref[s, first + j:first + j + 1, :] = piece
            else:
                dst_ref[s, pl.ds(first + j, bb, stride=pitch), :] = piece


def _slabs_to_history(src_ref, dst_ref, first, pitch):
    n, bb, w = dst_ref.shape
    for j in range(n):
        if bb == 1:
            pieces = [src_ref[s, first + j:first + j + 1, :] for s in range(w // LANES)]
        else:
            pieces = [src_ref[s, pl.ds(first + j, bb, stride=pitch), :] for s in range(w // LANES)]
        dst_ref[j] = jnp.concatenate(pieces, axis=1)


def _pool_kernel(n_valid, time_major_out, h_ref, buf_ref, g_ref, win_ref, wgrp_ref, scale_ref, wout_ref,
                 o_ref, st_ref, z_ref):
    bb, tl, _ = h_ref.shape
    rows = bb * tl
    l = pl.program_id(1)
    seqs = range(bb)
    slabs_per_group = POOL_GROUP // LANES
    pitch = POOL_PAD + tl

    @pl.when(l == 0)
    def _():
        _history_to_slabs(buf_ref, z_ref, POOL_PAD - POOL_BUF, pitch, bb)

    h = h_ref[...].reshape(rows, D_MODEL)
    xn = _rms_rows(h, g_ref[...])
    xn = xn.astype(BF16)
    ug = _mm(xn, win_ref[:, :D_MODEL])
    gate = _mm(xn, win_ref[:, D_MODEL:])
    for b in seqs:
        for s in range(D_MODEL // LANES):
            z_ref[s, b * pitch + POOL_PAD:(b + 1) * pitch, :] = ug[b * tl:(b + 1) * tl, s * LANES:(s + 1) * LANES]

    t = (l * tl + lax.broadcasted_iota(jnp.int32, (tl, LANES), 0)).astype(F32)
    parts = []
    for gi, w in enumerate(POOL_WINDOWS):
        cnt = jnp.minimum(float(w), t + (1.0 + n_valid))
        mixed_rows = []
        for b in seqs:
            cols = []
            for s in range(gi * slabs_per_group, (gi + 1) * slabs_per_group):
                cur = ug[b * tl:(b + 1) * tl, s * LANES:(s + 1) * LANES]
                acc = cur
                for j in range(1, w):
                    acc = acc + z_ref[s, pl.ds(b * pitch + POOL_PAD - j, tl, stride=1), :]
                cols.append(acc / cnt - cur)
            mixed_rows.append(jnp.concatenate(cols, axis=1))
        parts.append(_mm(jnp.concatenate(mixed_rows, axis=0), wgrp_ref[gi]))
    mixed = jnp.concatenate(parts, axis=-1) * scale_ref[...]
    y = _mm(mixed * jax.nn.silu(gate), wout_ref[...])
    o_ref[...] = (h + y).reshape(bb, tl, D_MODEL)

    @pl.when(l == pl.num_programs(1) - 1)
    def _():
        if time_major_out:
            _slabs_to_history(z_ref, st_ref, tl + POOL_PAD - POOL_BUF, pitch)
        else:
            for b in seqs:
                st_ref[b] = jnp.concatenate(
                    [z_ref[s, b * pitch + tl + 1:(b + 1) * pitch, :] for s in range(D_MODEL // LANES)], axis=1)

    for b in seqs:
        for s in range(D_MODEL // LANES):
            z_ref[s, b * pitch:b * pitch + POOL_PAD, :] = z_ref[s, b * pitch + tl:(b + 1) * pitch, :]


def _const_spec(shape):
    nd = len(shape)
    return pl.BlockSpec(shape, lambda b, l: (0,) * nd, pipeline_mode=pl.Buffered(1))


def _state_spec(block, shared=False):
    nd = len(block)
    if shared:
        return pl.BlockSpec((None, 1) + tuple(block[1:]), lambda b, l: (0,) * (nd + 1))
    return pl.BlockSpec((None,) + tuple(block), lambda b, l: (0, b) + (0,) * (nd - 1))


def _time_major(st):
    return jnp.transpose(st, (0, 2, 1, 3))


def _history_spec(n, bb, width, shared=False):
    if shared:
        return pl.BlockSpec((None, n, 1, width), lambda b, l: (0, 0, 0, 0))
    return pl.BlockSpec((None, n, bb, width), lambda b, l: (0, 0, b, 0))


def _reorders_rows(bb, tl, seq):
    return seq == tl and bb % SUBLANES == 0


def _state_shape(bsz, block):
    return jax.ShapeDtypeStruct((1, bsz) + tuple(block[1:]), F32)


def _params():
    return pltpu.CompilerParams(dimension_semantics=("parallel", "arbitrary"),
                                vmem_limit_bytes=VMEM_LIMIT)


def _pool_layer(h, buf, n_valid, w, bb, tl):
    bsz, seq, _ = h.shape
    hspec = pl.BlockSpec((bb, tl, D_MODEL), lambda b, l: (b, l, 0))
    tm_out = bb % SUBLANES == 0 or bb == bsz
    if tm_out:
        st_spec = _history_spec(POOL_BUF, bb, D_MODEL)
        st_shape = jax.ShapeDtypeStruct((1, POOL_BUF, bsz, D_MODEL), F32)
    else:
        st_spec = _state_spec((bb, POOL_BUF, D_MODEL))
        st_shape = _state_shape(bsz, (bb, POOL_BUF, D_MODEL))
    out, st = pl.pallas_call(
        functools.partial(_pool_kernel, float(n_valid), tm_out),
        grid=(bsz // bb, seq // tl),
        in_specs=[hspec, _history_spec(POOL_BUF, bb, D_MODEL, buf.shape[1] != bsz),
                  _const_spec((1, D_MODEL)), _const_spec((D_MODEL, 2 * D_MODEL)),
                  _const_spec((len(POOL_WINDOWS), POOL_GROUP, POOL_GROUP)),
                  _const_spec((1, D_MODEL)), _const_spec((D_MODEL, D_MODEL))],
        out_specs=[hspec, st_spec],
        out_shape=[jax.ShapeDtypeStruct(h.shape, F32), st_shape],
        scratch_shapes=[pltpu.VMEM((D_MODEL // LANES, bb * (POOL_PAD + tl), LANES), F32)],
        compiler_params=_params(),
        name="pool_layer",
    )(h, _time_major(buf), w["g"], w["w_in"], w["w_grp"], w["scale"], w["w_out"])
    return out, (_time_major(st) if tm_out else st)


def _gla_kernel(h_ref, s0_ref, g_ref, wq_ref, wk_ref, wv_ref, wgate_ref, wglow_ref, wgk_ref,
                bgk_ref, ng_ref, wout_ref, o_ref, s_ref, *tm_ref):
    bb, tl, _ = h_ref.shape
    rows = bb * tl

    @pl.when(pl.program_id(1) == 0)
    def _():
        s_ref[...] = jnp.broadcast_to(s0_ref[...], s_ref.shape)

    h = h_ref[...].reshape(rows, D_MODEL)
    xn = _rms_rows(h, g_ref[...]).astype(BF16)
    glow = _mm(xn, wglow_ref[...])
    gk = jax.nn.log_sigmoid(_mm(glow, wgk_ref[...]) + bgk_ref[...]) / GLA_GATE_NORM
    q = _mm(xn, wq_ref[...]) * GLA_DK ** -0.5
    k = _mm(xn, wk_ref[...])
    v = _mm(xn, wv_ref[...])
    gate = _mm(xn, wgate_ref[...])

    unit_seqs = max(1, min(bb, MXU_ROWS // tl))
    assert bb % unit_seqs == 0
    n_unit = unit_seqs * tl
    incl, _, _ = _block_masks(n_unit, tl)
    tri = incl.astype(F32).astype(BF16)
    eye_k = (lax.broadcasted_iota(jnp.int32, (GLA_DK, GLA_DK), 0)
             == lax.broadcasted_iota(jnp.int32, (GLA_DK, GLA_DK), 1))
    heads = range(GLA_HEADS)
    ksl = [slice(hd * GLA_DK, (hd + 1) * GLA_DK) for hd in heads]
    vsl = [slice(hd * GLA_DV, (hd + 1) * GLA_DV) for hd in heads]
    blks = [slice(i * tl, (i + 1) * tl) for i in range(unit_seqs)]

    o_units = []
    for u in range(bb // unit_seqs):
        ru = slice(u * n_unit, (u + 1) * n_unit)
        seqs = range(u * unit_seqs, (u + 1) * unit_seqs)
        ku, vu = k[ru], v[ru]
        bc = _exact_dot(gk[ru], lambda p: _dot(tri, p))
        qg = q[ru] * jnp.exp(bc)
        kg = ku * jnp.exp(-bc)
        kd, ebl = [], []
        for i in range(unit_seqs):
            bl = bc[(i + 1) * tl - 1:(i + 1) * tl, :]
            kd.append(ku[blks[i]] * jnp.exp(bl - bc[blks[i]]))
            ebl.append([jnp.exp(jnp.sum(
                jnp.where(eye_k, jnp.broadcast_to(bl[:, ksl[hd]], (GLA_DK, GLA_DK)), 0.0),
                axis=1, keepdims=True)) for hd in heads])
        qg_b, kg_b, v_b = _bf(qg, kg, vu)
        att = [jnp.where(incl, _dot_nt(qg_b[:, ksl[hd]], kg_b[:, ksl[hd]]), 0.0) for hd in heads]
        o = [_dot(att[hd].astype(BF16), v_b[:, vsl[hd]]) for hd in heads]
        st = [[s_ref[b, hd] for hd in heads] for b in seqs]
        o_st = [[None] * GLA_HEADS for _ in seqs]
        for i in range(unit_seqs):
            for hd in heads:
                qi, si = _cast_small(tl, qg[blks[i], ksl[hd]], st[i][hd])
                o_st[i][hd] = _dot(qi, si)
        for i, b in enumerate(seqs):
            for hd in heads:
                kdi, vi = _cast_small(tl, kd[i][:, ksl[hd]], vu[blks[i], vsl[hd]])
                s_ref[b, hd] = ebl[i][hd] * st[i][hd] + _dot_tn(kdi, vi)
        o_units.append(jnp.concatenate(
            [o[hd] + jnp.concatenate([o_st[i][hd] for i in range(unit_seqs)], axis=0) for hd in heads], axis=1))
    o_all = jnp.concatenate(o_units, axis=0)

    parts = []
    for hd in range(GLA_HEADS):
        parts.append(_rms_rows(o_all[:, hd * GLA_DV:(hd + 1) * GLA_DV], ng_ref[...]))
    on = jnp.concatenate(parts, axis=-1)
    out = h + _mm(on * jax.nn.silu(gate), wout_ref[...])
    if not tm_ref:
        for i in range(bb):
            o_ref[:, i * D_MODEL:(i + 1) * D_MODEL] = out[i * tl:(i + 1) * tl]
    else:
        for s in range(D_MODEL // LANES):
            for i in range(bb):
                tm_ref[0][s, pl.ds(i, tl, stride=bb), :] = out[i * tl:(i + 1) * tl, s * LANES:(s + 1) * LANES]
            o_ref[:, :, s * LANES:(s + 1) * LANES] = tm_ref[0][s].reshape(tl, bb, LANES)


def _gla_layer(h, s0, w, bb, tl):
    bsz, seq, _ = h.shape
    sblock = (bb, GLA_HEADS, GLA_DK, GLA_DV)
    hspec = pl.BlockSpec((bb, tl, D_MODEL), lambda b, l: (b, l, 0))
    if _reorders_rows(bb, tl, seq):
        ospec = pl.BlockSpec((tl, bb, D_MODEL), lambda b, l: (l, b, 0))
        oshape = jax.ShapeDtypeStruct((seq, bsz, D_MODEL), F32)
        scratch = [pltpu.VMEM((D_MODEL // LANES, tl * bb, LANES), F32)]
    else:
        ospec = pl.BlockSpec((tl, bb * D_MODEL), lambda b, l: (l, b))
        oshape = jax.ShapeDtypeStruct((seq, bsz * D_MODEL), F32)
        scratch = []
    return pl.pallas_call(
        _gla_kernel,
        grid=(bsz // bb, seq // tl),
        in_specs=[hspec, _state_spec(sblock, s0.shape[1] != bsz),
                  _const_spec((1, D_MODEL)),
                  _const_spec((D_MODEL, GLA_QK)), _const_spec((D_MODEL, GLA_QK)),
                  _const_spec((D_MODEL, GLA_V)), _const_spec((D_MODEL, GLA_V)),
                  _const_spec((D_MODEL, LANES)), _const_spec((LANES, GLA_QK)),
                  _const_spec((1, GLA_QK)), _const_spec((1, GLA_DV)),
                  _const_spec((GLA_V, D_MODEL))],
        out_specs=[ospec, _state_spec(sblock)],
        out_shape=[oshape, _state_shape(bsz, sblock)],
        scratch_shapes=scratch,
        compiler_params=_params(),
        name="gla_layer",
    )(h, s0, w["g"], w["wq"], w["wk"], w["wv"], w["wgate"], w["wglow"], w["wgk"], w["bgk"],
      w["ng"], w["w_out"])


def _s5_prep_kernel(logdt_ref, are_ref, aim_ref, bre_ref, bim_ref, cre_ref, cim_ref,
                    abr_ref, abi_ref, wb_ref, wc_ref):
    dt = jnp.exp(logdt_ref[...])
    lr = are_ref[...]
    li = aim_ref[...]
    mag = jnp.exp(lr * dt)
    abr = mag * jnp.cos(li * dt)
    abi = mag * jnp.sin(li * dt)
    den = lr * lr + li * li
    cr = ((abr - 1.0) * lr + abi * li) / den
    ci = (abi * lr - (abr - 1.0) * li) / den
    abr_ref[...] = abr
    abi_ref[...] = abi
    br = bre_ref[...]
    bi = bim_ref[...]
    bbr = (cr[:, None, :] * br - ci[:, None, :] * bi).astype(BF16)
    bbi = (cr[:, None, :] * bi + ci[:, None, :] * br).astype(BF16)
    c_re = cre_ref[...].astype(BF16)
    c_im = (-cim_ref[...]).astype(BF16)
    wb_ref[...] = jnp.zeros(wb_ref.shape, BF16)
    wc_ref[...] = jnp.zeros(wc_ref.shape, BF16)
    for j in range(S5_NB):
        for gl in range(S5_BUNDLE):
            g = j * S5_BUNDLE + gl
            ch = slice(gl * S5_GROUP, (gl + 1) * S5_GROUP)
            re = slice(gl * S5_N, (gl + 1) * S5_N)
            im = slice(S5_HALF + gl * S5_N, S5_HALF + (gl + 1) * S5_N)
            wb_ref[j, ch, re] = bbr[g]
            wb_ref[j, ch, im] = bbi[g]
            wc_ref[j, re, ch] = c_re[g]
            wc_ref[j, im, ch] = c_im[g]


def _s5_prep(log_dt, a_re, a_im, b_re, b_im, c_re, c_im):
    gn = jax.ShapeDtypeStruct((S5_G, S5_N), F32)
    in_w = S5_BUNDLE * S5_GROUP
    return pl.pallas_call(
        _s5_prep_kernel,
        out_shape=[gn, gn, jax.ShapeDtypeStruct((S5_NB, in_w, 2 * S5_HALF), BF16),
                   jax.ShapeDtypeStruct((S5_NB, 2 * S5_HALF, in_w), BF16)],
        name="s5_discretize")(
        log_dt.reshape(S5_G, 1), a_re, a_im, b_re.transpose(0, 2, 1), b_im.transpose(0, 2, 1),
        c_re.transpose(0, 2, 1), c_im.transpose(0, 2, 1))


def _s5_kernel(h_ref, h0_ref, g_ref, win_ref, wb_ref, wc_ref, abr_ref, abi_ref, dskip_ref,
               wglu_ref, bglu_ref, wout_ref, o_ref, hs_ref):
    tl, nb, _ = h_ref.shape
    rows = tl * nb
    ngrp = nb // SUBLANES

    @pl.when(pl.program_id(1) == 0)
    def _():
        hs_ref[...] = h0_ref[...]

    h = h_ref[...].reshape(rows, D_MODEL)
    xn = _rms_rows(h, g_ref[...]).astype(BF16)
    ug = _mm(xn, win_ref[...])
    u = ug[:, :D_MODEL]
    gate = ug[:, D_MODEL:]
    ub = u.astype(BF16)
    width = 2 * S5_HALF
    in_w = S5_BUNDLE * S5_GROUP

    def project_in(j):
        return jnp.dot(ub[:, j * in_w:(j + 1) * in_w], wb_ref[j], preferred_element_type=F32)

    parts = []
    bu_next = project_in(0)
    for j in range(S5_NB):
        bu = bu_next
        if j + 1 < S5_NB:
            bu_next = project_in(j + 1)
        re = slice(j * width, j * width + S5_HALF)
        im = slice(j * width + S5_HALF, (j + 1) * width)
        ar = jnp.broadcast_to(abr_ref[:, j * S5_HALF:(j + 1) * S5_HALF], (SUBLANES, S5_HALF))
        ai = jnp.broadcast_to(abi_ref[:, j * S5_HALF:(j + 1) * S5_HALF], (SUBLANES, S5_HALF))
        tiles = [None] * (tl * ngrp)
        for bg in range(ngrp):
            grp = slice(bg * SUBLANES, (bg + 1) * SUBLANES)
            hr = hs_ref[grp, re]
            hi = hs_ref[grp, im]
            for t in range(tl):
                r = t * nb + bg * SUBLANES
                hr, hi = (ar * hr - ai * hi + bu[r:r + SUBLANES, :S5_HALF],
                          ar * hi + ai * hr + bu[r:r + SUBLANES, S5_HALF:])
                tiles[t * ngrp + bg] = jnp.concatenate([hr, hi], axis=1)
            hs_ref[grp, re] = hr
            hs_ref[grp, im] = hi
        parts.append(_mm(jnp.concatenate(tiles, axis=0), wc_ref[j]))
    y = jnp.concatenate(parts, axis=-1) + dskip_ref[...] * u
    z = jax.nn.gelu(y)
    z = z * jax.nn.sigmoid(_mm(z, wglu_ref[...]) + bglu_ref[...])
    out = h + _mm(z * jax.nn.silu(gate), wout_ref[...])
    o_ref[...] = out.reshape(tl, nb, D_MODEL)


def _s5_layer(h_tb, h0, w, nb, tl):
    seq, bsz, _ = h_tb.shape
    hspec = pl.BlockSpec((tl, nb, D_MODEL), lambda b, l: (l, b, 0))
    sspec = pl.BlockSpec((nb, S5_STATE), lambda b, l: (b, 0))
    in_w = S5_BUNDLE * S5_GROUP
    return pl.pallas_call(
        _s5_kernel,
        grid=(bsz // nb, seq // tl),
        in_specs=[hspec, sspec, _const_spec((1, D_MODEL)), _const_spec((D_MODEL, 2 * D_MODEL)),
                  _const_spec((S5_NB, in_w, 2 * S5_HALF)), _const_spec((S5_NB, 2 * S5_HALF, in_w)),
                  _const_spec((1, S5_G * S5_N)), _const_spec((1, S5_G * S5_N)),
                  _const_spec((1, D_MODEL)), _const_spec((D_MODEL, D_MODEL)),
                  _const_spec((1, D_MODEL)), _const_spec((D_MODEL, D_MODEL))],
        out_specs=[hspec, sspec],
        out_shape=[jax.ShapeDtypeStruct(h_tb.shape, F32), jax.ShapeDtypeStruct((bsz, S5_STATE), F32)],
        compiler_params=_params(),
        name="s5_layer",
    )(h_tb, h0, w["g"], w["w_in"], w["wb"], w["wc"], w["abr"], w["abi"], w["d"], w["w_glu"],
      w["b_glu"], w["w_out"])


def _s5_pack_state(re, im):
    b = re.shape[0]
    return jnp.concatenate([re.reshape(b, S5_NB, S5_HALF), im.reshape(b, S5_NB, S5_HALF)],
                           axis=-1).reshape(b, S5_STATE)


def _s5_unpack_state(st):
    b = st.shape[0]
    st = st.reshape(b, S5_NB, 2 * S5_HALF)
    return (st[..., :S5_HALF].reshape(b, S5_G, S5_N), st[..., S5_HALF:].reshape(b, S5_G, S5_N))


def _gdn_kernel(nbu, h_ref, s0_ref, cb_ref, g_ref, wqkv_ref, wz_ref, wab_ref, convw_ref, alog_ref,
                dtb_ref, ng_ref, fg_ref, wout_ref, o_ref, s_ref, cbout_ref, ext_ref, *tm_ref):
    bb, tl, _ = o_ref.shape
    rows = bb * tl
    l = pl.program_id(1)

    heads = range(GDN_HEADS)
    seqs = range(bb)
    slabs = range(GDN_QKV // LANES)
    pitch = GDN_PAD + tl
    hist = GDN_PAD - (GDN_CONV - 1)

    @pl.when(l == 0)
    def _():
        s_ref[...] = jnp.broadcast_to(s0_ref[...], s_ref.shape)
        _history_to_slabs(cb_ref, ext_ref, hist, pitch, bb)

    if not tm_ref:
        h = jnp.concatenate([h_ref[:, i * D_MODEL:(i + 1) * D_MODEL] for i in range(bb)], axis=0)
    else:
        for s in range(D_MODEL // LANES):
            tm_ref[0][s] = h_ref[:, :, s * LANES:(s + 1) * LANES].reshape(tl * bb, LANES)
        h = jnp.concatenate(
            [jnp.concatenate([tm_ref[0][s, pl.ds(i, tl, stride=bb), :] for s in range(D_MODEL // LANES)], axis=1)
             for i in range(bb)], axis=0)
    xn = _rms_rows(h, g_ref[...]).astype(BF16)
    ab = _mm(xn, wab_ref[...])
    qkv = _mm(xn, wqkv_ref[...])
    z = _mm(xn, wz_ref[...])
    for b in seqs:
        for s in slabs:
            ext_ref[s, b * pitch + GDN_PAD:(b + 1) * pitch, :] = qkv[b * tl:(b + 1) * tl, s * LANES:(s + 1) * LANES]
    conv_rows = []
    for b in seqs:
        cols = []
        for s in slabs:
            lanes = slice(s * LANES, (s + 1) * LANES)
            acc = convw_ref[GDN_CONV - 1:GDN_CONV, lanes] * qkv[b * tl:(b + 1) * tl, lanes]
            for j in range(GDN_CONV - 1):
                acc = acc + convw_ref[j:j + 1, lanes] * ext_ref[s, pl.ds(b * pitch + hist + j, tl, stride=1), :]
            cols.append(acc)
        conv_rows.append(jnp.concatenate(cols, axis=1))
    act = jax.nn.silu(jnp.concatenate(conv_rows, axis=0))
    hsl = [slice(hd * GDN_DK, (hd + 1) * GDN_DK) for hd in heads]
    q_parts, k_parts = [], []
    for hd in heads:
        qh = act[:, hd * GDN_DK:(hd + 1) * GDN_DK]
        kh = act[:, GDN_QK + hd * GDN_DK:GDN_QK + (hd + 1) * GDN_DK]
        q_parts.append(qh * lax.rsqrt(jnp.sum(qh * qh, axis=-1, keepdims=True) + EPS) * GDN_DK ** -0.5)
        k_parts.append(kh * lax.rsqrt(jnp.sum(kh * kh, axis=-1, keepdims=True) + EPS))
    q = jnp.concatenate(q_parts, axis=1)
    k = jnp.concatenate(k_parts, axis=1)
    v = act[:, 2 * GDN_QK:]
    g = -jnp.exp(alog_ref[...]) * jax.nn.softplus(ab + dtb_ref[...])
    beta = jax.nn.sigmoid(ab)

    group = 4
    sup = nbu * tl
    width = group * sup
    n_units = bb // nbu
    n_sq = int(math.log2(tl)) - 1
    lg_sup = int(math.log2(sup))
    lg_tl = int(math.log2(tl))
    assert bb % nbu == 0 and 1 << lg_sup == sup and 1 << lg_tl == tl and n_sq >= 1
    cum_rows = min(rows, max(tl, MXU_ROWS))
    assert rows % cum_rows == 0
    tri = _block_masks(cum_rows, tl)[0].astype(F32).astype(BF16)
    t4 = lax.broadcasted_iota(jnp.int32, (sup, width), 0)
    col4 = lax.broadcasted_iota(jnp.int32, (sup, width), 1)
    s4 = col4 & (sup - 1)
    same4 = (t4 >> lg_tl) == (s4 >> lg_tl)
    incl4 = same4 & (t4 >= s4)
    strict4 = same4 & (t4 > s4)
    eye4 = (t4 == s4).astype(F32)
    bd_mask = ((lax.broadcasted_iota(jnp.int32, (width, width), 0) >> lg_sup)
               == (lax.broadcasted_iota(jnp.int32, (width, width), 1) >> lg_sup))
    head4 = col4 >> lg_sup

    def per_head_lanes(x, first):
        return jnp.concatenate(
            [jnp.broadcast_to(x[:, first + hd:first + hd + 1], (rows, GDN_DK)) for hd in heads], axis=1)

    gcum = jnp.concatenate([_exact_dot(g[r0:r0 + cum_rows], lambda p: _dot(tri, p))
                            for r0 in range(0, rows, cum_rows)], axis=0)
    gcx = per_head_lanes(gcum, 0)
    bx = per_head_lanes(beta, GDN_HEADS)
    egx = jnp.exp(gcx)
    kb = k * bx
    qe = q * egx
    q_b, k_b, kb_b = _bf(q, k, kb)
    rhs_v = v * bx
    rhs_k = kb * egx
    kend, egl = [], []
    for b in seqs:
        gl = gcx[(b + 1) * tl - 1:(b + 1) * tl, :]
        kend.append(k[b * tl:(b + 1) * tl] * jnp.exp(gl - gcx[b * tl:(b + 1) * tl]))
        egl.append(jnp.exp(gl))

    keys = [(u, hg) for u in range(n_units) for hg in range(GDN_HEADS // group)]
    m4, att4 = {}, {}
    for u, hg in keys:
        ru = slice(u * sup, (u + 1) * sup)
        lanes4 = slice(hg * group * GDN_DK, (hg + 1) * group * GDN_DK)
        kdiag = []
        for hh in range(group):
            pieces = [jnp.zeros((sup, GDN_DK), BF16)] * group
            pieces[hh] = k_b[ru, hsl[hg * group + hh]]
            kdiag.append(jnp.concatenate(pieces, axis=1))
        kdiag = jnp.concatenate(kdiag, axis=0)
        kk = _dot_nt(kb_b[ru, lanes4], kdiag)
        qk = _dot_nt(q_b[ru, lanes4], kdiag)
        gcol = None
        for hh in range(group):
            rep = gcx[ru, hsl[hg * group + hh]]
            rep = rep[:, :width] if width <= GDN_DK else jnp.concatenate([rep] * (width // GDN_DK), axis=1)
            gcol = rep if gcol is None else jnp.where(head4 == hh, rep, gcol)
        grow = jnp.sum(eye4 * gcol, axis=0, keepdims=True)
        decay = jnp.exp(gcol - grow)
        m4[u, hg] = -jnp.where(strict4, kk * decay, 0.0)
        att4[u, hg] = jnp.where(incl4, qk * decay, 0.0)

    def block_diag(m_b):
        return jnp.where(bd_mask, jnp.concatenate([m_b] * group, axis=0), jnp.zeros((), BF16))

    p4 = {key: eye4 + m4[key] for key in keys}
    m_b = {key: m4[key].astype(BF16) for key in keys}
    m4 = {key: _dot(m_b[key], block_diag(m_b[key])) for key in keys}
    for _ in range(1, n_sq):
        m_b = {key: m4[key].astype(BF16) for key in keys}
        x = {key: _dot(jnp.concatenate([p4[key].astype(BF16), m_b[key]], axis=0), block_diag(m_b[key]))
             for key in keys}
        p4 = {key: p4[key] + x[key][:sup] for key in keys}
        m4 = {key: x[key][sup:] for key in keys}
    m_b = {key: m4[key].astype(BF16) for key in keys}
    p4 = {key: p4[key] + _dot(p4[key].astype(BF16), block_diag(m_b[key])) for key in keys}

    uw = [[None] * GDN_HEADS for _ in range(n_units)]
    for u, hg in keys:
        ru = slice(u * sup, (u + 1) * sup)
        p_b = p4[u, hg].astype(BF16)
        for hh in range(group):
            hd = hg * group + hh
            rhs = jnp.concatenate([rhs_v[ru, hsl[hd]], rhs_k[ru, hsl[hd]]], axis=1)
            uw[u][hd] = _dot(p_b[:, hh * sup:(hh + 1) * sup], rhs.astype(BF16))

    st = [[s_ref[b, hd] for hd in heads] for b in seqs]
    v_new = [[None] * GDN_HEADS for _ in seqs]
    qs = [[None] * GDN_HEADS for _ in seqs]
    for b in seqs:
        u, i = divmod(b, nbu)
        for hd in heads:
            wq = jnp.concatenate([uw[u][hd][i * tl:(i + 1) * tl, GDN_DV:], qe[b * tl:(b + 1) * tl, hsl[hd]]],
                                 axis=0)
            ws = _dot(wq.astype(BF16), st[b][hd].astype(BF16))
            v_new[b][hd] = uw[u][hd][i * tl:(i + 1) * tl, :GDN_DV] - ws[:tl]
            qs[b][hd] = ws[tl:]
    o_units = []
    for u in range(n_units):
        o_heads = []
        for hd in heads:
            hg, hh = divmod(hd, group)
            members = range(u * nbu, (u + 1) * nbu)
            vn = jnp.concatenate([v_new[b][hd] for b in members], axis=0)
            att = att4[u, hg][:, hh * sup:(hh + 1) * sup]
            o_heads.append(jnp.concatenate([qs[b][hd] for b in members], axis=0)
                           + _dot(att.astype(BF16), vn.astype(BF16)))
        o_units.append(jnp.concatenate(o_heads, axis=1))
    for b in seqs:
        for hd in heads:
            ke, vn = _cast_small(tl, kend[b][:, hsl[hd]], v_new[b][hd])
            s_ref[b, hd] = egl[b][:, hsl[hd]] * st[b][hd] + _dot_tn(ke, vn)
    o_all = jnp.concatenate(o_units, axis=0)

    parts = []
    for hd in range(GDN_HEADS):
        parts.append(_rms_rows(o_all[:, hd * GDN_DV:(hd + 1) * GDN_DV], ng_ref[...]))
    on = jnp.concatenate(parts, axis=-1)
    out = h + _mm(on * jax.nn.silu(z), wout_ref[...])
    o_ref[...] = _rms_rows(out, fg_ref[...]).reshape(bb, tl, D_MODEL)

    @pl.when(l == pl.num_programs(1) - 1)
    def _():
        _slabs_to_history(ext_ref, cbout_ref, tl + hist, pitch)

    for b in seqs:
        for s in slabs:
            ext_ref[s, b * pitch:b * pitch + GDN_PAD, :] = ext_ref[s, b * pitch + tl:(b + 1) * pitch, :]


def _gdn_layer(h_tm, s0, cb, w, bb, tl, nbu):
    seq = h_tm.shape[0]
    bsz = h_tm.size // (seq * D_MODEL)
    shared = s0.shape[1] != bsz
    sblock = (bb, GDN_HEADS, GDN_DK, GDN_DV)
    hspec = pl.BlockSpec((bb, tl, D_MODEL), lambda b, l: (b, l, 0))
    scratch = [pltpu.VMEM((GDN_QKV // LANES, bb * (GDN_PAD + tl), LANES), F32)]
    if h_tm.ndim == 3:
        assert _reorders_rows(bb, tl, seq)
        in_spec = pl.BlockSpec((tl, bb, D_MODEL), lambda b, l: (l, b, 0))
        scratch.append(pltpu.VMEM((D_MODEL // LANES, tl * bb, LANES), F32))
    else:
        in_spec = pl.BlockSpec((tl, bb * D_MODEL), lambda b, l: (l, b))
    assert bb % SUBLANES == 0 or bb == bsz
    y, s_new, cb_new = pl.pallas_call(
        functools.partial(_gdn_kernel, nbu),
        grid=(bsz // bb, seq // tl),
        in_specs=[in_spec, _state_spec(sblock, shared), _history_spec(GDN_CONV - 1, bb, GDN_QKV, shared),
                  _const_spec((1, D_MODEL)), _const_spec((D_MODEL, GDN_QKV)),
                  _const_spec((D_MODEL, GDN_V)), _const_spec((D_MODEL, LANES)),
                  _const_spec((GDN_CONV, GDN_QKV)), _const_spec((1, LANES)), _const_spec((1, LANES)),
                  _const_spec((1, GDN_DV)), _const_spec((1, D_MODEL)), _const_spec((GDN_V, D_MODEL))],
        out_specs=[hspec, _state_spec(sblock), _history_spec(GDN_CONV - 1, bb, GDN_QKV)],
        out_shape=[jax.ShapeDtypeStruct((bsz, seq, D_MODEL), F32), _state_shape(bsz, sblock),
                   jax.ShapeDtypeStruct((1, GDN_CONV - 1, bsz, GDN_QKV), F32)],
        scratch_shapes=scratch,
        compiler_params=_params(),
        name="gdn_layer",
    )(h_tm, s0, _time_major(cb), w["g"], w["wqkv"], w["wz"], w["wab"], w["conv_w"], w["a_log"],
      w["dt_bias"], w["ng"], w["fg"], w["w_out"])
    return y, s_new, _time_major(cb_new)


_GLA_SPLITS = (GLA_QK, GLA_QK, GLA_V, GLA_V, GLA_RANK)
_GDN_SPLITS = (GDN_QKV, GDN_V, 2 * GDN_HEADS)
CAST_ROWS = 128


def _cast_kernel(n_in, splits, *refs):
    ins, outs = refs[:n_in], refs[n_in:]
    o = 0
    for ref, cols in zip(ins, splits):
        w = ref[...]
        lo = 0
        for width in cols:
            piece = w[:, lo:lo + width]
            if width % LANES:
                piece = jnp.concatenate(
                    [piece, jnp.zeros((piece.shape[0], LANES - width % LANES), piece.dtype)], axis=1)
            outs[o][...] = piece.astype(BF16)
            lo += width
            o += 1


def _cast_weights(weights, splits):
    k = weights[0].shape[1]
    out_widths = [wd + (-wd) % LANES for cols in splits for wd in cols]
    return pl.pallas_call(
        functools.partial(_cast_kernel, len(weights), splits),
        grid=(k // CAST_ROWS,),
        in_specs=[pl.BlockSpec((None, CAST_ROWS, w.shape[2]), lambda i: (0, i, 0)) for w in weights],
        out_specs=[pl.BlockSpec((CAST_ROWS, wd), lambda i: (i, 0)) for wd in out_widths],
        out_shape=[jax.ShapeDtypeStruct((k, wd), BF16) for wd in out_widths],
        compiler_params=pltpu.CompilerParams(dimension_semantics=("parallel",), vmem_limit_bytes=VMEM_LIMIT),
        name="cast_weights",
    )(*weights)


def _row(x, width=None):
    x = x.reshape(1, -1).astype(F32)
    if width is not None and x.shape[1] < width:
        x = jnp.pad(x, ((0, 0), (0, width - x.shape[1])))
    return x


def _s5_weights(j, norm_g, w_in, s5_b_re, s5_b_im, s5_c_re, s5_c_im, s5_d, s5_log_dt, s5_a_re,
                s5_a_im, w_glu, s5_b_glu, w_out):
    abr, abi, wb, wc = _s5_prep(s5_log_dt[j], s5_a_re[j], s5_a_im[j], s5_b_re[j], s5_b_im[j],
                                s5_c_re[j], s5_c_im[j])
    return dict(g=_row(norm_g), w_in=w_in, wb=wb, wc=wc, abr=_row(abr), abi=_row(abi), d=_row(s5_d[j]),
                w_glu=w_glu, b_glu=_row(s5_b_glu[j]), w_out=w_out)


def kernel(x_prompt, x_sample, state_pool, state_gla, state_s5_re, state_s5_im, state_gdn, state_gdn_conv, meta_tokens, norm_g, final_norm_g, pool_w_in, pool_w_grp, pool_scale, pool_w_out, gla_w_in, gla_w_gk, gla_b_gk, gla_norm_g, gla_w_out, s5_w_in, s5_b_re, s5_b_im, s5_c_re, s5_c_im, s5_d, s5_log_dt, s5_a_re, s5_a_im, s5_w_glu, s5_b_glu, s5_w_out, gdn_w_in, gdn_conv_w, gdn_a_log, gdn_dt_bias, gdn_norm_g, gdn_w_out):
    bp = x_prompt.shape[0]
    bs, ls, _ = x_sample.shape

    one = (D_MODEL,)
    (p_in, p_out, g_q, g_k, g_v, g_gate, g_low, g_out, s_in, s_glu, s_out, d_qkv, d_z, d_ab, d_out) = _cast_weights(
        [pool_w_in, pool_w_out, gla_w_in, gla_w_out, s5_w_in, s5_w_glu, s5_w_out, gdn_w_in, gdn_w_out],
        [(2 * D_MODEL,), one, _GLA_SPLITS, one, (2 * D_MODEL,), one, one, _GDN_SPLITS, one])
    wp = dict(g=_row(norm_g[0]), w_in=p_in, w_grp=pool_w_grp[0].astype(BF16),
              scale=_row(pool_scale[0]), w_out=p_out)
    wg = dict(g=_row(norm_g[1]), wq=g_q, wk=g_k, wv=g_v, wgate=g_gate, wglow=g_low,
              wgk=jnp.pad(gla_w_gk[0], ((0, LANES - GLA_RANK), (0, 0))).astype(BF16),
              bgk=_row(gla_b_gk[0]), ng=_row(gla_norm_g[0]), w_out=g_out)
    ws = _s5_weights(0, norm_g[2], s_in, s5_b_re, s5_b_im, s5_c_re, s5_c_im, s5_d, s5_log_dt,
                     s5_a_re, s5_a_im, s_glu, s5_b_glu, s_out)
    wd = dict(g=_row(norm_g[3]), wqkv=d_qkv, wz=d_z, wab=d_ab, conv_w=gdn_conv_w[0],
              a_log=_row(gdn_a_log[0], LANES), dt_bias=_row(gdn_dt_bias[0], LANES),
              ng=_row(gdn_norm_g[0]), fg=_row(final_norm_g), w_out=d_out)

    def run(h, pool_st, n_valid, gla_st, s5_st, gdn_st, conv_st, blocks):
        (pb, pt), (gb, gt), (sb, stl), (db, dtl, dn) = blocks
        bsz, seq, _ = h.shape
        h, pool_new = _pool_layer(h, pool_st, n_valid, wp, pb, pt)
        h_tm, gla_new = _gla_layer(h, gla_st, wg, gb, gt)
        h_tb, s5_new = _s5_layer(h_tm.reshape(seq, bsz, D_MODEL), s5_st, ws, sb, stl)
        if not _reorders_rows(db, dtl, seq):
            h_tb = h_tb.reshape(seq, bsz * D_MODEL)
        y, gdn_new, conv_new = _gdn_layer(h_tb, gdn_st, conv_st, wd, db, dtl, dn)
        return y, pool_new, gla_new, s5_new, gdn_new, conv_new

    hm = meta_tokens.astype(F32)[None]
    zeros = lambda *s: jnp.zeros(s, F32)
    hm, m_pool = _pool_layer(hm, zeros(1, 1, POOL_BUF, D_MODEL), 0, wp, 1, N_META)
    hm, m_gla = _gla_layer(hm, zeros(1, 1, GLA_HEADS, GLA_DK, GLA_DV), wg, 1, N_META)
    hm_tb, m_s5 = _s5_layer(jnp.broadcast_to(hm[:, None, :], (N_META, SUBLANES, D_MODEL)),
                            zeros(SUBLANES, S5_STATE), ws, SUBLANES, N_META)
    _, m_gdn, m_conv = _gdn_layer(hm_tb[:, 0], zeros(1, 1, GDN_HEADS, GDN_DK, GDN_DV),
                                  zeros(1, 1, GDN_CONV - 1, GDN_QKV), wd, 1, N_META, 1)

    yp, pool_p, gla_p, s5_p, gdn_p, conv_p = run(
        x_prompt, m_pool, N_META, m_gla, jnp.broadcast_to(m_s5[0:1], (bp, S5_STATE)), m_gdn, m_conv,
        ((1, 512), (8, 64), (bp, 64), (8, 64, 1)))
    ys, pool_s, gla_s, s5_s, gdn_s, conv_s = run(
        x_sample, state_pool, POOL_BUF, state_gla, _s5_pack_state(state_s5_re[0], state_s5_im[0]),
        state_gdn, state_gdn_conv, ((32, ls), (16, ls), (32, ls), (16, ls, 8)))

    s5r_p, s5i_p = _s5_unpack_state(s5_p)
    s5r_s, s5i_s = _s5_unpack_state(s5_s)
    return (yp, ys, pool_p, pool_s, gla_p, gla_s, s5r_p[None], s5i_p[None], s5r_s[None], s5i_s[None],
            gdn_p, conv_p, gdn_s, conv_s)
```

```python
import functools
import math

import jax
import jax.numpy as jnp
from jax import lax
from jax.experimental import pallas as pl
from jax.experimental.pallas import tpu as pltpu

F32 = jnp.float32
BF16 = jnp.bfloat16
HIGHEST = lax.Precision.HIGHEST

D_MODEL = 1024
EPS = 1e-6
N_META = 16

POOL_WINDOWS = (2, 4, 8, 16)
POOL_GROUP = D_MODEL // len(POOL_WINDOWS)
POOL_BUF = max(POOL_WINDOWS) - 1
POOL_PAD = POOL_BUF + 1

GLA_HEADS = 4
GLA_DK = 128
GLA_DV = 256
GLA_QK = GLA_HEADS * GLA_DK
GLA_V = GLA_HEADS * GLA_DV
GLA_RANK = 16
GLA_GATE_NORM = 16.0

S5_GROUP = 16
S5_G = D_MODEL // S5_GROUP
S5_N = 64
S5_BUNDLE = 8
S5_NB = S5_G // S5_BUNDLE
S5_HALF = S5_BUNDLE * S5_N
S5_STATE = 2 * S5_G * S5_N

GDN_HEADS = 8
GDN_DK = 128
GDN_DV = 128
GDN_CONV = 4
GDN_QK = GDN_HEADS * GDN_DK
GDN_V = GDN_HEADS * GDN_DV
GDN_QKV = 2 * GDN_QK + GDN_V
GDN_PAD = 8

LANES = 128
SUBLANES = 8
MXU_ROWS = 256
VMEM_LIMIT = 58 * 1024 * 1024

_NT = (((1,), (1,)), ((), ()))
_TN = (((0,), (0,)), ((), ()))


def _rms_rows(x, g):
    return x * lax.rsqrt(jnp.mean(x * x, axis=-1, keepdims=True) + EPS) * g


def _mm(a, w):
    return jnp.dot(a.astype(BF16), w, preferred_element_type=F32)


def _cast_small(c, *xs):
    if c % 16 == 0:
        return tuple(x.astype(BF16) for x in xs)
    return xs


def _bf(*xs):
    return tuple(x.astype(BF16) for x in xs)


def _block_masks(n, c):
    r = lax.broadcasted_iota(jnp.int32, (n, n), 0)
    s = lax.broadcasted_iota(jnp.int32, (n, n), 1)
    sh = int(math.log2(c))
    same = (r >> sh) == (s >> sh)
    return same & (r >= s), same & (r > s), r == s


def _exact_dot(x, dot_piece):
    hi = x.astype(BF16)
    r = x - hi.astype(F32)
    mid = r.astype(BF16)
    lo = (r - mid.astype(F32)).astype(BF16)
    return dot_piece(hi) + dot_piece(mid) + dot_piece(lo)


def _dot(a, b):
    return jnp.dot(a, b, preferred_element_type=F32)


def _dot_nt(a, b):
    return lax.dot_general(a, b, _NT, preferred_element_type=F32)


def _dot_tn(a, b):
    return lax.dot_general(a, b, _TN, preferred_element_type=F32)


def _history_to_slabs(src_ref, dst_ref, first, pitch, bb):
    n, nb, w = src_ref.shape
    for j in range(n):
        row = src_ref[j]
        if nb != bb:
            row = jnp.broadcast_to(row, (bb, w))
        for s in range(w // LANES):
            piece = row[:, s * LANES:(s + 1) * LANES]
            if bb == 1:
                dst_ref[s, first + j:first + j + 1, :] = piece
            else:
                dst_ref[s, pl.ds(first + j, bb, stride=pitch), :] = piece


def _slabs_to_history(src_ref, dst_ref, first, pitch):
    n, bb, w = dst_ref.shape
    for j in range(n):
        if bb == 1:
            pieces = [src_ref[s, first + j:first + j + 1, :] for s in range(w // LANES)]
        else:
            pieces = [src_ref[s, pl.ds(first + j, bb, stride=pitch), :] for s in range(w // LANES)]
        dst_ref[j] = jnp.concatenate(pieces, axis=1)


def _pool_kernel(n_valid, time_major_out, h_ref, buf_ref, g_ref, win_ref, wgrp_ref, scale_ref, wout_ref,
                 o_ref, st_ref, z_ref):
    bb, tl, _ = h_ref.shape
    rows = bb * tl
    l = pl.program_id(1)
    seqs = range(bb)
    slabs_per_group = POOL_GROUP // LANES
    pitch = POOL_PAD + tl

    @pl.when(l == 0)
    def _():
        _history_to_slabs(buf_ref, z_ref, POOL_PAD - POOL_BUF, pitch, bb)

    h = h_ref[...].reshape(rows, D_MODEL)
    xn = _rms_rows(h, g_ref[...])
    xn = xn.astype(BF16)
    ug = _mm(xn, win_ref[:, :D_MODEL])
    gate = _mm(xn, win_ref[:, D_MODEL:])
    for b in seqs:
        for s in range(D_MODEL // LANES):
            z_ref[s, b * pitch + POOL_PAD:(b + 1) * pitch, :] = ug[b * tl:(b + 1) * tl, s * LANES:(s + 1) * LANES]

    t = (l * tl + lax.broadcasted_iota(jnp.int32, (tl, LANES), 0)).astype(F32)
    parts = []
    for gi, w in enumerate(POOL_WINDOWS):
        cnt = jnp.minimum(float(w), t + (1.0 + n_valid))
        mixed_rows = []
        for b in seqs:
            cols = []
            for s in range(gi * slabs_per_group, (gi + 1) * slabs_per_group):
                cur = ug[b * tl:(b + 1) * tl, s * LANES:(s + 1) * LANES]
                acc = cur
                for j in range(1, w):
                    acc = acc + z_ref[s, pl.ds(b * pitch + POOL_PAD - j, tl, stride=1), :]
                cols.append(acc / cnt - cur)
            mixed_rows.append(jnp.concatenate(cols, axis=1))
        parts.append(_mm(jnp.concatenate(mixed_rows, axis=0), wgrp_ref[gi]))
    mixed = jnp.concatenate(parts, axis=-1) * scale_ref[...]
    y = _mm(mixed * jax.nn.silu(gate), wout_ref[...])
    o_ref[...] = (h + y).reshape(bb, tl, D_MODEL)

    @pl.when(l == pl.num_programs(1) - 1)
    def _():
        if time_major_out:
            _slabs_to_history(z_ref, st_ref, tl + POOL_PAD - POOL_BUF, pitch)
        else:
            for b in seqs:
                st_ref[b] = jnp.concatenate(
                    [z_ref[s, b * pitch + tl + 1:(b + 1) * pitch, :] for s in range(D_MODEL // LANES)], axis=1)

    for b in seqs:
        for s in range(D_MODEL // LANES):
            z_ref[s, b * pitch:b * pitch + POOL_PAD, :] = z_ref[s, b * pitch + tl:(b + 1) * pitch, :]


def _const_spec(shape):
    nd = len(shape)
    return pl.BlockSpec(shape, lambda b, l: (0,) * nd, pipeline_mode=pl.Buffered(1))


def _state_spec(block, shared=False):
    nd = len(block)
    if shared:
        return pl.BlockSpec((None, 1) + tuple(block[1:]), lambda b, l: (0,) * (nd + 1))
    return pl.BlockSpec((None,) + tuple(block), lambda b, l: (0, b) + (0,) * (nd - 1))


def _time_major(st):
    return jnp.transpose(st, (0, 2, 1, 3))


def _history_spec(n, bb, width, shared=False):
    if shared:
        return pl.BlockSpec((None, n, 1, width), lambda b, l: (0, 0, 0, 0))
    return pl.BlockSpec((None, n, bb, width), lambda b, l: (0, 0, b, 0))


def _reorders_rows(bb, tl, seq):
    return seq == tl and bb % SUBLANES == 0


def _state_shape(bsz, block):
    return jax.ShapeDtypeStruct((1, bsz) + tuple(block[1:]), F32)


def _params():
    return pltpu.CompilerParams(dimension_semantics=("parallel", "arbitrary"),
                                vmem_limit_bytes=VMEM_LIMIT)


def _pool_layer(h, buf, n_valid, w, bb, tl):
    bsz, seq, _ = h.shape
    hspec = pl.BlockSpec((bb, tl, D_MODEL), lambda b, l: (b, l, 0))
    tm_out = bb % SUBLANES == 0 or bb == bsz
    if tm_out:
        st_spec = _history_spec(POOL_BUF, bb, D_MODEL)
        st_shape = jax.ShapeDtypeStruct((1, POOL_BUF, bsz, D_MODEL), F32)
    else:
        st_spec = _state_spec((bb, POOL_BUF, D_MODEL))
        st_shape = _state_shape(bsz, (bb, POOL_BUF, D_MODEL))
    out, st = pl.pallas_call(
        functools.partial(_pool_kernel, float(n_valid), tm_out),
        grid=(bsz // bb, seq // tl),
        in_specs=[hspec, _history_spec(POOL_BUF, bb, D_MODEL, buf.shape[1] != bsz),
                  _const_spec((1, D_MODEL)), _const_spec((D_MODEL, 2 * D_MODEL)),
                  _const_spec((len(POOL_WINDOWS), POOL_GROUP, POOL_GROUP)),
                  _const_spec((1, D_MODEL)), _const_spec((D_MODEL, D_MODEL))],
        out_specs=[hspec, st_spec],
        out_shape=[jax.ShapeDtypeStruct(h.shape, F32), st_shape],
        scratch_shapes=[pltpu.VMEM((D_MODEL // LANES, bb * (POOL_PAD + tl), LANES), F32)],
        compiler_params=_params(),
        name="pool_layer",
    )(h, _time_major(buf), w["g"], w["w_in"], w["w_grp"], w["scale"], w["w_out"])
    return out, (_time_major(st) if tm_out else st)


def _gla_kernel(h_ref, s0_ref, g_ref, wq_ref, wk_ref, wv_ref, wgate_ref, wglow_ref, wgk_ref,
                bgk_ref, ng_ref, wout_ref, o_ref, s_ref, *tm_ref):
    bb, tl, _ = h_ref.shape
    rows = bb * tl

    @pl.when(pl.program_id(1) == 0)
    def _():
        s_ref[...] = jnp.broadcast_to(s0_ref[...], s_ref.shape)

    h = h_ref[...].reshape(rows, D_MODEL)
    xn = _rms_rows(h, g_ref[...]).astype(BF16)
    glow = _mm(xn, wglow_ref[...])
    gk = jax.nn.log_sigmoid(_mm(glow, wgk_ref[...]) + bgk_ref[...]) / GLA_GATE_NORM
    q = _mm(xn, wq_ref[...]) * GLA_DK ** -0.5
    k = _mm(xn, wk_ref[...])
    v = _mm(xn, wv_ref[...])
    gate = _mm(xn, wgate_ref[...])

    unit_seqs = max(1, min(bb, MXU_ROWS // tl))
    assert bb % unit_seqs == 0
    n_unit = unit_seqs * tl
    incl, _, _ = _block_masks(n_unit, tl)
    tri = incl.astype(F32).astype(BF16)
    eye_k = (lax.broadcasted_iota(jnp.int32, (GLA_DK, GLA_DK), 0)
             == lax.broadcasted_iota(jnp.int32, (GLA_DK, GLA_DK), 1))
    heads = range(GLA_HEADS)
    ksl = [slice(hd * GLA_DK, (hd + 1) * GLA_DK) for hd in heads]
    vsl = [slice(hd * GLA_DV, (hd + 1) * GLA_DV) for hd in heads]
    blks = [slice(i * tl, (i + 1) * tl) for i in range(unit_seqs)]

    o_units = []
    for u in range(bb // unit_seqs):
        ru = slice(u * n_unit, (u + 1) * n_unit)
        seqs = range(u * unit_seqs, (u + 1) * unit_seqs)
        ku, vu = k[ru], v[ru]
        bc = _exact_dot(gk[ru], lambda p: _dot(tri, p))
        qg = q[ru] * jnp.exp(bc)
        kg = ku * jnp.exp(-bc)
        kd, ebl = [], []
        for i in range(unit_seqs):
            bl = bc[(i + 1) * tl - 1:(i + 1) * tl, :]
            kd.append(ku[blks[i]] * jnp.exp(bl - bc[blks[i]]))
            ebl.append([jnp.exp(jnp.sum(
                jnp.where(eye_k, jnp.broadcast_to(bl[:, ksl[hd]], (GLA_DK, GLA_DK)), 0.0),
                axis=1, keepdims=True)) for hd in heads])
        qg_b, kg_b, v_b = _bf(qg, kg, vu)
        att = [jnp.where(incl, _dot_nt(qg_b[:, ksl[hd]], kg_b[:, ksl[hd]]), 0.0) for hd in heads]
        o = [_dot(att[hd].astype(BF16), v_b[:, vsl[hd]]) for hd in heads]
        st = [[s_ref[b, hd] for hd in heads] for b in seqs]
        o_st = [[None] * GLA_HEADS for _ in seqs]
        for i in range(unit_seqs):
            for hd in heads:
                qi, si = _cast_small(tl, qg[blks[i], ksl[hd]], st[i][hd])
                o_st[i][hd] = _dot(qi, si)
        for i, b in enumerate(seqs):
            for hd in heads:
                kdi, vi = _cast_small(tl, kd[i][:, ksl[hd]], vu[blks[i], vsl[hd]])
                s_ref[b, hd] = ebl[i][hd] * st[i][hd] + _dot_tn(kdi, vi)
        o_units.append(jnp.concatenate(
            [o[hd] + jnp.concatenate([o_st[i][hd] for i in range(unit_seqs)], axis=0) for hd in heads], axis=1))
    o_all = jnp.concatenate(o_units, axis=0)

    parts = []
    for hd in range(GLA_HEADS):
        parts.append(_rms_rows(o_all[:, hd * GLA_DV:(hd + 1) * GLA_DV], ng_ref[...]))
    on = jnp.concatenate(parts, axis=-1)
    out = h + _mm(on * jax.nn.silu(gate), wout_ref[...])
    if not tm_ref:
        for i in range(bb):
            o_ref[:, i * D_MODEL:(i + 1) * D_MODEL] = out[i * tl:(i + 1) * tl]
    else:
        for s in range(D_MODEL // LANES):
            for i in range(bb):
                tm_ref[0][s, pl.ds(i, tl, stride=bb), :] = out[i * tl:(i + 1) * tl, s * LANES:(s + 1) * LANES]
            o_ref[:, :, s * LANES:(s + 1) * LANES] = tm_ref[0][s].reshape(tl, bb, LANES)


def _gla_layer(h, s0, w, bb, tl):
    bsz, seq, _ = h.shape
    sblock = (bb, GLA_HEADS, GLA_DK, GLA_DV)
    hspec = pl.BlockSpec((bb, tl, D_MODEL), lambda b, l: (b, l, 0))
    if _reorders_rows(bb, tl, seq):
        ospec = pl.BlockSpec((tl, bb, D_MODEL), lambda b, l: (l, b, 0))
        oshape = jax.ShapeDtypeStruct((seq, bsz, D_MODEL), F32)
        scratch = [pltpu.VMEM((D_MODEL // LANES, tl * bb, LANES), F32)]
    else:
        ospec = pl.BlockSpec((tl, bb * D_MODEL), lambda b, l: (l, b))
        oshape = jax.ShapeDtypeStruct((seq, bsz * D_MODEL), F32)
        scratch = []
    return pl.pallas_call(
        _gla_kernel,
        grid=(bsz // bb, seq // tl),
        in_specs=[hspec, _state_spec(sblock, s0.shape[1] != bsz),
                  _const_spec((1, D_MODEL)),
                  _const_spec((D_MODEL, GLA_QK)), _const_spec((D_MODEL, GLA_QK)),
                  _const_spec((D_MODEL, GLA_V)), _const_spec((D_MODEL, GLA_V)),
                  _const_spec((D_MODEL, LANES)), _const_spec((LANES, GLA_QK)),
                  _const_spec((1, GLA_QK)), _const_spec((1, GLA_DV)),
                  _const_spec((GLA_V, D_MODEL))],
        out_specs=[ospec, _state_spec(sblock)],
        out_shape=[oshape, _state_shape(bsz, sblock)],
        scratch_shapes=scratch,
        compiler_params=_params(),
        name="gla_layer",
    )(h, s0, w["g"], w["wq"], w["wk"], w["wv"], w["wgate"], w["wglow"], w["wgk"], w["bgk"],
      w["ng"], w["w_out"])


def _s5_prep_kernel(logdt_ref, are_ref, aim_ref, bre_ref, bim_ref, cre_ref, cim_ref,
                    abr_ref, abi_ref, wb_ref, wc_ref):
    dt = jnp.exp(logdt_ref[...])
    lr = are_ref[...]
    li = aim_ref[...]
    mag = jnp.exp(lr * dt)
    abr = mag * jnp.cos(li * dt)
    abi = mag * jnp.sin(li * dt)
    den = lr * lr + li * li
    cr = ((abr - 1.0) * lr + abi * li) / den
    ci = (abi * lr - (abr - 1.0) * li) / den
    abr_ref[...] = abr
    abi_ref[...] = abi
    br = bre_ref[...]
    bi = bim_ref[...]
    bbr = (cr[:, None, :] * br - ci[:, None, :] * bi).astype(BF16)
    bbi = (cr[:, None, :] * bi + ci[:, None, :] * br).astype(BF16)
    c_re = cre_ref[...].astype(BF16)
    c_im = (-cim_ref[...]).astype(BF16)
    wb_ref[...] = jnp.zeros(wb_ref.shape, BF16)
    wc_ref[...] = jnp.zeros(wc_ref.shape, BF16)
    for j in range(S5_NB):
        for gl in range(S5_BUNDLE):
            g = j * S5_BUNDLE + gl
            ch = slice(gl * S5_GROUP, (gl + 1) * S5_GROUP)
            re = slice(gl * S5_N, (gl + 1) * S5_N)
            im = slice(S5_HALF + gl * S5_N, S5_HALF + (gl + 1) * S5_N)
            wb_ref[j, ch, re] = bbr[g]
            wb_ref[j, ch, im] = bbi[g]
            wc_ref[j, re, ch] = c_re[g]
            wc_ref[j, im, ch] = c_im[g]


def _s5_prep(log_dt, a_re, a_im, b_re, b_im, c_re, c_im):
    gn = jax.ShapeDtypeStruct((S5_G, S5_N), F32)
    in_w = S5_BUNDLE * S5_GROUP
    return pl.pallas_call(
        _s5_prep_kernel,
        out_shape=[gn, gn, jax.ShapeDtypeStruct((S5_NB, in_w, 2 * S5_HALF), BF16),
                   jax.ShapeDtypeStruct((S5_NB, 2 * S5_HALF, in_w), BF16)],
        name="s5_discretize")(
        log_dt.reshape(S5_G, 1), a_re, a_im, b_re.transpose(0, 2, 1), b_im.transpose(0, 2, 1),
        c_re.transpose(0, 2, 1), c_im.transpose(0, 2, 1))


def _s5_kernel(h_ref, h0_ref, g_ref, win_ref, wb_ref, wc_ref, abr_ref, abi_ref, dskip_ref,
               wglu_ref, bglu_ref, wout_ref, o_ref, hs_ref):
    tl, nb, _ = h_ref.shape
    rows = tl * nb
    ngrp = nb // SUBLANES

    @pl.when(pl.program_id(1) == 0)
    def _():
        hs_ref[...] = h0_ref[...]

    h = h_ref[...].reshape(rows, D_MODEL)
    xn = _rms_rows(h, g_ref[...]).astype(BF16)
    ug = _mm(xn, win_ref[...])
    u = ug[:, :D_MODEL]
    gate = ug[:, D_MODEL:]
    ub = u.astype(BF16)
    width = 2 * S5_HALF
    in_w = S5_BUNDLE * S5_GROUP

    def project_in(j):
        return jnp.dot(ub[:, j * in_w:(j + 1) * in_w], wb_ref[j], preferred_element_type=F32)

    parts = []
    bu_next = project_in(0)
    for j in range(S5_NB):
        bu = bu_next
        if j + 1 < S5_NB:
            bu_next = project_in(j + 1)
        re = slice(j * width, j * width + S5_HALF)
        im = slice(j * width + S5_HALF, (j + 1) * width)
        ar = jnp.broadcast_to(abr_ref[:, j * S5_HALF:(j + 1) * S5_HALF], (SUBLANES, S5_HALF))
        ai = jnp.broadcast_to(abi_ref[:, j * S5_HALF:(j + 1) * S5_HALF], (SUBLANES, S5_HALF))
        tiles = [None] * (tl * ngrp)
        for bg in range(ngrp):
            grp = slice(bg * SUBLANES, (bg + 1) * SUBLANES)
            hr = hs_ref[grp, re]
            hi = hs_ref[grp, im]
            for t in range(tl):
                r = t * nb + bg * SUBLANES
                hr, hi = (ar * hr - ai * hi + bu[r:r + SUBLANES, :S5_HALF],
                          ar * hi + ai * hr + bu[r:r + SUBLANES, S5_HALF:])
                tiles[t * ngrp + bg] = jnp.concatenate([hr, hi], axis=1)
            hs_ref[grp, re] = hr
            hs_ref[grp, im] = hi
        parts.append(_mm(jnp.concatenate(tiles, axis=0), wc_ref[j]))
    y = jnp.concatenate(parts, axis=-1) + dskip_ref[...] * u
    z = jax.nn.gelu(y)
    z = z * jax.nn.sigmoid(_mm(z, wglu_ref[...]) + bglu_ref[...])
    out = h + _mm(z * jax.nn.silu(gate), wout_ref[...])
    o_ref[...] = out.reshape(tl, nb, D_MODEL)


def _s5_layer(h_tb, h0, w, nb, tl):
    seq, bsz, _ = h_tb.shape
    hspec = pl.BlockSpec((tl, nb, D_MODEL), lambda b, l: (l, b, 0))
    sspec = pl.BlockSpec((nb, S5_STATE), lambda b, l: (b, 0))
    in_w = S5_BUNDLE * S5_GROUP
    return pl.pallas_call(
        _s5_kernel,
        grid=(bsz // nb, seq // tl),
        in_specs=[hspec, sspec, _const_spec((1, D_MODEL)), _const_spec((D_MODEL, 2 * D_MODEL)),
                  _const_spec((S5_NB, in_w, 2 * S5_HALF)), _const_spec((S5_NB, 2 * S5_HALF, in_w)),
                  _const_spec((1, S5_G * S5_N)), _const_spec((1, S5_G * S5_N)),
                  _const_spec((1, D_MODEL)), _const_spec((D_MODEL, D_MODEL)),
                  _const_spec((1, D_MODEL)), _const_spec((D_MODEL, D_MODEL))],
        out_specs=[hspec, sspec],
        out_shape=[jax.ShapeDtypeStruct(h_tb.shape, F32), jax.ShapeDtypeStruct((bsz, S5_STATE), F32)],
        compiler_params=_params(),
        name="s5_layer",
    )(h_tb, h0, w["g"], w["w_in"], w["wb"], w["wc"], w["abr"], w["abi"], w["d"], w["w_glu"],
      w["b_glu"], w["w_out"])


def _s5_pack_state(re, im):
    b = re.shape[0]
    return jnp.concatenate([re.reshape(b, S5_NB, S5_HALF), im.reshape(b, S5_NB, S5_HALF)],
                           axis=-1).reshape(b, S5_STATE)


def _s5_unpack_state(st):
    b = st.shape[0]
    st = st.reshape(b, S5_NB, 2 * S5_HALF)
    return (st[..., :S5_HALF].reshape(b, S5_G, S5_N), st[..., S5_HALF:].reshape(b, S5_G, S5_N))


def _gdn_kernel(nbu, h_ref, s0_ref, cb_ref, g_ref, wqkv_ref, wz_ref, wab_ref, convw_ref, alog_ref,
                dtb_ref, ng_ref, fg_ref, wout_ref, o_ref, s_ref, cbout_ref, ext_ref, *tm_ref):
    bb, tl, _ = o_ref.shape
    rows = bb * tl
    l = pl.program_id(1)

    heads = range(GDN_HEADS)
    seqs = range(bb)
    slabs = range(GDN_QKV // LANES)
    pitch = GDN_PAD + tl
    hist = GDN_PAD - (GDN_CONV - 1)

    @pl.when(l == 0)
    def _():
        s_ref[...] = jnp.broadcast_to(s0_ref[...], s_ref.shape)
        _history_to_slabs(cb_ref, ext_ref, hist, pitch, bb)

    if not tm_ref:
        h = jnp.concatenate([h_ref[:, i * D_MODEL:(i + 1) * D_MODEL] for i in range(bb)], axis=0)
    else:
        for s in range(D_MODEL // LANES):
            tm_ref[0][s] = h_ref[:, :, s * LANES:(s + 1) * LANES].reshape(tl * bb, LANES)
        h = jnp.concatenate(
            [jnp.concatenate([tm_ref[0][s, pl.ds(i, tl, stride=bb), :] for s in range(D_MODEL // LANES)], axis=1)
             for i in range(bb)], axis=0)
    xn = _rms_rows(h, g_ref[...]).astype(BF16)
    ab = _mm(xn, wab_ref[...])
    qkv = _mm(xn, wqkv_ref[...])
    z = _mm(xn, wz_ref[...])
    for b in seqs:
        for s in slabs:
            ext_ref[s, b * pitch + GDN_PAD:(b + 1) * pitch, :] = qkv[b * tl:(b + 1) * tl, s * LANES:(s + 1) * LANES]
    conv_rows = []
    for b in seqs:
        cols = []
        for s in slabs:
            lanes = slice(s * LANES, (s + 1) * LANES)
            acc = convw_ref[GDN_CONV - 1:GDN_CONV, lanes] * qkv[b * tl:(b + 1) * tl, lanes]
            for j in range(GDN_CONV - 1):
                acc = acc + convw_ref[j:j + 1, lanes] * ext_ref[s, pl.ds(b * pitch + hist + j, tl, stride=1), :]
            cols.append(acc)
        conv_rows.append(jnp.concatenate(cols, axis=1))
    act = jax.nn.silu(jnp.concatenate(conv_rows, axis=0))
    hsl = [slice(hd * GDN_DK, (hd + 1) * GDN_DK) for hd in heads]
    q_parts, k_parts = [], []
    for hd in heads:
        qh = act[:, hd * GDN_DK:(hd + 1) * GDN_DK]
        kh = act[:, GDN_QK + hd * GDN_DK:GDN_QK + (hd + 1) * GDN_DK]
        q_parts.append(qh * lax.rsqrt(jnp.sum(qh * qh, axis=-1, keepdims=True) + EPS) * GDN_DK ** -0.5)
        k_parts.append(kh * lax.rsqrt(jnp.sum(kh * kh, axis=-1, keepdims=True) + EPS))
    q = jnp.concatenate(q_parts, axis=1)
    k = jnp.concatenate(k_parts, axis=1)
    v = act[:, 2 * GDN_QK:]
    g = -jnp.exp(alog_ref[...]) * jax.nn.softplus(ab + dtb_ref[...])
    beta = jax.nn.sigmoid(ab)

    group = 4
    sup = nbu * tl
    width = group * sup
    n_units = bb // nbu
    n_sq = int(math.log2(tl)) - 1
    lg_sup = int(math.log2(sup))
    lg_tl = int(math.log2(tl))
    assert bb % nbu == 0 and 1 << lg_sup == sup and 1 << lg_tl == tl and n_sq >= 1
    cum_rows = min(rows, max(tl, MXU_ROWS))
    assert rows % cum_rows == 0
    tri = _block_masks(cum_rows, tl)[0].astype(F32).astype(BF16)
    t4 = lax.broadcasted_iota(jnp.int32, (sup, width), 0)
    col4 = lax.broadcasted_iota(jnp.int32, (sup, width), 1)
    s4 = col4 & (sup - 1)
    same4 = (t4 >> lg_tl) == (s4 >> lg_tl)
    incl4 = same4 & (t4 >= s4)
    strict4 = same4 & (t4 > s4)
    eye4 = (t4 == s4).astype(F32)
    bd_mask = ((lax.broadcasted_iota(jnp.int32, (width, width), 0) >> lg_sup)
               == (lax.broadcasted_iota(jnp.int32, (width, width), 1) >> lg_sup))
    head4 = col4 >> lg_sup

    def per_head_lanes(x, first):
        return jnp.concatenate(
            [jnp.broadcast_to(x[:, first + hd:first + hd + 1], (rows, GDN_DK)) for hd in heads], axis=1)

    gcum = jnp.concatenate([_exact_dot(g[r0:r0 + cum_rows], lambda p: _dot(tri, p))
                            for r0 in range(0, rows, cum_rows)], axis=0)
    gcx = per_head_lanes(gcum, 0)
    bx = per_head_lanes(beta, GDN_HEADS)
    egx = jnp.exp(gcx)
    kb = k * bx
    qe = q * egx
    q_b, k_b, kb_b = _bf(q, k, kb)
    rhs_v = v * bx
    rhs_k = kb * egx
    kend, egl = [], []
    for b in seqs:
        gl = gcx[(b + 1) * tl - 1:(b + 1) * tl, :]
        kend.append(k[b * tl:(b + 1) * tl] * jnp.exp(gl - gcx[b * tl:(b + 1) * tl]))
        egl.append(jnp.exp(gl))

    keys = [(u, hg) for u in range(n_units) for hg in range(GDN_HEADS // group)]
    m4, att4 = {}, {}
    for u, hg in keys:
        ru = slice(u * sup, (u + 1) * sup)
        lanes4 = slice(hg * group * GDN_DK, (hg + 1) * group * GDN_DK)
        kdiag = []
        for hh in range(group):
            pieces = [jnp.zeros((sup, GDN_DK), BF16)] * group
            pieces[hh] = k_b[ru, hsl[hg * group + hh]]
            kdiag.append(jnp.concatenate(pieces, axis=1))
        kdiag = jnp.concatenate(kdiag, axis=0)
        kk = _dot_nt(kb_b[ru, lanes4], kdiag)
        qk = _dot_nt(q_b[ru, lanes4], kdiag)
        gcol = None
        for hh in range(group):
            rep = gcx[ru, hsl[hg * group + hh]]
            rep = rep[:, :width] if width <= GDN_DK else jnp.concatenate([rep] * (width // GDN_DK), axis=1)
            gcol = rep if gcol is None else jnp.where(head4 == hh, rep, gcol)
        grow = jnp.sum(eye4 * gcol, axis=0, keepdims=True)
        decay = jnp.exp(gcol - grow)
        m4[u, hg] = -jnp.where(strict4, kk * decay, 0.0)
        att4[u, hg] = jnp.where(incl4, qk * decay, 0.0)

    def block_diag(m_b):
        return jnp.where(bd_mask, jnp.concatenate([m_b] * group, axis=0), jnp.zeros((), BF16))

    p4 = {key: eye4 + m4[key] for key in keys}
    m_b = {key: m4[key].astype(BF16) for key in keys}
    m4 = {key: _dot(m_b[key], block_diag(m_b[key])) for key in keys}
    for _ in range(1, n_sq):
        m_b = {key: m4[key].astype(BF16) for key in keys}
        x = {key: _dot(jnp.concatenate([p4[key].astype(BF16), m_b[key]], axis=0), block_diag(m_b[key]))
             for key in keys}
        p4 = {key: p4[key] + x[key][:sup] for key in keys}
        m4 = {key: x[key][sup:] for key in keys}
    m_b = {key: m4[key].astype(BF16) for key in keys}
    p4 = {key: p4[key] + _dot(p4[key].astype(BF16), block_diag(m_b[key])) for key in keys}

    uw = [[None] * GDN_HEADS for _ in range(n_units)]
    for u, hg in keys:
        ru = slice(u * sup, (u + 1) * sup)
        p_b = p4[u, hg].astype(BF16)
        for hh in range(group):
            hd = hg * group + hh
            rhs = jnp.concatenate([rhs_v[ru, hsl[hd]], rhs_k[ru, hsl[hd]]], axis=1)
            uw[u][hd] = _dot(p_b[:, hh * sup:(hh + 1) * sup], rhs.astype(BF16))

    st = [[s_ref[b, hd] for hd in heads] for b in seqs]
    v_new = [[None] * GDN_HEADS for _ in seqs]
    qs = [[None] * GDN_HEADS for _ in seqs]
    for b in seqs:
        u, i = divmod(b, nbu)
        for hd in heads:
            wq = jnp.concatenate([uw[u][hd][i * tl:(i + 1) * tl, GDN_DV:], qe[b * tl:(b + 1) * tl, hsl[hd]]],
                                 axis=0)
            ws = _dot(wq.astype(BF16), st[b][hd].astype(BF16))
            v_new[b][hd] = uw[u][hd][i * tl:(i + 1) * tl, :GDN_DV] - ws[:tl]
            qs[b][hd] = ws[tl:]
    o_units = []
    for u in range(n_units):
        o_heads = []
        for hd in heads:
            hg, hh = divmod(hd, group)
            members = range(u * nbu, (u + 1) * nbu)
            vn = jnp.concatenate([v_new[b][hd] for b in members], axis=0)
            att = att4[u, hg][:, hh * sup:(hh + 1) * sup]
            o_heads.append(jnp.concatenate([qs[b][hd] for b in members], axis=0)
                           + _dot(att.astype(BF16), vn.astype(BF16)))
        o_units.append(jnp.concatenate(o_heads, axis=1))
    for b in seqs:
        for hd in heads:
            ke, vn = _cast_small(tl, kend[b][:, hsl[hd]], v_new[b][hd])
            s_ref[b, hd] = egl[b][:, hsl[hd]] * st[b][hd] + _dot_tn(ke, vn)
    o_all = jnp.concatenate(o_units, axis=0)

    parts = []
    for hd in range(GDN_HEADS):
        parts.append(_rms_rows(o_all[:, hd * GDN_DV:(hd + 1) * GDN_DV], ng_ref[...]))
    on = jnp.concatenate(parts, axis=-1)
    out = h + _mm(on * jax.nn.silu(z), wout_ref[...])
    o_ref[...] = _rms_rows(out, fg_ref[...]).reshape(bb, tl, D_MODEL)

    @pl.when(l == pl.num_programs(1) - 1)
    def _():
        _slabs_to_history(ext_ref, cbout_ref, tl + hist, pitch)

    for b in seqs:
        for s in slabs:
            ext_ref[s, b * pitch:b * pitch + GDN_PAD, :] = ext_ref[s, b * pitch + tl:(b + 1) * pitch, :]


def _gdn_layer(h_tm, s0, cb, w, bb, tl, nbu):
    seq = h_tm.shape[0]
    bsz = h_tm.size // (seq * D_MODEL)
    shared = s0.shape[1] != bsz
    sblock = (bb, GDN_HEADS, GDN_DK, GDN_DV)
    hspec = pl.BlockSpec((bb, tl, D_MODEL), lambda b, l: (b, l, 0))
    scratch = [pltpu.VMEM((GDN_QKV // LANES, bb * (GDN_PAD + tl), LANES), F32)]
    if h_tm.ndim == 3:
        assert _reorders_rows(bb, tl, seq)
        in_spec = pl.BlockSpec((tl, bb, D_MODEL), lambda b, l: (l, b, 0))
        scratch.append(pltpu.VMEM((D_MODEL // LANES, tl * bb, LANES), F32))
    else:
        in_spec = pl.BlockSpec((tl, bb * D_MODEL), lambda b, l: (l, b))
    assert bb % SUBLANES == 0 or bb == bsz
    y, s_new, cb_new = pl.pallas_call(
        functools.partial(_gdn_kernel, nbu),
        grid=(bsz // bb, seq // tl),
        in_specs=[in_spec, _state_spec(sblock, shared), _history_spec(GDN_CONV - 1, bb, GDN_QKV, shared),
                  _const_spec((1, D_MODEL)), _const_spec((D_MODEL, GDN_QKV)),
                  _const_spec((D_MODEL, GDN_V)), _const_spec((D_MODEL, LANES)),
                  _const_spec((GDN_CONV, GDN_QKV)), _const_spec((1, LANES)), _const_spec((1, LANES)),
                  _const_spec((1, GDN_DV)), _const_spec((1, D_MODEL)), _const_spec((GDN_V, D_MODEL))],
        out_specs=[hspec, _state_spec(sblock), _history_spec(GDN_CONV - 1, bb, GDN_QKV)],
        out_shape=[jax.ShapeDtypeStruct((bsz, seq, D_MODEL), F32), _state_shape(bsz, sblock),
                   jax.ShapeDtypeStruct((1, GDN_CONV - 1, bsz, GDN_QKV), F32)],
        scratch_shapes=scratch,
        compiler_params=_params(),
        name="gdn_layer",
    )(h_tm, s0, _time_major(cb), w["g"], w["wqkv"], w["wz"], w["wab"], w["conv_w"], w["a_log"],
      w["dt_bias"], w["ng"], w["fg"], w["w_out"])
    return y, s_new, _time_major(cb_new)


_GLA_SPLITS = (GLA_QK, GLA_QK, GLA_V, GLA_V, GLA_RANK)
_GDN_SPLITS = (GDN_QKV, GDN_V, 2 * GDN_HEADS)
CAST_ROWS = 128


def _cast_kernel(n_in, splits, *refs):
    ins, outs = refs[:n_in], refs[n_in:]
    o = 0
    for ref, cols in zip(ins, splits):
        w = ref[...]
        transposed = len(cols) > 1
        lo = 0
        for width in cols:
            pad = (-width) % LANES
            if transposed:
                piece = w[lo:lo + width]
                if pad:
                    piece = jnp.concatenate([piece, jnp.zeros((pad, piece.shape[1]), piece.dtype)], axis=0)
                piece = piece.T
            else:
                piece = w[:, lo:lo + width]
                if pad:
                    piece = jnp.concatenate([piece, jnp.zeros((piece.shape[0], pad), piece.dtype)], axis=1)
            outs[o][...] = piece.astype(BF16)
            lo += width
            o += 1


def _cast_weights(weights, splits):
    k = weights[0].shape[1]
    out_widths = [wd + (-wd) % LANES for cols in splits for wd in cols]
    operands, in_specs = [], []
    for w, cols in zip(weights, splits):
        if len(cols) > 1:
            operands.append(jnp.swapaxes(w, 1, 2))
            in_specs.append(pl.BlockSpec((None, w.shape[2], CAST_ROWS), lambda i: (0, 0, i)))
        else:
            operands.append(w)
            in_specs.append(pl.BlockSpec((None, CAST_ROWS, w.shape[2]), lambda i: (0, i, 0)))
    return pl.pallas_call(
        functools.partial(_cast_kernel, len(weights), splits),
        grid=(k // CAST_ROWS,),
        in_specs=in_specs,
        out_specs=[pl.BlockSpec((CAST_ROWS, wd), lambda i: (i, 0)) for wd in out_widths],
        out_shape=[jax.ShapeDtypeStruct((k, wd), BF16) for wd in out_widths],
        compiler_params=pltpu.CompilerParams(dimension_semantics=("parallel",), vmem_limit_bytes=VMEM_LIMIT),
        name="cast_weights",
    )(*operands)


def _row(x, width=None):
    x = x.reshape(1, -1).astype(F32)
    if width is not None and x.shape[1] < width:
        x = jnp.pad(x, ((0, 0), (0, width - x.shape[1])))
    return x


def _s5_weights(j, norm_g, w_in, s5_b_re, s5_b_im, s5_c_re, s5_c_im, s5_d, s5_log_dt, s5_a_re,
                s5_a_im, w_glu, s5_b_glu, w_out):
    abr, abi, wb, wc = _s5_prep(s5_log_dt[j], s5_a_re[j], s5_a_im[j], s5_b_re[j], s5_b_im[j],
                                s5_c_re[j], s5_c_im[j])
    return dict(g=_row(norm_g), w_in=w_in, wb=wb, wc=wc, abr=_row(abr), abi=_row(abi), d=_row(s5_d[j]),
                w_glu=w_glu, b_glu=_row(s5_b_glu[j]), w_out=w_out)


def kernel(x_prompt, x_sample, state_pool, state_gla, state_s5_re, state_s5_im, state_gdn, state_gdn_conv, meta_tokens, norm_g, final_norm_g, pool_w_in, pool_w_grp, pool_scale, pool_w_out, gla_w_in, gla_w_gk, gla_b_gk, gla_norm_g, gla_w_out, s5_w_in, s5_b_re, s5_b_im, s5_c_re, s5_c_im, s5_d, s5_log_dt, s5_a_re, s5_a_im, s5_w_glu, s5_b_glu, s5_w_out, gdn_w_in, gdn_conv_w, gdn_a_log, gdn_dt_bias, gdn_norm_g, gdn_w_out):
    bp = x_prompt.shape[0]
    bs, ls, _ = x_sample.shape

    one = (D_MODEL,)
    (p_in, p_out, g_q, g_k, g_v, g_gate, g_low, g_out, s_in, s_glu, s_out, d_qkv, d_z, d_ab, d_out) = _cast_weights(
        [pool_w_in, pool_w_out, gla_w_in, gla_w_out, s5_w_in, s5_w_glu, s5_w_out, gdn_w_in, gdn_w_out],
        [(2 * D_MODEL,), one, _GLA_SPLITS, one, (2 * D_MODEL,), one, one, _GDN_SPLITS, one])
    wp = dict(g=_row(norm_g[0]), w_in=p_in, w_grp=pool_w_grp[0].astype(BF16),
              scale=_row(pool_scale[0]), w_out=p_out)
    wg = dict(g=_row(norm_g[1]), wq=g_q, wk=g_k, wv=g_v, wgate=g_gate, wglow=g_low,
              wgk=jnp.pad(gla_w_gk[0], ((0, LANES - GLA_RANK), (0, 0))).astype(BF16),
              bgk=_row(gla_b_gk[0]), ng=_row(gla_norm_g[0]), w_out=g_out)
    ws = _s5_weights(0, norm_g[2], s_in, s5_b_re, s5_b_im, s5_c_re, s5_c_im, s5_d, s5_log_dt,
                     s5_a_re, s5_a_im, s_glu, s5_b_glu, s_out)
    wd = dict(g=_row(norm_g[3]), wqkv=d_qkv, wz=d_z, wab=d_ab, conv_w=gdn_conv_w[0],
              a_log=_row(gdn_a_log[0], LANES), dt_bias=_row(gdn_dt_bias[0], LANES),
              ng=_row(gdn_norm_g[0]), fg=_row(final_norm_g), w_out=d_out)

    def run(h, pool_st, n_valid, gla_st, s5_st, gdn_st, conv_st, blocks):
        (pb, pt), (gb, gt), (sb, stl), (db, dtl, dn) = blocks
        bsz, seq, _ = h.shape
        h, pool_new = _pool_layer(h, pool_st, n_valid, wp, pb, pt)
        h_tm, gla_new = _gla_layer(h, gla_st, wg, gb, gt)
        h_tb, s5_new = _s5_layer(h_tm.reshape(seq, bsz, D_MODEL), s5_st, ws, sb, stl)
        if not _reorders_rows(db, dtl, seq):
            h_tb = h_tb.reshape(seq, bsz * D_MODEL)
        y, gdn_new, conv_new = _gdn_layer(h_tb, gdn_st, conv_st, wd, db, dtl, dn)
        return y, pool_new, gla_new, s5_new, gdn_new, conv_new

    hm = meta_tokens.astype(F32)[None]
    zeros = lambda *s: jnp.zeros(s, F32)
    hm, m_pool = _pool_layer(hm, zeros(1, 1, POOL_BUF, D_MODEL), 0, wp, 1, N_META)
    hm, m_gla = _gla_layer(hm, zeros(1, 1, GLA_HEADS, GLA_DK, GLA_DV), wg, 1, N_META)
    hm_tb, m_s5 = _s5_layer(jnp.broadcast_to(hm[:, None, :], (N_META, SUBLANES, D_MODEL)),
                            zeros(SUBLANES, S5_STATE), ws, SUBLANES, N_META)
    _, m_gdn, m_conv = _gdn_layer(hm_tb[:, 0], zeros(1, 1, GDN_HEADS, GDN_DK, GDN_DV),
                                  zeros(1, 1, GDN_CONV - 1, GDN_QKV), wd, 1, N_META, 1)

    yp, pool_p, gla_p, s5_p, gdn_p, conv_p = run(
        x_prompt, m_pool, N_META, m_gla, jnp.broadcast_to(m_s5[0:1], (bp, S5_STATE)), m_gdn, m_conv,
        ((1, 512), (8, 64), (bp, 64), (8, 64, 1)))
    ys, pool_s, gla_s, s5_s, gdn_s, conv_s = run(
        x_sample, state_pool, POOL_BUF, state_gla, _s5_pack_state(state_s5_re[0], state_s5_im[0]),
        state_gdn, state_gdn_conv, ((32, ls), (16, ls), (32, ls), (16, ls, 8)))

    s5r_p, s5i_p = _s5_unpack_state(s5_p)
    s5r_s, s5i_s = _s5_unpack_state(s5_s)
    return (yp, ys, pool_p, pool_s, gla_p, gla_s, s5r_p[None], s5i_p[None], s5r_s[None], s5i_s[None],
            gdn_p, conv_p, gdn_s, conv_s)
```
